```python
import math
import jax
import jax.numpy as jnp
from jax import lax
import numpy as np

D_MODEL = 1024
BATCH = 1
SEQ = 16384
DEPTH = 4
DEC_BATCH = 8
DEC_SEQ = 4096
PAST_LEN = 128

A_HEADS = 4
A_HEAD_DIM = 128
A_WIDTH = A_HEADS * A_HEAD_DIM
A_CHUNK = 64
B_Q_HEADS = 8
B_KV_HEADS = 2
B_HEAD_DIM = 64
B_WIDTH = B_Q_HEADS * B_HEAD_DIM
B_KV_WIDTH = B_KV_HEADS * B_HEAD_DIM
WINDOW = 128
B_BLOCK = 128
ROPE_THETA = 10000.0
C_HEADS = 4
C_HEAD_DIM = 128
C_WIDTH = C_HEADS * C_HEAD_DIM
C_CHUNK = 64
CONV_K = 5
MEM_TOKENS = 256
MEM_HEADS = 4
MEM_HEAD_DIM = D_MODEL // MEM_HEADS
FFN_HIDDEN = ((8 * D_MODEL + 3 * 256 - 1) // (3 * 256)) * 256
DN_ALPHA = (2 * DEPTH) ** 0.25
DN_BETA = (8 * DEPTH) ** -0.25
SPLIT_SIZES = (A_WIDTH, A_WIDTH, A_WIDTH, A_WIDTH, A_WIDTH,
               B_WIDTH, B_KV_WIDTH, B_KV_WIDTH,
               3 * C_WIDTH, C_WIDTH, 2 * C_HEADS, 2 * C_HEADS,
               3 * D_MODEL)
IN_COLS = sum(SPLIT_SIZES)

kernel_name = 'hybrid_bidir_hgrn2_swa_gdn_encoder'


def split_points():
    pts = []
    acc = 0
    for s in SPLIT_SIZES[:-1]:
        acc += s
        pts.append(acc)
    return pts


def layer_norm(x, g, b, eps=1e-5):
    xf = x.astype(jnp.float32)
    mu = jnp.mean(xf, -1, keepdims=True)
    var = jnp.mean(jnp.square(xf - mu), -1, keepdims=True)
    return ((xf - mu) * lax.rsqrt(var + eps) * g.astype(jnp.float32) + b.astype(jnp.float32)).astype(x.dtype)


def rms_norm(x, g, eps=1e-6):
    xf = x.astype(jnp.float32)
    return xf * lax.rsqrt(jnp.mean(xf * xf, -1, keepdims=True) + eps) * g.astype(jnp.float32)


def l2_normalize(x, eps=1e-6):
    return x * lax.rsqrt(jnp.sum(x * x, -1, keepdims=True) + eps)


def rope_tables(seq_len):
    inv = ROPE_THETA ** (-jnp.arange(0, B_HEAD_DIM, 2, dtype=jnp.float32) / B_HEAD_DIM)
    ang = jnp.arange(seq_len, dtype=jnp.float32)[:, None] * inv[None, :]
    return jnp.cos(ang), jnp.sin(ang)


def apply_rope(x, cos, sin):
    x1, x2 = jnp.split(x.astype(jnp.float32), 2, axis=-1)
    c = cos[None, :, None, :]
    s = sin[None, :, None, :]
    return jnp.concatenate([x1 * c - x2 * s, x2 * c + x1 * s], axis=-1)


def hgrn_lower_bounds(logits):
    cum = jnp.cumsum(jax.nn.softmax(logits.astype(jnp.float32), axis=1), axis=1)
    return cum - cum[:, :1]


def gla_chunk(q, k, v, log_f):
    B, H, S, dk = q.shape
    dv = v.shape[-1]
    C = A_CHUNK
    n = S // C
    qc = q.reshape(B, H, n, C, dk)
    kc = k.reshape(B, H, n, C, dk)
    vc = v.reshape(B, H, n, C, dv)
    gc = jnp.cumsum(log_f.reshape(B, H, n, C, dk), axis=3)
    q_dec = qc * jnp.exp(gc)
    k_dec = kc * jnp.exp(gc[:, :, :, -1:] - gc)
    g_last = jnp.exp(gc[:, :, :, -1])
    incl = jnp.tril(jnp.ones((C, C), dtype=bool))[:, :, None]

    def step(state, inp):
        q_i, k_i, v_i, g_i, qd_i, kd_i, gl_i = inp
        rel = jnp.where(incl, g_i[:, :, :, None, :] - g_i[:, :, None, :, :], -jnp.inf)
        scores = jnp.sum(q_i[:, :, :, None, :] * k_i[:, :, None, :, :] * jnp.exp(rel), axis=-1)
        out = jnp.einsum('bhij,bhjv->bhiv', scores, v_i) + jnp.einsum('bhid,bhdv->bhiv', qd_i, state)
        state = state * gl_i[..., None] + jnp.einsum('bhjd,bhjv->bhdv', kd_i, v_i)
        return state, out

    xs = tuple(jnp.moveaxis(t, 2, 0) for t in (qc, kc, vc, gc, q_dec, k_dec, g_last))
    _, out = lax.scan(step, jnp.zeros((B, H, dk, dv), jnp.float32), xs)
    return jnp.moveaxis(out, 0, 2).reshape(B, H, S, dv)


def hgrn2_branch(q_in, f_fwd_in, f_bwd_in, i_in, g_in, lb_fwd, lb_bwd, norm_g):
    B, S, _ = q_in.shape

    def heads(t):
        return jnp.swapaxes(t.astype(jnp.float32).reshape(B, S, A_HEADS, A_HEAD_DIM), 1, 2)

    q = jax.nn.silu(heads(q_in)) * A_HEAD_DIM ** -0.5
    v = heads(i_in)

    def gate_terms(f_in, lb):
        z = heads(f_in)
        lb = lb.reshape(A_HEADS, 1, A_HEAD_DIM)
        log_f = jnp.logaddexp(jnp.log(lb), jnp.log1p(-lb) + jax.nn.log_sigmoid(z))
        key = (1.0 - lb) * jax.nn.sigmoid(-z)
        return key, log_f

    k_f, lf_f = gate_terms(f_fwd_in, lb_fwd)
    k_b, lf_b = gate_terms(f_bwd_in, lb_bwd)
    flip = lambda t: jnp.flip(t, axis=2)
    o = gla_chunk(q, k_f, v, lf_f) + flip(gla_chunk(flip(q), flip(k_b), flip(v), flip(lf_b)))
    o = jnp.swapaxes(o, 1, 2)
    gate = g_in.astype(jnp.float32).reshape(B, S, A_HEADS, A_HEAD_DIM)
    o = rms_norm(o, norm_g) * jax.nn.silu(gate)
    return o.reshape(B, S, A_WIDTH).astype(q_in.dtype)


def window_attention(q_in, k_in, v_in, sink, cos, sin):
    B, S, _ = q_in.shape
    nb = S // B_BLOCK
    G = B_Q_HEADS // B_KV_HEADS
    q = apply_rope(q_in.reshape(B, S, B_Q_HEADS, B_HEAD_DIM), cos, sin)
    k = apply_rope(k_in.reshape(B, S, B_KV_HEADS, B_HEAD_DIM), cos, sin)
    v = v_in.astype(jnp.float32).reshape(B, S, B_KV_HEADS, B_HEAD_DIM)
    qb = q.reshape(B, nb, B_BLOCK, B_KV_HEADS, G, B_HEAD_DIM)
    pad = ((0, 0), (B_BLOCK, B_BLOCK), (0, 0), (0, 0))
    kp = jnp.pad(k, pad).reshape(B, nb + 2, B_BLOCK, B_KV_HEADS, B_HEAD_DIM)
    vp = jnp.pad(v, pad).reshape(B, nb + 2, B_BLOCK, B_KV_HEADS, B_HEAD_DIM)
    kw = jnp.concatenate([kp[:, :-2], kp[:, 1:-1], kp[:, 2:]], axis=2)
    vw = jnp.concatenate([vp[:, :-2], vp[:, 1:-1], vp[:, 2:]], axis=2)
    s = jnp.einsum('bnqhgd,bnkhd->bnhgqk', qb, kw) * B_HEAD_DIM ** -0.5
    blk = jnp.arange(nb)[:, None, None]
    qpos = blk * B_BLOCK + jnp.arange(B_BLOCK)[None, :, None]
    kpos = (blk - 1) * B_BLOCK + jnp.arange(3 * B_BLOCK)[None, None, :]
    mask = (jnp.abs(qpos - kpos) <= WINDOW) & (kpos >= 0) & (kpos < S)
    s = jnp.where(mask[None, :, None, None], s, -jnp.inf)
    sink_l = sink.astype(jnp.float32).reshape(B_KV_HEADS, G)[None, None, :, :, None, None]
    m = jnp.maximum(jnp.max(s, -1, keepdims=True), sink_l)
    p = jnp.exp(s - m)
    denom = jnp.sum(p, -1, keepdims=True) + jnp.exp(sink_l - m)
    o = jnp.einsum('bnhgqk,bnkhd->bnqhgd', p / denom, vw)
    return o.reshape(B, S, B_WIDTH).astype(q_in.dtype)


def depthwise_conv(x, w):
    return lax.conv_general_dilated(x, w[:, None, :], window_strides=(1,),
                                    padding=[(CONV_K // 2, CONV_K // 2)],
                                    dimension_numbers=('NWC', 'WIO', 'NWC'),
                                    feature_group_count=x.shape[-1])


def gated_delta_chunk(q, k, v, g, beta):
    B, H, S, dk = q.shape
    dv = v.shape[-1]
    C = C_CHUNK
    n = S // C
    qc = q.reshape(B, H, n, C, dk)
    kc = k.reshape(B, H, n, C, dk)
    vc = v.reshape(B, H, n, C, dv)
    bc = beta.reshape(B, H, n, C)
    gc = jnp.cumsum(g.reshape(B, H, n, C), axis=-1)
    incl = jnp.tril(jnp.ones((C, C), dtype=bool))
    strict = jnp.tril(jnp.ones((C, C), dtype=bool), -1)
    decay = jnp.exp(jnp.where(incl, gc[..., :, None] - gc[..., None, :], -jnp.inf))
    kk = jnp.einsum('bhnid,bhnjd->bhnij', kc, kc)
    a_strict = jnp.where(strict, kk * decay * bc[..., :, None], 0.0)
    rhs = jnp.concatenate([vc * bc[..., None], kc * (bc * jnp.exp(gc))[..., None]], axis=-1)
    sol = lax.linalg.triangular_solve(a_strict + jnp.eye(C, dtype=a_strict.dtype), rhs,
                                      left_side=True, lower=True)
    u = sol[..., :dv]
    w = sol[..., dv:]
    qk = jnp.einsum('bhnid,bhnjd->bhnij', qc, kc) * decay
    q_dec = qc * jnp.exp(gc)[..., None]
    k_dec = kc * jnp.exp(gc[..., -1:] - gc)[..., None]
    g_last = jnp.exp(gc[..., -1])

    def step(state, inp):
        u_i, w_i, qk_i, qd_i, kd_i, gl_i = inp
        v_new = u_i - jnp.einsum('bhcd,bhdv->bhcv', w_i, state)
        out = jnp.einsum('bhcd,bhdv->bhcv', qd_i, state) + jnp.einsum('bhij,bhjv->bhiv', qk_i, v_new)
        state = state * gl_i[..., None, None] + jnp.einsum('bhcd,bhcv->bhdv', kd_i, v_new)
        return state, out

    xs = tuple(jnp.moveaxis(t, 2, 0) for t in (u, w, qk, q_dec, k_dec, g_last))
    _, out = lax.scan(step, jnp.zeros((B, H, dk, dv), jnp.float32), xs)
    return jnp.moveaxis(out, 0, 2).reshape(B, H, S, dv)


def gated_deltanet_branch(qkv_in, gate_in, beta_in, a_in, conv_w, a_log, dt_bias, norm_g):
    B, S, _ = qkv_in.shape
    qkv = jax.nn.silu(depthwise_conv(qkv_in, conv_w).astype(jnp.float32))
    q, k, v = jnp.split(qkv, 3, axis=-1)

    def heads(t):
        return jnp.swapaxes(t.reshape(B, S, C_HEADS, C_HEAD_DIM), 1, 2)

    q = l2_normalize(heads(q)) * C_HEAD_DIM ** -0.5
    k = l2_normalize(heads(k))
    v = heads(v)
    beta = jax.nn.sigmoid(beta_in.astype(jnp.float32).reshape(B, S, 2, C_HEADS))
    g = -jnp.exp(a_log.astype(jnp.float32)) * jax.nn.softplus(
        a_in.astype(jnp.float32).reshape(B, S, 2, C_HEADS) + dt_bias.astype(jnp.float32))
    beta = jnp.transpose(beta, (2, 0, 3, 1))
    g = jnp.transpose(g, (2, 0, 3, 1))
    flip = lambda t: jnp.flip(t, axis=2)
    o_f = gated_delta_chunk(q, k, v, g[0], beta[0])
    o_b = flip(gated_delta_chunk(flip(q), flip(k), flip(v), flip(g[1]), flip(beta[1])))
    o = jnp.swapaxes(o_f + o_b, 1, 2)
    gate = gate_in.astype(jnp.float32).reshape(B, S, C_HEADS, C_HEAD_DIM)
    o = rms_norm(o, norm_g) * jax.nn.silu(gate)
    return o.reshape(B, S, C_WIDTH).astype(qkv_in.dtype)


def memory_cross_attention(x, mem, w_q, w_kv, w_o):
    B, S, _ = x.shape
    M = mem.shape[1]
    q = (x @ w_q).astype(jnp.float32).reshape(B, S, MEM_HEADS, MEM_HEAD_DIM)
    k, v = jnp.split((mem @ w_kv).astype(jnp.float32), 2, axis=-1)
    k = k.reshape(B, M, MEM_HEADS, MEM_HEAD_DIM)
    v = v.reshape(B, M, MEM_HEADS, MEM_HEAD_DIM)
    p = jax.nn.softmax(jnp.einsum('bshd,bmhd->bhsm', q, k) * MEM_HEAD_DIM ** -0.5, axis=-1)
    o = jnp.einsum('bhsm,bmhd->bshd', p, v).reshape(B, S, D_MODEL).astype(x.dtype)
    return o @ w_o


def swiglu(x, w_in, w_out):
    gate, up = jnp.split(x @ w_in, 2, axis=-1)
    return (jax.nn.silu(gate) * up) @ w_out


def trunk(x, mem, w_in, hgrn_lb_logits, hgrn_norm_g, attn_sink, gdn_conv_w, gdn_a_log, gdn_dt_bias,
          gdn_norm_g, w_branch_a, w_branch_b, w_branch_c, w_mix_out, w_mem_q, w_mem_kv, w_mem_o,
          w_ffn_in, w_ffn_out, ln_g, ln_b):
    cos, sin = rope_tables(x.shape[1])
    lb = hgrn_lower_bounds(hgrn_lb_logits)
    pts = split_points()
    for l in range(DEPTH):
        h = x @ w_in[l]
        (a_q, a_f_fwd, a_f_bwd, a_i, a_g, b_q, b_k, b_v,
         c_qkv, c_gate, c_beta, c_a, merge) = jnp.split(h, pts, axis=-1)
        o_a = hgrn2_branch(a_q, a_f_fwd, a_f_bwd, a_i, a_g, lb[0, l], lb[1, l], hgrn_norm_g[l])
        o_b = window_attention(b_q, b_k, b_v, attn_sink[l], cos, sin)
        o_c = gated_deltanet_branch(c_qkv, c_gate, c_beta, c_a, gdn_conv_w[l], gdn_a_log[l],
                                    gdn_dt_bias[l], gdn_norm_g[l])
        g_a, g_b, g_c = jnp.split(jax.nn.sigmoid(merge), 3, axis=-1)
        mix = g_a * (o_a @ w_branch_a[l]) + g_b * (o_b @ w_branch_b[l]) + g_c * (o_c @ w_branch_c[l])
        x = layer_norm(DN_ALPHA * x + mix @ w_mix_out[l], ln_g[l, 0], ln_b[l, 0])
        x = layer_norm(DN_ALPHA * x + memory_cross_attention(x, mem, w_mem_q[l], w_mem_kv[l], w_mem_o[l]),
                       ln_g[l, 1], ln_b[l, 1])
        x = layer_norm(DN_ALPHA * x + swiglu(x, w_ffn_in[l], w_ffn_out[l]), ln_g[l, 2], ln_b[l, 2])
    return x


def setup_inputs(seed: int = 0) -> dict:
    key = jax.random.key(seed)
    ks = jax.random.split(key, 24)
    nrm = lambda k, shape, scale: jax.random.normal(k, shape, jnp.float32) * scale
    dt = jnp.exp(jax.random.uniform(ks[8], (DEPTH, 2, C_HEADS), jnp.float32,
                                    minval=math.log(1e-3), maxval=math.log(1e-1)))
    return {
        'x_prompt': nrm(ks[0], (BATCH, SEQ, D_MODEL), 1.0),
        'x_sample': nrm(ks[1], (DEC_BATCH, DEC_SEQ, D_MODEL), 1.0),
        'mem_prompt': nrm(ks[2], (BATCH, MEM_TOKENS, D_MODEL), 1.0),
        'mem_sample': nrm(ks[3], (DEC_BATCH, MEM_TOKENS, D_MODEL), 1.0),
        'w_in': nrm(ks[4], (DEPTH, D_MODEL, IN_COLS), D_MODEL ** -0.5),
        'hgrn_lb_logits': nrm(ks[5], (2, DEPTH, A_WIDTH), 0.5),
        'hgrn_norm_g': 1.0 + nrm(ks[6], (DEPTH, A_HEAD_DIM), 0.02),
        'attn_sink': nrm(ks[7], (DEPTH, B_Q_HEADS), 0.5),
        'gdn_conv_w': nrm(ks[9], (DEPTH, CONV_K, 3 * C_WIDTH), CONV_K ** -0.5),
        'gdn_a_log': jnp.log(jax.random.uniform(ks[10], (DEPTH, 2, C_HEADS), jnp.float32, minval=1.0, maxval=16.0)),
        'gdn_dt_bias': dt + jnp.log(-jnp.expm1(-dt)),
        'gdn_norm_g': 1.0 + nrm(ks[11], (DEPTH, C_HEAD_DIM), 0.02),
        'w_branch_a': nrm(ks[12], (DEPTH, A_WIDTH, D_MODEL), A_WIDTH ** -0.5),
        'w_branch_b': nrm(ks[13], (DEPTH, B_WIDTH, D_MODEL), B_WIDTH ** -0.5),
        'w_branch_c': nrm(ks[14], (DEPTH, C_WIDTH, D_MODEL), C_WIDTH ** -0.5),
        'w_mix_out': nrm(ks[15], (DEPTH, D_MODEL, D_MODEL), DN_BETA * D_MODEL ** -0.5),
        'w_mem_q': nrm(ks[16], (DEPTH, D_MODEL, D_MODEL), D_MODEL ** -0.5),
        'w_mem_kv': nrm(ks[17], (DEPTH, D_MODEL, 2 * D_MODEL), D_MODEL ** -0.5),
        'w_mem_o': nrm(ks[18], (DEPTH, D_MODEL, D_MODEL), DN_BETA * D_MODEL ** -0.5),
        'w_ffn_in': nrm(ks[19], (DEPTH, D_MODEL, 2 * FFN_HIDDEN), D_MODEL ** -0.5),
        'w_ffn_out': nrm(ks[20], (DEPTH, FFN_HIDDEN, D_MODEL), DN_BETA * FFN_HIDDEN ** -0.5),
        'ln_g': 1.0 + nrm(ks[21], (DEPTH, 3, D_MODEL), 0.02),
        'ln_b': nrm(ks[22], (DEPTH, 3, D_MODEL), 0.02),
    }


def reference(x_prompt, x_sample, mem_prompt, mem_sample, w_in, hgrn_lb_logits, hgrn_norm_g, attn_sink,
              gdn_conv_w, gdn_a_log, gdn_dt_bias, gdn_norm_g, w_branch_a, w_branch_b, w_branch_c, w_mix_out,
              w_mem_q, w_mem_kv, w_mem_o, w_ffn_in, w_ffn_out, ln_g, ln_b):
    y_prompt = trunk(x_prompt, mem_prompt, w_in, hgrn_lb_logits, hgrn_norm_g, attn_sink, gdn_conv_w,
                     gdn_a_log, gdn_dt_bias, gdn_norm_g, w_branch_a, w_branch_b, w_branch_c, w_mix_out,
                     w_mem_q, w_mem_kv, w_mem_o, w_ffn_in, w_ffn_out, ln_g, ln_b)
    y_sample = trunk(x_sample, mem_sample, w_in, hgrn_lb_logits, hgrn_norm_g, attn_sink, gdn_conv_w,
                     gdn_a_log, gdn_dt_bias, gdn_norm_g, w_branch_a, w_branch_b, w_branch_c, w_mix_out,
                     w_mem_q, w_mem_kv, w_mem_o, w_ffn_in, w_ffn_out, ln_g, ln_b)
    return (y_prompt, y_sample)
```

```python
import functools
import math

import numpy as np
import jax
import jax.numpy as jnp
from jax import lax
from jax.experimental import pallas as pl
from jax.experimental.pallas import tpu as pltpu

D_MODEL = 1024
DEPTH = 4
HEADS = 4
HEAD_DIM = 128
WIDTH = HEADS * HEAD_DIM
B_Q_HEADS = 8
B_KV_HEADS = 2
B_GROUP = B_Q_HEADS // B_KV_HEADS
B_HEAD_DIM = 64
WINDOW = 128
B_BLOCK = 128
ROPE_THETA = 10000.0
CONV_K = 5
MEM_HEADS = 4
MEM_HEAD_DIM = D_MODEL // MEM_HEADS
FFN_HIDDEN = 2816
DN_ALPHA = (2 * DEPTH) ** 0.25

CHUNK = 64
SUB = 8
LANES = 128
VMEM_LIMIT = 56 * 1024 * 1024

COL_A_Q, COL_A_FF, COL_A_FB, COL_A_I, COL_A_G = 0, 512, 1024, 1536, 2048
COL_B_Q = 2560
COL_C_Q, COL_C_K, COL_C_V = 3072, 3584, 4096
COL_C_GATE = 4608
COL_MERGE = 5120
COL_B_K, COL_B_V = 8192, 8320
COL_GB = 8448
IN_COLS_PAD = 8704

_f32 = jnp.float32
_bf16 = jnp.bfloat16


def _dot(a, b):
    return jnp.dot(a.astype(_bf16), b.astype(_bf16), preferred_element_type=_f32)


def _dot_nt(a, b):
    return lax.dot_general(a.astype(_bf16), b.astype(_bf16), (((1,), (1,)), ((), ())),
                           preferred_element_type=_f32)


def _dot_tn(a, b):
    return lax.dot_general(a.astype(_bf16), b.astype(_bf16), (((0,), (0,)), ((), ())),
                           preferred_element_type=_f32)


def _split3(x):
    hi = x.astype(_bf16)
    r1 = x - hi.astype(_f32)
    mid = r1.astype(_bf16)
    lo = (r1 - mid.astype(_f32)).astype(_bf16)
    return hi, mid, lo


def _tri_cumsum(tri, x):
    hi, mid, lo = _split3(x)
    dot = lambda p: jnp.dot(tri, p, preferred_element_type=_f32)
    return dot(hi) + dot(mid) + dot(lo)


def _silu(x):
    return x * jax.nn.sigmoid(x)


def _iota2(shape, dim):
    return lax.broadcasted_iota(jnp.int32, shape, dim)


def _params(sem):
    return pltpu.CompilerParams(dimension_semantics=sem, vmem_limit_bytes=VMEM_LIMIT)


def _matmul_body(x_ref, w_ref, o_ref, xb_ref):
    @pl.when(pl.program_id(1) == 0)
    def _():
        xb_ref[...] = x_ref[...].astype(_bf16)

    o_ref[...] = jnp.dot(xb_ref[...], w_ref[...], preferred_element_type=_f32)


def _matmul(x, w, tm, tn):
    t, k = x.shape
    n = w.shape[1]
    return pl.pallas_call(
        _matmul_body,
        grid=(t // tm, n // tn),
        in_specs=[pl.BlockSpec((tm, k), lambda i, j: (i, 0)),
                  pl.BlockSpec((k, tn), lambda i, j: (0, j))],
        out_specs=pl.BlockSpec((tm, tn), lambda i, j: (i, j)),
        out_shape=jax.ShapeDtypeStruct((t, n), _f32),
        scratch_shapes=[pltpu.VMEM((tm, k), _bf16)],
        compiler_params=_params(("parallel", "arbitrary")),
        name="dense_proj",
    )(x, w)


def _chunk_masks(c, reverse):
    row = _iota2((c, c), 0)
    col = _iota2((c, c), 1)
    if reverse:
        return col >= row, col > row
    return col <= row, col < row


def _level_mask(c, blk, reverse):
    row = _iota2((c, c), 0)
    col = _iota2((c, c), 1)
    half = blk // 2
    same = (row // blk) == (col // blk)
    r_hi = (row % blk) >= half
    c_hi = (col % blk) >= half
    if reverse:
        return same & jnp.logical_not(r_hi) & c_hi
    return same & r_hi & jnp.logical_not(c_hi)


def _block_ref_rows(g, blk, reverse):
    c, n = g.shape
    idx = blk // 2 if reverse else blk // 2 - 1
    g3 = g.reshape(c // blk, blk, n)
    ref = g3[:, idx:idx + 1, :]
    return jnp.broadcast_to(ref, (c // blk, blk, n)).reshape(c, n)


def _shift_rows(x, d, reverse):
    c = x.shape[0]
    return pltpu.roll(x, (c - d) if reverse else d, 0)


def _hgrn_body(first_ref, zq_ref, zf_ref, zi_ref, lb_ref, o_ref, st_ref, *, reverse):
    c = CHUNK
    step = pl.program_id(0)

    @pl.when(first_ref[step] == 1)
    def _():
        st_ref[...] = jnp.zeros_like(st_ref)

    incl, _ = _chunk_masks(c, reverse)
    tri = incl.astype(_bf16)
    levels = []
    blk = c
    while blk // 2 >= SUB:
        levels.append((blk, _level_mask(c, blk, reverse)))
        blk //= 2
    sub_pos = _iota2((c, 1), 0) % SUB
    tot_row = 0 if reverse else c - 1

    zf = zf_ref[...]
    log_lb = lb_ref[0:1, :]
    log_1m_lb = lb_ref[1:2, :]
    one_m_lb = lb_ref[2:3, :]
    log_sig = jnp.minimum(zf, 0.0) - jnp.log1p(jnp.exp(-jnp.abs(zf)))
    b = log_1m_lb + log_sig
    log_f = jnp.maximum(log_lb, b) + jnp.log1p(jnp.exp(-jnp.abs(log_lb - b)))
    key_all = one_m_lb * jax.nn.sigmoid(-zf)
    gc_all = _tri_cumsum(tri, log_f)
    q_all = _silu(zq_ref[...]) * (HEAD_DIM ** -0.5)
    v_all = zi_ref[...]

    for h in range(HEADS):
        sl = slice(h * HEAD_DIM, (h + 1) * HEAD_DIM)
        q, k, v, gc = q_all[:, sl], key_all[:, sl], v_all[:, sl], gc_all[:, sl]
        st = st_ref[h]
        g_tot = gc[tot_row:tot_row + 1, :]

        scores = jnp.zeros((c, c), _f32)
        for blk, mask in levels:
            e = jnp.exp(-jnp.abs(gc - _block_ref_rows(gc, blk, reverse)))
            scores = scores + jnp.where(mask, _dot_nt(q * e, k * e), 0.0)
        out = _dot(scores, v)
        out = out + jnp.sum(q * k, axis=1, keepdims=True) * v
        for d in range(1, SUB):
            ks = _shift_rows(k, d, reverse)
            gs = _shift_rows(gc, d, reverse)
            vs = _shift_rows(v, d, reverse)
            s = jnp.sum(q * ks * jnp.exp(jnp.minimum(gc - gs, 0.0)), axis=1, keepdims=True)
            ok = (sub_pos + d < SUB) if reverse else (sub_pos >= d)
            out = out + jnp.where(ok, s, 0.0) * vs
        out = out + _dot_nt(q * jnp.exp(gc), st)
        st_ref[h] = st * jnp.exp(g_tot) + _dot_tn(v, k * jnp.exp(g_tot - gc))
        o_ref[:, sl] = out


def _hgrn(h, lb_consts, first, reverse, col_f):
    t = h.shape[0]
    n = t // CHUNK
    cmap = (lambda i, f: n - 1 - i) if reverse else (lambda i, f: i)
    spec = lambda col: pl.BlockSpec((CHUNK, WIDTH), lambda i, f: (cmap(i, f), col // WIDTH))
    return pl.pallas_call(
        functools.partial(_hgrn_body, reverse=reverse),
        grid_spec=pltpu.PrefetchScalarGridSpec(
            num_scalar_prefetch=1,
            grid=(n,),
            in_specs=[spec(COL_A_Q), spec(col_f), spec(COL_A_I),
                      pl.BlockSpec((8, WIDTH), lambda i, f: (0, 0))],
            out_specs=pl.BlockSpec((CHUNK, WIDTH), lambda i, f: (cmap(i, f), 0)),
            scratch_shapes=[pltpu.VMEM((HEADS, HEAD_DIM, HEAD_DIM), _f32)]),
        out_shape=jax.ShapeDtypeStruct((t, WIDTH), _f32),
        compiler_params=_params(("arbitrary",)),
        name="hgrn_bwd" if reverse else "hgrn_fwd",
    )(first, h, h, h, lb_consts)


def _gdn_prep_body(first_ref, last_ref, prev_ref, cur_ref, next_ref, w_ref, o_ref, *, tm):
    i = pl.program_id(0)
    halo = CONV_K // 2
    cur = cur_ref[...]
    prev = jnp.where(first_ref[i] == 1, 0.0, prev_ref[...])
    nxt = jnp.where(last_ref[i] == 1, 0.0, next_ref[...])
    ext = jnp.concatenate([prev, cur, nxt], axis=0)
    acc = jnp.zeros_like(cur)
    for j in range(CONV_K):
        off = 8 - halo + j
        acc = acc + ext[off:off + tm, :] * w_ref[j:j + 1, :]
    y = _silu(acc)
    for part in range(3):
        for h in range(HEADS):
            lo = part * WIDTH + h * HEAD_DIM
            x = y[:, lo:lo + HEAD_DIM]
            if part < 2:
                x = x * lax.rsqrt(jnp.sum(x * x, axis=1, keepdims=True) + 1e-6)
            if part == 0:
                x = x * (HEAD_DIM ** -0.5)
            o_ref[:, lo:lo + HEAD_DIM] = x


def _gdn_prep(h, conv_w, first, last, tm):
    t = h.shape[0]
    n = t // tm
    r8 = tm // 8
    cb = COL_C_Q // (3 * WIDTH)
    return pl.pallas_call(
        functools.partial(_gdn_prep_body, tm=tm),
        grid_spec=pltpu.PrefetchScalarGridSpec(
            num_scalar_prefetch=2,
            grid=(n,),
            in_specs=[pl.BlockSpec((8, 3 * WIDTH), lambda i, f, l: (jnp.maximum(i * r8 - 1, 0), cb)),
                      pl.BlockSpec((tm, 3 * WIDTH), lambda i, f, l: (i, cb)),
                      pl.BlockSpec((8, 3 * WIDTH), lambda i, f, l: (jnp.minimum((i + 1) * r8, n * r8 - 1), cb)),
                      pl.BlockSpec((8, 3 * WIDTH), lambda i, f, l: (0, 0))],
            out_specs=pl.BlockSpec((tm, 3 * WIDTH), lambda i, f, l: (i, 0))),
        out_shape=jax.ShapeDtypeStruct((t, 3 * WIDTH), _f32),
        compiler_params=_params(("parallel",)),
        name="gdn_prep",
    )(first, last, h, h, h, conv_w)


def _gdn_body(first_ref, q_ref, k_ref, v_ref, gb_ref, cst_ref, o_ref, st_ref, *, reverse):
    c = CHUNK
    step = pl.program_id(0)
    d = 1 if reverse else 0

    @pl.when(first_ref[step] == 1)
    def _():
        st_ref[...] = jnp.zeros_like(st_ref)

    incl, strict = _chunk_masks(c, reverse)
    tri = incl.astype(_bf16)
    row = _iota2((c, c), 0)
    col = _iota2((c, c), 1)
    eye = (row == col).astype(_f32)
    tot_row = 0 if reverse else c - 1

    gb = gb_ref[...]
    neg_a = cst_ref[0:1, :]
    dt_bias = cst_ref[1:2, :]
    beta_t = jax.nn.sigmoid(gb)
    zz = gb + dt_bias
    softplus = jnp.maximum(zz, 0.0) + jnp.log1p(jnp.exp(-jnp.abs(zz)))
    gc_t = _tri_cumsum(tri, neg_a * softplus)
    gc_tt = jnp.transpose(gc_t)

    for h in range(HEADS):
        sl = slice(h * HEAD_DIM, (h + 1) * HEAD_DIM)
        q, k, v = q_ref[:, sl], k_ref[:, sl], v_ref[:, sl]
        st = st_ref[h]
        cg = 2 * HEADS + d * HEADS + h
        cbeta = d * HEADS + h
        g_col = gc_t[:, cg:cg + 1]
        g_row = gc_tt[cg:cg + 1, :]
        beta = beta_t[:, cbeta:cbeta + 1]
        g_tot = gc_t[tot_row:tot_row + 1, cg:cg + 1]

        decay = jnp.where(incl, jnp.exp(jnp.minimum(g_col - g_row, 0.0)), 0.0)
        a = jnp.where(strict, _dot_nt(k, k) * decay * beta, 0.0)
        inv = eye - jnp.where(_level_mask(c, 2, reverse), a, 0.0)
        s = 2
        while s < c:
            off = jnp.where(_level_mask(c, 2 * s, reverse), a, 0.0)
            inv = inv - _dot(inv, _dot(off, inv))
            s *= 2
        e_g = jnp.exp(g_col)
        u = _dot(inv, v * beta)
        w = _dot(inv, k * (beta * e_g))
        v_new = u - _dot_nt(w, st)
        out = _dot_nt(q * e_g, st) + _dot(_dot_nt(q, k) * decay, v_new)
        st_ref[h] = st * jnp.exp(g_tot) + _dot_tn(v_new, k * jnp.exp(g_tot - g_col))
        o_ref[:, sl] = out


def _gdn(qkv, h, consts, first, reverse):
    t = qkv.shape[0]
    n = t // CHUNK
    cmap = (lambda i, f: n - 1 - i) if reverse else (lambda i, f: i)
    spec = lambda part: pl.BlockSpec((CHUNK, WIDTH), lambda i, f: (cmap(i, f), part))
    return pl.pallas_call(
        functools.partial(_gdn_body, reverse=reverse),
        grid_spec=pltpu.PrefetchScalarGridSpec(
            num_scalar_prefetch=1,
            grid=(n,),
            in_specs=[spec(0), spec(1), spec(2),
                      pl.BlockSpec((CHUNK, LANES), lambda i, f: (cmap(i, f), COL_GB // LANES)),
                      pl.BlockSpec((8, LANES), lambda i, f: (0, 0))],
            out_specs=pl.BlockSpec((CHUNK, WIDTH), lambda i, f: (cmap(i, f), 0)),
            scratch_shapes=[pltpu.VMEM((HEADS, HEAD_DIM, HEAD_DIM), _f32)]),
        out_shape=jax.ShapeDtypeStruct((t, WIDTH), _f32),
        compiler_params=_params(("arbitrary",)),
        name="gdn_bwd" if reverse else "gdn_fwd",
    )(first, qkv, qkv, qkv, h, consts)


def _rope(x, cos, sin_signed):
    lane = _iota2(x.shape, 1) % B_HEAD_DIM
    half = B_HEAD_DIM // 2
    rot = jnp.where(lane < half, pltpu.roll(x, LANES - half, 1), pltpu.roll(x, half, 1))
    return x * cos + rot * sin_signed


def _swa_body(first_ref, last_ref, pos_ref, q_ref, kp_ref, kc_ref, kn_ref, vp_ref, vc_ref, vn_ref,
              cp_ref, sp_ref, cc_ref, sc_ref, cn_ref, sn_ref, sink_ref, o_ref):
    i = pl.program_id(0)
    blk = B_BLOCK
    has_prev = first_ref[i] == 0
    has_next = last_ref[i] == 0
    k_all = jnp.concatenate([_rope(kp_ref[...], cp_ref[...], sp_ref[...]),
                             _rope(kc_ref[...], cc_ref[...], sc_ref[...]),
                             _rope(kn_ref[...], cn_ref[...], sn_ref[...])], axis=0)
    v_all = jnp.concatenate([vp_ref[...], vc_ref[...], vn_ref[...]], axis=0)
    row = _iota2((blk, 3 * blk), 0)
    col = _iota2((blk, 3 * blk), 1)
    rel = col - row
    ok = (rel >= 0) & (rel <= 2 * WINDOW)
    ok = ok & (has_prev | (col >= blk)) & (has_next | (col < 2 * blk))
    cos, sin_signed = cc_ref[...], sc_ref[...]
    for pair in range(B_Q_HEADS // 2):
        qr = _rope(q_ref[:, pair * LANES:(pair + 1) * LANES], cos, sin_signed)
        for sub in range(2):
            hq = 2 * pair + sub
            kv = hq // B_GROUP
            qh = qr[:, sub * B_HEAD_DIM:(sub + 1) * B_HEAD_DIM]
            kh = k_all[:, kv * B_HEAD_DIM:(kv + 1) * B_HEAD_DIM]
            vh = v_all[:, kv * B_HEAD_DIM:(kv + 1) * B_HEAD_DIM]
            s = jnp.where(ok, _dot_nt(qh, kh) * (B_HEAD_DIM ** -0.5), -jnp.inf)
            sink = sink_ref[0:1, hq:hq + 1]
            m = jnp.maximum(jnp.max(s, axis=1, keepdims=True), sink)
            p = jnp.exp(s - m)
            denom = jnp.sum(p, axis=1, keepdims=True) + jnp.exp(sink - m)
            o_ref[:, hq * B_HEAD_DIM:(hq + 1) * B_HEAD_DIM] = _dot(p / denom, vh)


def _swa(h, cos_t, sin_t, sink, first, last, pos):
    t = h.shape[0]
    n = t // B_BLOCK
    npos = cos_t.shape[0] // B_BLOCK
    prev = lambda i: jnp.maximum(i - 1, 0)
    nxt = lambda i: jnp.minimum(i + 1, n - 1)
    kcol, vcol = COL_B_K // LANES, COL_B_V // LANES
    hspec = lambda f, colblk: pl.BlockSpec((B_BLOCK, LANES), lambda i, a, b, p: (f(i), colblk))
    tspec = lambda delta: pl.BlockSpec(
        (B_BLOCK, LANES), lambda i, a, b, p: (jnp.clip(p[i] + delta, 0, npos - 1), 0))
    ident = lambda i: i
    return pl.pallas_call(
        _swa_body,
        grid_spec=pltpu.PrefetchScalarGridSpec(
            num_scalar_prefetch=3,
            grid=(n,),
            in_specs=[pl.BlockSpec((B_BLOCK, B_Q_HEADS * B_HEAD_DIM),
                                   lambda i, a, b, p: (i, COL_B_Q // (B_Q_HEADS * B_HEAD_DIM))),
                      hspec(prev, kcol), hspec(ident, kcol), hspec(nxt, kcol),
                      hspec(prev, vcol), hspec(ident, vcol), hspec(nxt, vcol),
                      tspec(-1), tspec(-1), tspec(0), tspec(0), tspec(1), tspec(1),
                      pl.BlockSpec((8, LANES), lambda i, a, b, p: (0, 0))],
            out_specs=pl.BlockSpec((B_BLOCK, B_Q_HEADS * B_HEAD_DIM), lambda i, a, b, p: (i, 0))),
        out_shape=jax.ShapeDtypeStruct((t, B_Q_HEADS * B_HEAD_DIM), _f32),
        compiler_params=_params(("parallel",)),
        name="window_attn",
    )(first, last, pos, h, h, h, h, h, h, h, cos_t, sin_t, cos_t, sin_t, cos_t, sin_t, sink)


def _residual_ln(x, y, g, b):
    z = DN_ALPHA * x + y
    mu = jnp.mean(z, axis=1, keepdims=True)
    zc = z - mu
    var = jnp.mean(zc * zc, axis=1, keepdims=True)
    return zc * lax.rsqrt(var + 1e-5) * g + b


def _gated_rms(o, gate, g):
    outs = []
    for h in range(HEADS):
        sl = slice(h * HEAD_DIM, (h + 1) * HEAD_DIM)
        x = o[:, sl]
        x = x * lax.rsqrt(jnp.mean(x * x, axis=1, keepdims=True) + 1e-6) * g
        outs.append(x * _silu(gate[:, sl]))
    return jnp.concatenate(outs, axis=1)


def _merge_body(x_ref, af_ref, ab_ref, ag_ref, ob_ref, cf_ref, cb_ref, cg_ref, ma_ref, mb_ref, mc_ref,
                wa_ref, wb_ref, wc_ref, wo_ref, nrm_ref, ln_ref, o_ref):
    oa = _gated_rms(af_ref[...] + ab_ref[...], ag_ref[...], nrm_ref[0:1, :])
    oc = _gated_rms(cf_ref[...] + cb_ref[...], cg_ref[...], nrm_ref[1:2, :])
    pa = jnp.dot(oa.astype(_bf16), wa_ref[...], preferred_element_type=_f32)
    pb = jnp.dot(ob_ref[...].astype(_bf16), wb_ref[...], preferred_element_type=_f32)
    pc = jnp.dot(oc.astype(_bf16), wc_ref[...], preferred_element_type=_f32)
    mix = (jax.nn.sigmoid(ma_ref[...]) * pa + jax.nn.sigmoid(mb_ref[...]) * pb
           + jax.nn.sigmoid(mc_ref[...]) * pc)
    y = jnp.dot(mix.astype(_bf16), wo_ref[...], preferred_element_type=_f32)
    o_ref[...] = _residual_ln(x_ref[...], y, ln_ref[0:1, :], ln_ref[1:2, :])


def _merge(x, h, oa_f, oa_b, ob, oc_f, oc_b, wa, wb, wc, wo, nrm, ln, tm):
    t = x.shape[0]
    row = lambda w, colblk=0: pl.BlockSpec((tm, w), lambda i: (i, colblk))
    full = lambda a: pl.BlockSpec(a.shape, lambda i: (0, 0))
    mcol = COL_MERGE // D_MODEL
    return pl.pallas_call(
        _merge_body,
        grid=(t // tm,),
        in_specs=[row(D_MODEL), row(WIDTH), row(WIDTH), row(WIDTH, COL_A_G // WIDTH), row(WIDTH),
                  row(WIDTH), row(WIDTH), row(WIDTH, COL_C_GATE // WIDTH),
                  row(D_MODEL, mcol), row(D_MODEL, mcol + 1), row(D_MODEL, mcol + 2),
                  full(wa), full(wb), full(wc), full(wo), full(nrm), full(ln)],
        out_specs=row(D_MODEL),
        out_shape=jax.ShapeDtypeStruct((t, D_MODEL), _f32),
        compiler_params=_params(("parallel",)),
        name="merge_mix",
    )(x, oa_f, oa_b, h, ob, oc_f, oc_b, h, h, h, h, wa, wb, wc, wo, nrm, ln)


def _xattn_body(seq_ref, x_ref, kv_ref, wq_ref, wo_ref, ln_ref, o_ref):
    x = x_ref[...]
    q = jnp.dot(x.astype(_bf16), wq_ref[...], preferred_element_type=_f32)
    outs = []
    for h in range(MEM_HEADS):
        lo = h * MEM_HEAD_DIM
        qh = q[:, lo:lo + MEM_HEAD_DIM]
        kh = kv_ref[0, :, lo:lo + MEM_HEAD_DIM]
        vh = kv_ref[0, :, D_MODEL + lo:D_MODEL + lo + MEM_HEAD_DIM]
        s = _dot_nt(qh, kh) * (MEM_HEAD_DIM ** -0.5)
        p = jnp.exp(s - jnp.max(s, axis=1, keepdims=True))
        p = p / jnp.sum(p, axis=1, keepdims=True)
        outs.append(_dot(p, vh))
    o = jnp.concatenate(outs, axis=1)
    y = jnp.dot(o.astype(_bf16), wo_ref[...], preferred_element_type=_f32)
    o_ref[...] = _residual_ln(x, y, ln_ref[0:1, :], ln_ref[1:2, :])


def _xattn(x, kv, wq, wo, ln, seq_of_tile, tm):
    t = x.shape[0]
    full = lambda a: pl.BlockSpec(a.shape, lambda i, s: (0, 0))
    return pl.pallas_call(
        _xattn_body,
        grid_spec=pltpu.PrefetchScalarGridSpec(
            num_scalar_prefetch=1,
            grid=(t // tm,),
            in_specs=[pl.BlockSpec((tm, D_MODEL), lambda i, s: (i, 0)),
                      pl.BlockSpec((1,) + kv.shape[1:], lambda i, s: (s[i], 0, 0)),
                      full(wq), full(wo), full(ln)],
            out_specs=pl.BlockSpec((tm, D_MODEL), lambda i, s: (i, 0))),
        out_shape=jax.ShapeDtypeStruct((t, D_MODEL), _f32),
        compiler_params=_params(("parallel",)),
        name="mem_xattn",
    )(seq_of_tile, x, kv, wq, wo, ln)


def _ffn_body(x_ref, wg_ref, wu_ref, wd_ref, ln_ref, o_ref, xb_ref, acc_ref):
    j = pl.program_id(1)

    @pl.when(j == 0)
    def _():
        xb_ref[...] = x_ref[...].astype(_bf16)
        acc_ref[...] = jnp.zeros_like(acc_ref)

    xb = xb_ref[...]
    gate = jnp.dot(xb, wg_ref[...], preferred_element_type=_f32)
    up = jnp.dot(xb, wu_ref[...], preferred_element_type=_f32)
    acc_ref[...] += jnp.dot((_silu(gate) * up).astype(_bf16), wd_ref[...], preferred_element_type=_f32)

    @pl.when(j == pl.num_programs(1) - 1)
    def _():
        o_ref[...] = _residual_ln(x_ref[...], acc_ref[...], ln_ref[0:1, :], ln_ref[1:2, :])


def _ffn(x, w_in, w_out, ln, tm, th):
    t = x.shape[0]
    nh = FFN_HIDDEN // th
    return pl.pallas_call(
        _ffn_body,
        grid=(t // tm, nh),
        in_specs=[pl.BlockSpec((tm, D_MODEL), lambda i, j: (i, 0)),
                  pl.BlockSpec((D_MODEL, th), lambda i, j: (0, j)),
                  pl.BlockSpec((D_MODEL, th), lambda i, j: (0, nh + j)),
                  pl.BlockSpec((th, D_MODEL), lambda i, j: (j, 0)),
                  pl.BlockSpec(ln.shape, lambda i, j: (0, 0))],
        out_specs=pl.BlockSpec((tm, D_MODEL), lambda i, j: (i, 0)),
        out_shape=jax.ShapeDtypeStruct((t, D_MODEL), _f32),
        scratch_shapes=[pltpu.VMEM((tm, D_MODEL), _bf16), pltpu.VMEM((tm, D_MODEL), _f32)],
        compiler_params=_params(("parallel", "arbitrary")),
        name="swiglu_ffn",
    )(x, w_in, w_in, w_out, ln)


def _boundary_tables(seq_lens, tile, reverse=False):
    first, last, pos, seq = [], [], [], []
    for sid, length in enumerate(seq_lens):
        n = length // tile
        for b in range(n):
            first.append(int(b == 0))
            last.append(int(b == n - 1))
            pos.append(b)
            seq.append(sid)
    as_i32 = lambda v: jnp.asarray(np.asarray(v, np.int32))
    return as_i32(first), as_i32(last), as_i32(pos), as_i32(seq)


def _permute_in_cols(w):
    pad = jnp.zeros(w.shape[:-1] + (IN_COLS_PAD - 8464,), w.dtype)
    return jnp.concatenate([w[..., 0:3072], w[..., 3328:4864], w[..., 4864:5376], w[..., 5392:8464],
                            w[..., 3072:3200], w[..., 3200:3328], w[..., 5376:5392], pad], axis=-1)


def _rows8(*rows):
    n = rows[0].shape[-1]
    out = jnp.zeros((8, n), _f32)
    for r, v in enumerate(rows):
        out = out.at[r].set(v.astype(_f32))
    return out


def _lane_pad(v, offset):
    return jnp.zeros((LANES,), _f32).at[offset:offset + v.shape[0]].set(v.astype(_f32))


def kernel(x_prompt, x_sample, mem_prompt, mem_sample, w_in, hgrn_lb_logits, hgrn_norm_g, attn_sink,
           gdn_conv_w, gdn_a_log, gdn_dt_bias, gdn_norm_g, w_branch_a, w_branch_b, w_branch_c, w_mix_out,
           w_mem_q, w_mem_kv, w_mem_o, w_ffn_in, w_ffn_out, ln_g, ln_b):
    depth = w_in.shape[0]
    d = x_prompt.shape[-1]
    seq_lens = (x_prompt.shape[1],) * x_prompt.shape[0] + (x_sample.shape[1],) * x_sample.shape[0]
    n_prompt = x_prompt.shape[0] * x_prompt.shape[1]
    x = jnp.concatenate([x_prompt.reshape(-1, d), x_sample.reshape(-1, d)], axis=0)
    mem = jnp.concatenate([mem_prompt, mem_sample], axis=0)
    n_seq, n_mem, _ = mem.shape
    t = x.shape[0]

    tm = math.gcd(512, *seq_lens)
    tm_proj = math.gcd(1024, t)
    first_c, last_c, _, _ = _boundary_tables(seq_lens, CHUNK)
    first_b, last_b, pos_b, _ = _boundary_tables(seq_lens, B_BLOCK)
    first_t, last_t, _, seq_t = _boundary_tables(seq_lens, tm)

    s_max = max(seq_lens)
    inv = ROPE_THETA ** (-jnp.arange(0, B_HEAD_DIM, 2, dtype=_f32) / B_HEAD_DIM)
    ang = jnp.arange(s_max, dtype=_f32)[:, None] * inv[None, :]
    cos_t = jnp.tile(jnp.cos(ang), (1, 4))
    sin_t = jnp.tile(jnp.concatenate([-jnp.sin(ang), jnp.sin(ang)], axis=1), (1, 2))

    cum = jnp.cumsum(jax.nn.softmax(hgrn_lb_logits.astype(_f32), axis=1), axis=1)
    lb = cum - cum[:, :1]

    w_in_p = _permute_in_cols(w_in).astype(_bf16)
    bf = lambda w: w.astype(_bf16)
    w_a, w_b, w_c, w_mix = bf(w_branch_a), bf(w_branch_b), bf(w_branch_c), bf(w_mix_out)
    w_q, w_kv, w_o = bf(w_mem_q), bf(w_mem_kv), bf(w_mem_o)
    w_f1, w_f2 = bf(w_ffn_in), bf(w_ffn_out)
    mem2 = mem.reshape(n_seq * n_mem, d)

    for l in range(depth):
        h = _matmul(x, w_in_p[l], tm_proj, 512)
        o_a = []
        for rev in (False, True):
            lbd = lb[1 if rev else 0, l]
            consts = _rows8(jnp.log(lbd), jnp.log1p(-lbd), 1.0 - lbd)
            o_a.append(_hgrn(h, consts, last_c[::-1] if rev else first_c, rev, COL_A_FB if rev else COL_A_FF))
        o_b = _swa(h, cos_t, sin_t, _rows8(_lane_pad(attn_sink[l], 0)), first_b, last_b, pos_b)
        qkv = _gdn_prep(h, _rows8(*[gdn_conv_w[l, j] for j in range(CONV_K)]), first_t, last_t, tm)
        gconst = _rows8(_lane_pad(-jnp.exp(gdn_a_log[l].astype(_f32)).reshape(-1), 2 * HEADS),
                        _lane_pad(gdn_dt_bias[l].reshape(-1), 2 * HEADS))
        o_c = [_gdn(qkv, h, gconst, last_c[::-1] if rev else first_c, rev) for rev in (False, True)]
        nrm = _rows8(hgrn_norm_g[l], gdn_norm_g[l])
        x = _merge(x, h, o_a[0], o_a[1], o_b, o_c[0], o_c[1], w_a[l], w_b[l], w_c[l], w_mix[l], nrm,
                   _rows8(ln_g[l, 0], ln_b[l, 0]), tm)
        kv = _matmul(mem2, w_kv[l], n_mem, 512).reshape(n_seq, n_mem, 2 * d)
        x = _xattn(x, kv, w_q[l], w_o[l], _rows8(ln_g[l, 1], ln_b[l, 1]), seq_t, tm)
        x = _ffn(x, w_f1[l], w_f2[l], _rows8(ln_g[l, 2], ln_b[l, 2]), tm, FFN_HIDDEN // 2)

    y_prompt = x[:n_prompt].reshape(x_prompt.shape)
    y_sample = x[n_prompt:].reshape(x_sample.shape)
    return (y_prompt, y_sample)
```

```python
import math

import numpy as np
import jax
import jax.numpy as jnp
from jax import lax
from jax.experimental import pallas as pl
from jax.experimental.pallas import tpu as pltpu

D_MODEL = 1024
DEPTH = 4
HEADS = 4
HEAD_DIM = 128
WIDTH = HEADS * HEAD_DIM
B_Q_HEADS = 8
B_KV_HEADS = 2
B_GROUP = B_Q_HEADS // B_KV_HEADS
B_HEAD_DIM = 64
WINDOW = 128
B_BLOCK = 128
ROPE_THETA = 10000.0
CONV_K = 5
MEM_HEADS = 4
MEM_HEAD_DIM = D_MODEL // MEM_HEADS
FFN_HIDDEN = 2816
DN_ALPHA = (2 * DEPTH) ** 0.25

CHUNK = 64
GDN_CHUNKS_PER_STEP = 4
LANES = 128
LOG2_E = math.log2(math.e)
VMEM_LIMIT = 56 * 1024 * 1024

COL_A_Q, COL_A_FF, COL_A_FB, COL_A_I, COL_A_G = 0, 512, 1024, 1536, 2048
COL_B_Q = 2560
COL_C_Q, COL_C_K, COL_C_V = 3072, 3584, 4096
COL_C_GATE = 4608
COL_MERGE = 5120
COL_B_K, COL_B_V = 8192, 8320
COL_GB = 8448
IN_COLS_PAD = 8704

_f32 = jnp.float32
_bf16 = jnp.bfloat16


def _dot(a, b):
    return jnp.dot(a.astype(_bf16), b.astype(_bf16), preferred_element_type=_f32)


def _dot_nt(a, b):
    return lax.dot_general(a.astype(_bf16), b.astype(_bf16), (((1,), (1,)), ((), ())),
                           preferred_element_type=_f32)


def _dot_tn(a, b):
    return lax.dot_general(a.astype(_bf16), b.astype(_bf16), (((0,), (0,)), ((), ())),
                           preferred_element_type=_f32)


def _split3(x):
    hi = x.astype(_bf16)
    r1 = x - hi.astype(_f32)
    mid = r1.astype(_bf16)
    lo = (r1 - mid.astype(_f32)).astype(_bf16)
    return hi, mid, lo


def _exact_dot(sel, parts):
    dot = lambda p: jnp.dot(sel, p, preferred_element_type=_f32)
    return dot(parts[0]) + dot(parts[1]) + dot(parts[2])


def _silu(x):
    return x * jax.nn.sigmoid(x)


def _iota2(shape, dim):
    return lax.broadcasted_iota(jnp.int32, shape, dim)


def _params(sem):
    return pltpu.CompilerParams(dimension_semantics=sem, vmem_limit_bytes=VMEM_LIMIT)


def _matmul_body(x_ref, w_ref, o_ref, xb_ref):
    @pl.when(pl.program_id(1) == 0)
    def _():
        xb_ref[...] = x_ref[...].astype(_bf16)

    o_ref[...] = jnp.dot(xb_ref[...], w_ref[...], preferred_element_type=_f32)


def _matmul(x, w, tm, tn):
    t, k = x.shape
    n = w.shape[1]
    return pl.pallas_call(
        _matmul_body,
        grid=(t // tm, n // tn),
        in_specs=[pl.BlockSpec((tm, k), lambda i, j: (i, 0)),
                  pl.BlockSpec((k, tn), lambda i, j: (0, j))],
        out_specs=pl.BlockSpec((tm, tn), lambda i, j: (i, j)),
        out_shape=jax.ShapeDtypeStruct((t, n), _f32),
        scratch_shapes=[pltpu.VMEM((tm, k), _bf16)],
        compiler_params=_params(("parallel", "arbitrary")),
        name="dense_proj",
    )(x, w)


def _chunk_masks(c, reverse):
    row = _iota2((c, c), 0)
    col = _iota2((c, c), 1)
    if reverse:
        return col >= row, col > row
    return col <= row, col < row


def _level_mask(c, blk, reverse):
    row = _iota2((c, c), 0)
    col = _iota2((c, c), 1)
    half = blk // 2
    same = (row // blk) == (col // blk)
    r_hi = (row % blk) >= half
    c_hi = (col % blk) >= half
    if reverse:
        return same & jnp.logical_not(r_hi) & c_hi
    return same & r_hi & jnp.logical_not(c_hi)


def _block_ref_rows(g, blk, reverse):
    c, n = g.shape
    idx = blk // 2 if reverse else blk // 2 - 1
    rows = max(blk, 8)
    g3 = g.reshape(c // rows, rows, n)
    pick = lambda r: jnp.broadcast_to(g3[:, r:r + 1, :], g3.shape)
    out = pick(idx)
    if blk < rows:
        sub = lax.broadcasted_iota(jnp.int32, g3.shape, 1)
        for b in range(1, rows // blk):
            out = jnp.where(sub >= b * blk, pick(b * blk + idx), out)
    return out.reshape(c, n)


def _levels(c):
    out, blk = [], c
    while blk >= 2:
        out.append(blk)
        blk //= 2
    return out


def _hgrn_body(first_f_ref, first_b_ref, zq_f, zf_f, zi_f, zq_b, zf_b, zi_b, lb_ref, o_f, o_b, st_ref):
    c = CHUNK
    step = pl.program_id(0)

    @pl.when(first_f_ref[step] == 1)
    def _():
        st_ref[0:HEADS] = jnp.zeros((HEADS, HEAD_DIM, HEAD_DIM), _f32)

    @pl.when(first_b_ref[step] == 1)
    def _():
        st_ref[HEADS:2 * HEADS] = jnp.zeros((HEADS, HEAD_DIM, HEAD_DIM), _f32)

    row = _iota2((c, c), 0)
    col = _iota2((c, c), 1)
    eye = row == col
    blks = _levels(c)
    units = []
    for d, (zq_ref, zf_ref, zi_ref) in enumerate(((zq_f, zf_f, zi_f), (zq_b, zf_b, zi_b))):
        reverse = d == 1
        incl, _ = _chunk_masks(c, reverse)
        masks = [_level_mask(c, blk, reverse) for blk in blks]
        zf = zf_ref[...]
        log_lb = lb_ref[3 * d:3 * d + 1, :]
        log_1m_lb = lb_ref[3 * d + 1:3 * d + 2, :]
        one_m_lb = lb_ref[3 * d + 2:3 * d + 3, :]
        ez = jnp.exp(-jnp.abs(zf))
        one_p = 1.0 + ez
        log_sig = jnp.minimum(zf, 0.0) - jnp.log(one_p)
        b = log_1m_lb + log_sig
        log_f = jnp.maximum(log_lb, b) + jnp.log(1.0 + jnp.exp(-jnp.abs(log_lb - b)))
        key_all = one_m_lb * (jnp.where(zf >= 0.0, ez, 1.0) / one_p)
        gc_all = _exact_dot(incl.astype(_bf16), _split3(log_f))
        ref_all = [_block_ref_rows(gc_all, blk, reverse) for blk in blks]
        rpos = _iota2((c, HEAD_DIM), 0)
        later = [((rpos % blk) < blk // 2) if reverse else ((rpos % blk) >= blk // 2) for blk in blks]
        sgn = [jnp.where(m, LOG2_E, -LOG2_E) for m in later]
        q_all = _silu(zq_ref[...]) * (HEAD_DIM ** -0.5)
        v_all = zi_ref[...]
        tot_row = 0 if reverse else c - 1
        for h in range(HEADS):
            sl = slice(h * HEAD_DIM, (h + 1) * HEAD_DIM)
            units.append(dict(q=q_all[:, sl], k=key_all[:, sl], v=v_all[:, sl], gc=gc_all[:, sl],
                              refs=[r[:, sl] for r in ref_all],
                              masks=masks, sgn=sgn, st=st_ref[d * HEADS + h],
                              g_tot=gc_all[tot_row:tot_row + 1, sl]))

    scores = [jnp.where(eye, _dot_nt(u["q"], u["k"]), 0.0) for u in units]
    for l in range(len(blks)):
        e = [jnp.exp2((u["gc"] - u["refs"][l]) * u["sgn"][l]) for u in units]
        scores = [scores[i] + jnp.where(u["masks"][l], _dot_nt(u["q"] * e[i], u["k"] * e[i]), 0.0)
                  for i, u in enumerate(units)]
    out = [_dot(scores[i], u["v"]) + _dot_nt(u["q"] * jnp.exp(u["gc"]), u["st"]) for i, u in enumerate(units)]
    st_new = [u["st"] * jnp.exp(u["g_tot"]) + _dot_tn(u["v"], u["k"] * jnp.exp(u["g_tot"] - u["gc"]))
              for u in units]
    for i in range(len(units)):
        d, h = divmod(i, HEADS)
        st_ref[i] = st_new[i]
        (o_f, o_b)[d][:, h * HEAD_DIM:(h + 1) * HEAD_DIM] = out[i]


def _hgrn(h, lb_consts, first_f, first_b):
    t = h.shape[0]
    n = t // CHUNK
    fwd = lambda col: pl.BlockSpec((CHUNK, WIDTH), lambda i, a, b: (i, col // WIDTH))
    bwd = lambda col: pl.BlockSpec((CHUNK, WIDTH), lambda i, a, b: (n - 1 - i, col // WIDTH))
    out = jax.ShapeDtypeStruct((t, WIDTH), _f32)
    return pl.pallas_call(
        _hgrn_body,
        grid_spec=pltpu.PrefetchScalarGridSpec(
            num_scalar_prefetch=2,
            grid=(n,),
            in_specs=[fwd(COL_A_Q), fwd(COL_A_FF), fwd(COL_A_I), bwd(COL_A_Q), bwd(COL_A_FB), bwd(COL_A_I),
                      pl.BlockSpec((8, WIDTH), lambda i, a, b: (0, 0))],
            out_specs=[fwd(0), bwd(0)],
            scratch_shapes=[pltpu.VMEM((2 * HEADS, HEAD_DIM, HEAD_DIM), _f32)]),
        out_shape=[out, out],
        compiler_params=_params(("arbitrary",)),
        name="hgrn",
    )(first_f, first_b, h, h, h, h, h, h, lb_consts)


def _gdn_prep_body(first_ref, last_ref, prev_ref, cur_ref, next_ref, w_ref, o_ref):
    i = pl.program_id(0)
    tm = cur_ref.shape[0]
    halo = CONV_K // 2
    cur = cur_ref[...]
    prev = jnp.where(first_ref[i] == 1, 0.0, prev_ref[...])
    nxt = jnp.where(last_ref[i] == 1, 0.0, next_ref[...])
    ext = jnp.concatenate([prev, cur, nxt], axis=0)
    acc = jnp.zeros_like(cur)
    for j in range(CONV_K):
        off = 8 - halo + j
        acc = acc + ext[off:off + tm, :] * w_ref[j:j + 1, :]
    y = _silu(acc)
    for part in range(3):
        for h in range(HEADS):
            lo = part * WIDTH + h * HEAD_DIM
            x = y[:, lo:lo + HEAD_DIM]
            if part < 2:
                x = x * lax.rsqrt(jnp.sum(x * x, axis=1, keepdims=True) + 1e-6)
            if part == 0:
                x = x * (HEAD_DIM ** -0.5)
            o_ref[:, lo:lo + HEAD_DIM] = x


def _gdn_prep(h, conv_w, first, last, tm):
    t = h.shape[0]
    n = t // tm
    r8 = tm // 8
    cb = COL_C_Q // (3 * WIDTH)
    return pl.pallas_call(
        _gdn_prep_body,
        grid_spec=pltpu.PrefetchScalarGridSpec(
            num_scalar_prefetch=2,
            grid=(n,),
            in_specs=[pl.BlockSpec((8, 3 * WIDTH), lambda i, f, l: (jnp.maximum(i * r8 - 1, 0), cb)),
                      pl.BlockSpec((tm, 3 * WIDTH), lambda i, f, l: (i, cb)),
                      pl.BlockSpec((8, 3 * WIDTH), lambda i, f, l: (jnp.minimum((i + 1) * r8, n * r8 - 1), cb)),
                      pl.BlockSpec((8, 3 * WIDTH), lambda i, f, l: (0, 0))],
            out_specs=pl.BlockSpec((tm, 3 * WIDTH), lambda i, f, l: (i, 0))),
        out_shape=jax.ShapeDtypeStruct((t, 3 * WIDTH), _f32),
        compiler_params=_params(("parallel",)),
        name="gdn_prep",
    )(first, last, h, h, h, conv_w)


def _gdn_body(first_f_ref, first_b_ref, q_f, k_f, v_f, gb_f, q_b, k_b, v_b, gb_b, cst_ref, o_f, o_b, st_ref):
    c = CHUNK
    step = pl.program_id(0)

    @pl.when(first_f_ref[step] == 1)
    def _():
        st_ref[0:HEADS] = jnp.zeros((HEADS, HEAD_DIM, HEAD_DIM), _f32)

    @pl.when(first_b_ref[step] == 1)
    def _():
        st_ref[HEADS:2 * HEADS] = jnp.zeros((HEADS, HEAD_DIM, HEAD_DIM), _f32)

    nsub = q_f.shape[0] // c
    row = _iota2((c, c), 0)
    col = _iota2((c, c), 1)
    eye = (row == col).astype(_f32)
    neg_a = cst_ref[0:1, :]
    dt_bias = cst_ref[1:2, :]
    sizes = _levels(c)[::-1]
    units = []
    for d, (q_ref, k_ref, v_ref, gb_ref) in enumerate(((q_f, k_f, v_f, gb_f), (q_b, k_b, v_b, gb_b))):
        reverse = d == 1
        incl, strict = _chunk_masks(c, reverse)
        tri = incl.astype(_bf16)
        masks = [_level_mask(c, blk, reverse) for blk in sizes]
        gb = gb_ref[...]
        beta_t = jax.nn.sigmoid(gb)
        zz = gb + dt_bias
        g_in = neg_a * (jnp.maximum(zz, 0.0) + jnp.log1p(jnp.exp(-jnp.abs(zz))))
        tot_row = 0 if reverse else c - 1
        for s in (range(nsub - 1, -1, -1) if reverse else range(nsub)):
            rs = slice(s * c, (s + 1) * c)
            gc_t = _exact_dot(tri, _split3(g_in[rs]))
            gc_tt = jnp.transpose(gc_t)
            for h in range(HEADS):
                sl = slice(h * HEAD_DIM, (h + 1) * HEAD_DIM)
                j = 2 * HEADS + d * HEADS + h
                g_col = gc_t[:, j:j + 1]
                decay = jnp.where(incl, jnp.exp(jnp.minimum(g_col - gc_tt[j:j + 1, :], 0.0)), 0.0)
                units.append(dict(q=q_ref[rs, sl], k=k_ref[rs, sl], v=v_ref[rs, sl], rows=rs,
                                  g_col=g_col, decay=decay, strict=strict, masks=masks,
                                  beta=beta_t[rs, d * HEADS + h:d * HEADS + h + 1],
                                  g_tot=gc_t[tot_row:tot_row + 1, j:j + 1]))

    a = [jnp.where(u["strict"], _dot_nt(u["k"], u["k"]) * u["decay"] * u["beta"], 0.0) for u in units]
    qk = [_dot_nt(u["q"], u["k"]) * u["decay"] for u in units]
    inv = [eye - jnp.where(u["masks"][0], a[i], 0.0) for i, u in enumerate(units)]
    for l in range(1, len(sizes)):
        t1 = [_dot(jnp.where(u["masks"][l], a[i], 0.0), inv[i]) for i, u in enumerate(units)]
        inv = [inv[i] - _dot(inv[i], t1[i]) for i in range(len(units))]
    e_g = [jnp.exp(u["g_col"]) for u in units]
    uu = [_dot(inv[i], u["v"] * u["beta"]) for i, u in enumerate(units)]
    ww = [_dot(inv[i], u["k"] * (u["beta"] * e_g[i])) for i, u in enumerate(units)]
    qd = [u["q"] * e_g[i] for i, u in enumerate(units)]
    kd = [u["k"] * jnp.exp(u["g_tot"] - u["g_col"]) for u in units]

    st = [st_ref[i] for i in range(2 * HEADS)]
    for slot in range(nsub):
        for d in range(2):
            ids = [(d * nsub + slot) * HEADS + h for h in range(HEADS)]
            v_new = [uu[i] - _dot_nt(ww[i], st[d * HEADS + h]) for h, i in enumerate(ids)]
            out = [_dot_nt(qd[i], st[d * HEADS + h]) + _dot(qk[i], v_new[h]) for h, i in enumerate(ids)]
            for h, i in enumerate(ids):
                st[d * HEADS + h] = (st[d * HEADS + h] * jnp.exp(units[i]["g_tot"])
                                     + _dot_tn(v_new[h], kd[i]))
                (o_f, o_b)[d][units[i]["rows"], h * HEAD_DIM:(h + 1) * HEAD_DIM] = out[h]
    for i in range(2 * HEADS):
        st_ref[i] = st[i]


def _gdn(qkv, h, consts, first_f, first_b, rows):
    t = qkv.shape[0]
    n = t // rows
    fwd = lambda w, colblk: pl.BlockSpec((rows, w), lambda i, a, b: (i, colblk))
    bwd = lambda w, colblk: pl.BlockSpec((rows, w), lambda i, a, b: (n - 1 - i, colblk))
    gcol = COL_GB // LANES
    out = jax.ShapeDtypeStruct((t, WIDTH), _f32)
    return pl.pallas_call(
        _gdn_body,
        grid_spec=pltpu.PrefetchScalarGridSpec(
            num_scalar_prefetch=2,
            grid=(n,),
            in_specs=[fwd(WIDTH, 0), fwd(WIDTH, 1), fwd(WIDTH, 2), fwd(LANES, gcol),
                      bwd(WIDTH, 0), bwd(WIDTH, 1), bwd(WIDTH, 2), bwd(LANES, gcol),
                      pl.BlockSpec((8, LANES), lambda i, a, b: (0, 0))],
            out_specs=[fwd(WIDTH, 0), bwd(WIDTH, 0)],
            scratch_shapes=[pltpu.VMEM((2 * HEADS, HEAD_DIM, HEAD_DIM), _f32)]),
        out_shape=[out, out],
        compiler_params=_params(("arbitrary",)),
        name="gdn",
    )(first_f, first_b, qkv, qkv, qkv, h, qkv, qkv, qkv, h, consts)


def _rope(x, cos, sin_signed):
    lane = _iota2(x.shape, 1) % B_HEAD_DIM
    half = B_HEAD_DIM // 2
    rot = jnp.where(lane < half, pltpu.roll(x, LANES - half, 1), pltpu.roll(x, half, 1))
    return x * cos + rot * sin_signed


def _swa_body(first_ref, last_ref, pos_ref, q_ref, kp_ref, kc_ref, kn_ref, vp_ref, vc_ref, vn_ref,
              cp_ref, sp_ref, cc_ref, sc_ref, cn_ref, sn_ref, sink_ref, o_ref):
    i = pl.program_id(0)
    blk = B_BLOCK
    has_prev = first_ref[i] == 0
    has_next = last_ref[i] == 0
    k_all = jnp.concatenate([_rope(kp_ref[...], cp_ref[...], sp_ref[...]),
                             _rope(kc_ref[...], cc_ref[...], sc_ref[...]),
                             _rope(kn_ref[...], cn_ref[...], sn_ref[...])], axis=0)
    v_all = jnp.concatenate([vp_ref[...], vc_ref[...], vn_ref[...]], axis=0)
    rows = B_GROUP * blk
    row = _iota2((rows, 3 * blk), 0) % blk
    col = _iota2((rows, 3 * blk), 1)
    rel = col - row
    ok = (rel >= 0) & (rel <= 2 * WINDOW)
    ok = ok & (has_prev | (col >= blk)) & (has_next | (col < 2 * blk))
    cos, sin_signed = cc_ref[...], sc_ref[...]
    scale = B_HEAD_DIM ** -0.5
    qr = [_rope(q_ref[:, p * LANES:(p + 1) * LANES], cos, sin_signed) * scale for p in range(B_Q_HEADS // 2)]
    heads = [qr[hq // 2][:, (hq % 2) * B_HEAD_DIM:(hq % 2 + 1) * B_HEAD_DIM] for hq in range(B_Q_HEADS)]
    sink_row = sink_ref[0:1, :]
    for kv in range(B_KV_HEADS):
        q4 = jnp.concatenate(heads[kv * B_GROUP:(kv + 1) * B_GROUP], axis=0)
        sink = jnp.concatenate([jnp.broadcast_to(sink_row[:, kv * B_GROUP + g:kv * B_GROUP + g + 1], (blk, 1))
                                for g in range(B_GROUP)], axis=0)
        kh = k_all[:, kv * B_HEAD_DIM:(kv + 1) * B_HEAD_DIM]
        vh = v_all[:, kv * B_HEAD_DIM:(kv + 1) * B_HEAD_DIM]
        s = jnp.where(ok, _dot_nt(q4, kh), -jnp.inf)
        m = jnp.maximum(jnp.max(s, axis=1, keepdims=True), sink)
        p = jnp.exp(s - m)
        denom = jnp.sum(p, axis=1, keepdims=True) + jnp.exp(sink - m)
        o4 = _dot(p, vh) / denom
        for g in range(B_GROUP):
            hq = kv * B_GROUP + g
            o_ref[:, hq * B_HEAD_DIM:(hq + 1) * B_HEAD_DIM] = o4[g * blk:(g + 1) * blk, :]


def _swa(h, cos_t, sin_t, sink, first, last, pos):
    t = h.shape[0]
    n = t // B_BLOCK
    npos = cos_t.shape[0] // B_BLOCK
    prev = lambda i: jnp.maximum(i - 1, 0)
    nxt = lambda i: jnp.minimum(i + 1, n - 1)
    kcol, vcol = COL_B_K // LANES, COL_B_V // LANES
    hspec = lambda f, colblk: pl.BlockSpec((B_BLOCK, LANES), lambda i, a, b, p: (f(i), colblk))
    tspec = lambda delta: pl.BlockSpec(
        (B_BLOCK, LANES), lambda i, a, b, p: (jnp.clip(p[i] + delta, 0, npos - 1), 0))
    ident = lambda i: i
    qw = B_Q_HEADS * B_HEAD_DIM
    return pl.pallas_call(
        _swa_body,
        grid_spec=pltpu.PrefetchScalarGridSpec(
            num_scalar_prefetch=3,
            grid=(n,),
            in_specs=[pl.BlockSpec((B_BLOCK, qw), lambda i, a, b, p: (i, COL_B_Q // qw)),
                      hspec(prev, kcol), hspec(ident, kcol), hspec(nxt, kcol),
                      hspec(prev, vcol), hspec(ident, vcol), hspec(nxt, vcol),
                      tspec(-1), tspec(-1), tspec(0), tspec(0), tspec(1), tspec(1),
                      pl.BlockSpec((8, LANES), lambda i, a, b, p: (0, 0))],
            out_specs=pl.BlockSpec((B_BLOCK, qw), lambda i, a, b, p: (i, 0))),
        out_shape=jax.ShapeDtypeStruct((t, qw), _f32),
        compiler_params=_params(("parallel",)),
        name="window_attn",
    )(first, last, pos, h, h, h, h, h, h, h, cos_t, sin_t, cos_t, sin_t, cos_t, sin_t, sink)


def _residual_ln(x, y, g, b):
    z = DN_ALPHA * x + y
    mu = jnp.mean(z, axis=1, keepdims=True)
    zc = z - mu
    var = jnp.mean(zc * zc, axis=1, keepdims=True)
    return zc * lax.rsqrt(var + 1e-5) * g + b


def _gated_rms(o, gate, g):
    outs = []
    for h in range(HEADS):
        sl = slice(h * HEAD_DIM, (h + 1) * HEAD_DIM)
        x = o[:, sl]
        x = x * lax.rsqrt(jnp.mean(x * x, axis=1, keepdims=True) + 1e-6) * g
        outs.append(x * _silu(gate[:, sl]))
    return jnp.concatenate(outs, axis=1)


def _merge_body(x_ref, af_ref, ab_ref, ag_ref, ob_ref, cf_ref, cb_ref, cg_ref, ma_ref, mb_ref, mc_ref,
                wa_ref, wb_ref, wc_ref, wo_ref, nrm_ref, ln_ref, o_ref):
    oa = _gated_rms(af_ref[...] + ab_ref[...], ag_ref[...], nrm_ref[0:1, :])
    oc = _gated_rms(cf_ref[...] + cb_ref[...], cg_ref[...], nrm_ref[1:2, :])
    pa = jnp.dot(oa.astype(_bf16), wa_ref[...], preferred_element_type=_f32)
    pb = jnp.dot(ob_ref[...].astype(_bf16), wb_ref[...], preferred_element_type=_f32)
    pc = jnp.dot(oc.astype(_bf16), wc_ref[...], preferred_element_type=_f32)
    mix = (jax.nn.sigmoid(ma_ref[...]) * pa + jax.nn.sigmoid(mb_ref[...]) * pb
           + jax.nn.sigmoid(mc_ref[...]) * pc)
    y = jnp.dot(mix.astype(_bf16), wo_ref[...], preferred_element_type=_f32)
    o_ref[...] = _residual_ln(x_ref[...], y, ln_ref[0:1, :], ln_ref[1:2, :])


def _merge(x, h, oa_f, oa_b, ob, oc_f, oc_b, wa, wb, wc, wo, nrm, ln, tm):
    t = x.shape[0]
    row = lambda w, colblk=0: pl.BlockSpec((tm, w), lambda i: (i, colblk))
    full = lambda a: pl.BlockSpec(a.shape, lambda i: (0, 0))
    mcol = COL_MERGE // D_MODEL
    return pl.pallas_call(
        _merge_body,
        grid=(t // tm,),
        in_specs=[row(D_MODEL), row(WIDTH), row(WIDTH), row(WIDTH, COL_A_G // WIDTH), row(WIDTH),
                  row(WIDTH), row(WIDTH), row(WIDTH, COL_C_GATE // WIDTH),
                  row(D_MODEL, mcol), row(D_MODEL, mcol + 1), row(D_MODEL, mcol + 2),
                  full(wa), full(wb), full(wc), full(wo), full(nrm), full(ln)],
        out_specs=row(D_MODEL),
        out_shape=jax.ShapeDtypeStruct((t, D_MODEL), _f32),
        compiler_params=_params(("parallel",)),
        name="merge_mix",
    )(x, oa_f, oa_b, h, ob, oc_f, oc_b, h, h, h, h, wa, wb, wc, wo, nrm, ln)


def _xattn_body(seq_ref, x_ref, kv_ref, wq_ref, wo_ref, ln_ref, o_ref):
    x = x_ref[...]
    q = jnp.dot(x.astype(_bf16), wq_ref[...], preferred_element_type=_f32)
    outs = []
    for h in range(MEM_HEADS):
        lo = h * MEM_HEAD_DIM
        qh = q[:, lo:lo + MEM_HEAD_DIM]
        kh = kv_ref[0, :, lo:lo + MEM_HEAD_DIM]
        vh = kv_ref[0, :, D_MODEL + lo:D_MODEL + lo + MEM_HEAD_DIM]
        s = _dot_nt(qh, kh) * (MEM_HEAD_DIM ** -0.5)
        p = jnp.exp(s - jnp.max(s, axis=1, keepdims=True))
        p = p / jnp.sum(p, axis=1, keepdims=True)
        outs.append(_dot(p, vh))
    o = jnp.concatenate(outs, axis=1)
    y = jnp.dot(o.astype(_bf16), wo_ref[...], preferred_element_type=_f32)
    o_ref[...] = _residual_ln(x, y, ln_ref[0:1, :], ln_ref[1:2, :])


def _xattn(x, kv, wq, wo, ln, seq_of_tile, tm):
    t = x.shape[0]
    full = lambda a: pl.BlockSpec(a.shape, lambda i, s: (0, 0))
    return pl.pallas_call(
        _xattn_body,
        grid_spec=pltpu.PrefetchScalarGridSpec(
            num_scalar_prefetch=1,
            grid=(t // tm,),
            in_specs=[pl.BlockSpec((tm, D_MODEL), lambda i, s: (i, 0)),
                      pl.BlockSpec((1,) + kv.shape[1:], lambda i, s: (s[i], 0, 0)),
                      full(wq), full(wo), full(ln)],
            out_specs=pl.BlockSpec((tm, D_MODEL), lambda i, s: (i, 0))),
        out_shape=jax.ShapeDtypeStruct((t, D_MODEL), _f32),
        compiler_params=_params(("parallel",)),
        name="mem_xattn",
    )(seq_of_tile, x, kv, wq, wo, ln)


def _ffn_body(x_ref, wg_ref, wu_ref, wd_ref, ln_ref, o_ref, xb_ref, acc_ref):
    j = pl.program_id(1)

    @pl.when(j == 0)
    def _():
        xb_ref[...] = x_ref[...].astype(_bf16)
        acc_ref[...] = jnp.zeros_like(acc_ref)

    xb = xb_ref[...]
    gate = jnp.dot(xb, wg_ref[...], preferred_element_type=_f32)
    up = jnp.dot(xb, wu_ref[...], preferred_element_type=_f32)
    acc_ref[...] += jnp.dot((_silu(gate) * up).astype(_bf16), wd_ref[...], preferred_element_type=_f32)

    @pl.when(j == pl.num_programs(1) - 1)
    def _():
        o_ref[...] = _residual_ln(x_ref[...], acc_ref[...], ln_ref[0:1, :], ln_ref[1:2, :])


def _ffn(x, w_in, w_out, ln, tm, th):
    t = x.shape[0]
    nh = FFN_HIDDEN // th
    return pl.pallas_call(
        _ffn_body,
        grid=(t // tm, nh),
        in_specs=[pl.BlockSpec((tm, D_MODEL), lambda i, j: (i, 0)),
                  pl.BlockSpec((D_MODEL, th), lambda i, j: (0, j)),
                  pl.BlockSpec((D_MODEL, th), lambda i, j: (0, nh + j)),
                  pl.BlockSpec((th, D_MODEL), lambda i, j: (j, 0)),
                  pl.BlockSpec(ln.shape, lambda i, j: (0, 0))],
        out_specs=pl.BlockSpec((tm, D_MODEL), lambda i, j: (i, 0)),
        out_shape=jax.ShapeDtypeStruct((t, D_MODEL), _f32),
        scratch_shapes=[pltpu.VMEM((tm, D_MODEL), _bf16), pltpu.VMEM((tm, D_MODEL), _f32)],
        compiler_params=_params(("parallel", "arbitrary")),
        name="swiglu_ffn",
    )(x, w_in, w_in, w_out, ln)


def _boundary_tables(seq_lens, tile):
    first, last, pos, seq = [], [], [], []
    for sid, length in enumerate(seq_lens):
        n = length // tile
        for b in range(n):
            first.append(int(b == 0))
            last.append(int(b == n - 1))
            pos.append(b)
            seq.append(sid)
    as_i32 = lambda v: jnp.asarray(np.asarray(v, np.int32))
    return as_i32(first), as_i32(last), as_i32(pos), as_i32(seq)


def _permute_in_cols(w):
    pad = jnp.zeros(w.shape[:-1] + (IN_COLS_PAD - 8464,), w.dtype)
    return jnp.concatenate([w[..., 0:3072], w[..., 3328:4864], w[..., 4864:5376], w[..., 5392:8464],
                            w[..., 3072:3200], w[..., 3200:3328], w[..., 5376:5392], pad], axis=-1)


def _rows8(*rows):
    n = rows[0].shape[-1]
    out = jnp.zeros((8, n), _f32)
    for r, v in enumerate(rows):
        out = out.at[r].set(v.astype(_f32))
    return out


def _lane_pad(v, offset):
    return jnp.zeros((LANES,), _f32).at[offset:offset + v.shape[0]].set(v.astype(_f32))


def kernel(x_prompt, x_sample, mem_prompt, mem_sample, w_in, hgrn_lb_logits, hgrn_norm_g, attn_sink,
           gdn_conv_w, gdn_a_log, gdn_dt_bias, gdn_norm_g, w_branch_a, w_branch_b, w_branch_c, w_mix_out,
           w_mem_q, w_mem_kv, w_mem_o, w_ffn_in, w_ffn_out, ln_g, ln_b):
    depth = w_in.shape[0]
    d = x_prompt.shape[-1]
    seq_lens = (x_prompt.shape[1],) * x_prompt.shape[0] + (x_sample.shape[1],) * x_sample.shape[0]
    n_prompt = x_prompt.shape[0] * x_prompt.shape[1]
    x = jnp.concatenate([x_prompt.reshape(-1, d), x_sample.reshape(-1, d)], axis=0)
    mem = jnp.concatenate([mem_prompt, mem_sample], axis=0)
    n_seq, n_mem, _ = mem.shape
    t = x.shape[0]

    tm = math.gcd(512, *seq_lens)
    tm_proj = math.gcd(1024, t)
    first_c, last_c, _, _ = _boundary_tables(seq_lens, CHUNK)
    last_c_rev = last_c[::-1]
    rows_g = math.gcd(GDN_CHUNKS_PER_STEP * CHUNK, *seq_lens)
    first_g, last_g, _, _ = _boundary_tables(seq_lens, rows_g)
    last_g_rev = last_g[::-1]
    first_b, last_b, pos_b, _ = _boundary_tables(seq_lens, B_BLOCK)
    first_t, last_t, _, seq_t = _boundary_tables(seq_lens, tm)

    s_max = max(seq_lens)
    inv = ROPE_THETA ** (-jnp.arange(0, B_HEAD_DIM, 2, dtype=_f32) / B_HEAD_DIM)
    ang = jnp.arange(s_max, dtype=_f32)[:, None] * inv[None, :]
    cos_t = jnp.tile(jnp.cos(ang), (1, 4))
    sin_t = jnp.tile(jnp.concatenate([-jnp.sin(ang), jnp.sin(ang)], axis=1), (1, 2))

    cum = jnp.cumsum(jax.nn.softmax(hgrn_lb_logits.astype(_f32), axis=1), axis=1)
    lb = cum - cum[:, :1]

    w_in_p = _permute_in_cols(w_in).astype(_bf16)
    bf = lambda w: w.astype(_bf16)
    w_a, w_b, w_c, w_mix = bf(w_branch_a), bf(w_branch_b), bf(w_branch_c), bf(w_mix_out)
    w_q, w_kv, w_o = bf(w_mem_q), bf(w_mem_kv), bf(w_mem_o)
    w_f1, w_f2 = bf(w_ffn_in), bf(w_ffn_out)
    mem2 = mem.reshape(n_seq * n_mem, d)

    for l in range(depth):
        h = _matmul(x, w_in_p[l], tm_proj, 512)
        lb_rows = []
        for dirn in range(2):
            lbd = lb[dirn, l]
            lb_rows += [jnp.log(lbd), jnp.log1p(-lbd), 1.0 - lbd]
        oa_f, oa_b = _hgrn(h, _rows8(*lb_rows), first_c, last_c_rev)
        o_b = _swa(h, cos_t, sin_t, _rows8(_lane_pad(attn_sink[l], 0)), first_b, last_b, pos_b)
        qkv = _gdn_prep(h, _rows8(*[gdn_conv_w[l, j] for j in range(CONV_K)]), first_t, last_t, tm)
        gconst = _rows8(_lane_pad(-jnp.exp(gdn_a_log[l].astype(_f32)).reshape(-1), 2 * HEADS),
                        _lane_pad(gdn_dt_bias[l].reshape(-1), 2 * HEADS))
        oc_f, oc_b = _gdn(qkv, h, gconst, first_g, last_g_rev, rows_g)
        nrm = _rows8(hgrn_norm_g[l], gdn_norm_g[l])
        x = _merge(x, h, oa_f, oa_b, o_b, oc_f, oc_b, w_a[l], w_b[l], w_c[l], w_mix[l], nrm,
                   _rows8(ln_g[l, 0], ln_b[l, 0]), tm)
        kv = _matmul(mem2, w_kv[l], n_mem, 512).reshape(n_seq, n_mem, 2 * d)
        x = _xattn(x, kv, w_q[l], w_o[l], _rows8(ln_g[l, 1], ln_b[l, 1]), seq_t, tm)
        x = _ffn(x, w_f1[l], w_f2[l], _rows8(ln_g[l, 2], ln_b[l, 2]), tm, FFN_HIDDEN // 2)

    y_prompt = x[:n_prompt].reshape(x_prompt.shape)
    y_sample = x[n_prompt:].reshape(x_sample.shape)
    return (y_prompt, y_sample)
```

```python
import math

import numpy as np
import jax
import jax.numpy as jnp
from jax import lax
from jax.experimental import pallas as pl
from jax.experimental.pallas import tpu as pltpu

D_MODEL = 1024
DEPTH = 4
HEADS = 4
HEAD_DIM = 128
WIDTH = HEADS * HEAD_DIM
B_Q_HEADS = 8
B_KV_HEADS = 2
B_GROUP = B_Q_HEADS // B_KV_HEADS
B_HEAD_DIM = 64
WINDOW = 128
B_BLOCK = 128
ROPE_THETA = 10000.0
CONV_K = 5
MEM_HEADS = 4
MEM_HEAD_DIM = D_MODEL // MEM_HEADS
FFN_HIDDEN = 2816
DN_ALPHA = (2 * DEPTH) ** 0.25

CHUNK = 64
GDN_CHUNKS_PER_STEP = 4
LANES = 128
LOG2_E = math.log2(math.e)
VMEM_LIMIT = 56 * 1024 * 1024

COL_A_Q, COL_A_FF, COL_A_FB, COL_A_I, COL_A_G = 0, 512, 1024, 1536, 2048
COL_B_Q = 2560
COL_C_Q, COL_C_K, COL_C_V = 3072, 3584, 4096
COL_C_GATE = 4608
COL_MERGE = 5120
COL_B_K, COL_B_V = 8192, 8320
COL_GB = 8448
IN_COLS_PAD = 8704

_f32 = jnp.float32
_bf16 = jnp.bfloat16


def _dot(a, b):
    return jnp.dot(a.astype(_bf16), b.astype(_bf16), preferred_element_type=_f32)


def _dot_nt(a, b):
    return lax.dot_general(a.astype(_bf16), b.astype(_bf16), (((1,), (1,)), ((), ())),
                           preferred_element_type=_f32)


def _dot_tn(a, b):
    return lax.dot_general(a.astype(_bf16), b.astype(_bf16), (((0,), (0,)), ((), ())),
                           preferred_element_type=_f32)


def _split3(x):
    hi = x.astype(_bf16)
    r1 = x - hi.astype(_f32)
    mid = r1.astype(_bf16)
    lo = (r1 - mid.astype(_f32)).astype(_bf16)
    return hi, mid, lo


def _exact_dot(sel, parts):
    dot = lambda p: jnp.dot(sel, p, preferred_element_type=_f32)
    return dot(parts[0]) + dot(parts[1]) + dot(parts[2])


def _silu(x):
    return x * jax.nn.sigmoid(x)


def _iota2(shape, dim):
    return lax.broadcasted_iota(jnp.int32, shape, dim)


def _params(sem):
    return pltpu.CompilerParams(dimension_semantics=sem, vmem_limit_bytes=VMEM_LIMIT)


def _resident(shape):
    return pl.BlockSpec(shape, lambda *_: (0,) * len(shape), pipeline_mode=pl.Buffered(1))


def _col_tiles(w, tn):
    k, n = w.shape
    return w.reshape(k, n // tn, tn).transpose(1, 0, 2)


def _matmul_body(x_ref, w_ref, o_ref, xb_ref):
    j = pl.program_id(1)

    @pl.when(j == 0)
    def _():
        xb_ref[...] = x_ref[...].astype(_bf16)

    o_ref[...] = jnp.dot(xb_ref[...], w_ref[j], preferred_element_type=_f32)


def _matmul(x, w_tiles, tm):
    t, k = x.shape
    nt, _, tn = w_tiles.shape
    n = nt * tn
    return pl.pallas_call(
        _matmul_body,
        grid=(t // tm, nt),
        in_specs=[pl.BlockSpec((tm, k), lambda i, j: (i, 0)), _resident(w_tiles.shape)],
        out_specs=pl.BlockSpec((tm, tn), lambda i, j: (i, j)),
        out_shape=jax.ShapeDtypeStruct((t, n), _f32),
        scratch_shapes=[pltpu.VMEM((tm, k), _bf16)],
        compiler_params=_params(("parallel", "arbitrary")),
        name="dense_proj",
    )(x, w_tiles)


def _chunk_masks(c, reverse):
    row = _iota2((c, c), 0)
    col = _iota2((c, c), 1)
    if reverse:
        return col >= row, col > row
    return col <= row, col < row


def _level_mask(c, blk, reverse):
    row = _iota2((c, c), 0)
    col = _iota2((c, c), 1)
    half = blk // 2
    same = (row // blk) == (col // blk)
    r_hi = (row % blk) >= half
    c_hi = (col % blk) >= half
    if reverse:
        return same & jnp.logical_not(r_hi) & c_hi
    return same & r_hi & jnp.logical_not(c_hi)


def _block_ref_rows(g, blk, reverse):
    c, n = g.shape
    idx = blk // 2 if reverse else blk // 2 - 1
    rows = max(blk, 8)
    g3 = g.reshape(c // rows, rows, n)
    pick = lambda r: jnp.broadcast_to(g3[:, r:r + 1, :], g3.shape)
    out = pick(idx)
    if blk < rows:
        sub = lax.broadcasted_iota(jnp.int32, g3.shape, 1)
        for b in range(1, rows // blk):
            out = jnp.where(sub >= b * blk, pick(b * blk + idx), out)
    return out.reshape(c, n)


def _levels(c):
    out, blk = [], c
    while blk >= 2:
        out.append(blk)
        blk //= 2
    return out


def _hgrn_body(first_f_ref, first_b_ref, zq_f, zf_f, zi_f, zq_b, zf_b, zi_b, lb_ref, o_f, o_b, st_ref):
    c = CHUNK
    step = pl.program_id(0)

    @pl.when(first_f_ref[step] == 1)
    def _():
        st_ref[0:HEADS] = jnp.zeros((HEADS, HEAD_DIM, HEAD_DIM), _f32)

    @pl.when(first_b_ref[step] == 1)
    def _():
        st_ref[HEADS:2 * HEADS] = jnp.zeros((HEADS, HEAD_DIM, HEAD_DIM), _f32)

    row = _iota2((c, c), 0)
    col = _iota2((c, c), 1)
    eye = row == col
    blks = _levels(c)
    units = []
    for d, (zq_ref, zf_ref, zi_ref) in enumerate(((zq_f, zf_f, zi_f), (zq_b, zf_b, zi_b))):
        reverse = d == 1
        incl, _ = _chunk_masks(c, reverse)
        masks = [_level_mask(c, blk, reverse) for blk in blks]
        zf = zf_ref[...]
        log_lb = lb_ref[3 * d:3 * d + 1, :]
        log_1m_lb = lb_ref[3 * d + 1:3 * d + 2, :]
        one_m_lb = lb_ref[3 * d + 2:3 * d + 3, :]
        ez = jnp.exp(-jnp.abs(zf))
        one_p = 1.0 + ez
        log_sig = jnp.minimum(zf, 0.0) - jnp.log(one_p)
        b = log_1m_lb + log_sig
        log_f = jnp.maximum(log_lb, b) + jnp.log(1.0 + jnp.exp(-jnp.abs(log_lb - b)))
        key_all = one_m_lb * (jnp.where(zf >= 0.0, ez, 1.0) / one_p)
        gc_all = _exact_dot(incl.astype(_bf16), _split3(log_f))
        ref_all = [_block_ref_rows(gc_all, blk, reverse) for blk in blks]
        rpos = _iota2((c, HEAD_DIM), 0)
        later = [((rpos % blk) < blk // 2) if reverse else ((rpos % blk) >= blk // 2) for blk in blks]
        sgn = [jnp.where(m, LOG2_E, -LOG2_E) for m in later]
        q_all = _silu(zq_ref[...]) * (HEAD_DIM ** -0.5)
        v_all = zi_ref[...]
        tot_row = 0 if reverse else c - 1
        for h in range(HEADS):
            sl = slice(h * HEAD_DIM, (h + 1) * HEAD_DIM)
            units.append(dict(q=q_all[:, sl], k=key_all[:, sl], v=v_all[:, sl], gc=gc_all[:, sl],
                              refs=[r[:, sl] for r in ref_all],
                              masks=masks, sgn=sgn, st=st_ref[d * HEADS + h],
                              g_tot=gc_all[tot_row:tot_row + 1, sl]))

    scores = [jnp.where(eye, _dot_nt(u["q"], u["k"]), 0.0) for u in units]
    for l in range(len(blks)):
        e = [jnp.exp2((u["gc"] - u["refs"][l]) * u["sgn"][l]) for u in units]
        scores = [scores[i] + jnp.where(u["masks"][l], _dot_nt(u["q"] * e[i], u["k"] * e[i]), 0.0)
                  for i, u in enumerate(units)]
    out = [_dot(scores[i], u["v"]) + _dot_nt(u["q"] * jnp.exp(u["gc"]), u["st"]) for i, u in enumerate(units)]
    st_new = [u["st"] * jnp.exp(u["g_tot"]) + _dot_tn(u["v"], u["k"] * jnp.exp(u["g_tot"] - u["gc"]))
              for u in units]
    for i in range(len(units)):
        d, h = divmod(i, HEADS)
        st_ref[i] = st_new[i]
        (o_f, o_b)[d][:, h * HEAD_DIM:(h + 1) * HEAD_DIM] = out[i]


def _hgrn(h, lb_consts, first_f, first_b):
    t = h.shape[0]
    n = t // CHUNK
    fwd = lambda col: pl.BlockSpec((CHUNK, WIDTH), lambda i, a, b: (i, col // WIDTH))
    bwd = lambda col: pl.BlockSpec((CHUNK, WIDTH), lambda i, a, b: (n - 1 - i, col // WIDTH))
    out = jax.ShapeDtypeStruct((t, WIDTH), _f32)
    return pl.pallas_call(
        _hgrn_body,
        grid_spec=pltpu.PrefetchScalarGridSpec(
            num_scalar_prefetch=2,
            grid=(n,),
            in_specs=[fwd(COL_A_Q), fwd(COL_A_FF), fwd(COL_A_I), bwd(COL_A_Q), bwd(COL_A_FB), bwd(COL_A_I),
                      pl.BlockSpec((8, WIDTH), lambda i, a, b: (0, 0))],
            out_specs=[fwd(0), bwd(0)],
            scratch_shapes=[pltpu.VMEM((2 * HEADS, HEAD_DIM, HEAD_DIM), _f32)]),
        out_shape=[out, out],
        compiler_params=_params(("arbitrary",)),
        name="hgrn",
    )(first_f, first_b, h, h, h, h, h, h, lb_consts)


def _gdn_prep_body(first_ref, last_ref, prev_ref, cur_ref, next_ref, w_ref, o_ref):
    i = pl.program_id(0)
    tm = cur_ref.shape[0]
    halo = CONV_K // 2
    cur = cur_ref[...]
    prev = jnp.where(first_ref[i] == 1, 0.0, prev_ref[...])
    nxt = jnp.where(last_ref[i] == 1, 0.0, next_ref[...])
    ext = jnp.concatenate([prev, cur, nxt], axis=0)
    acc = jnp.zeros_like(cur)
    for j in range(CONV_K):
        off = 8 - halo + j
        acc = acc + ext[off:off + tm, :] * w_ref[j:j + 1, :]
    y = _silu(acc)
    for part in range(3):
        for h in range(HEADS):
            lo = part * WIDTH + h * HEAD_DIM
            x = y[:, lo:lo + HEAD_DIM]
            if part < 2:
                x = x * lax.rsqrt(jnp.sum(x * x, axis=1, keepdims=True) + 1e-6)
            if part == 0:
                x = x * (HEAD_DIM ** -0.5)
            o_ref[:, lo:lo + HEAD_DIM] = x


def _gdn_prep(h, conv_w, first, last, tm):
    t = h.shape[0]
    n = t // tm
    r8 = tm // 8
    cb = COL_C_Q // (3 * WIDTH)
    return pl.pallas_call(
        _gdn_prep_body,
        grid_spec=pltpu.PrefetchScalarGridSpec(
            num_scalar_prefetch=2,
            grid=(n,),
            in_specs=[pl.BlockSpec((8, 3 * WIDTH), lambda i, f, l: (jnp.maximum(i * r8 - 1, 0), cb)),
                      pl.BlockSpec((tm, 3 * WIDTH), lambda i, f, l: (i, cb)),
                      pl.BlockSpec((8, 3 * WIDTH), lambda i, f, l: (jnp.minimum((i + 1) * r8, n * r8 - 1), cb)),
                      pl.BlockSpec((8, 3 * WIDTH), lambda i, f, l: (0, 0))],
            out_specs=pl.BlockSpec((tm, 3 * WIDTH), lambda i, f, l: (i, 0))),
        out_shape=jax.ShapeDtypeStruct((t, 3 * WIDTH), _f32),
        compiler_params=_params(("parallel",)),
        name="gdn_prep",
    )(first, last, h, h, h, conv_w)


def _gdn_body(first_f_ref, first_b_ref, q_f, k_f, v_f, gb_f, q_b, k_b, v_b, gb_b, cst_ref, o_f, o_b, st_ref):
    c = CHUNK
    step = pl.program_id(0)

    @pl.when(first_f_ref[step] == 1)
    def _():
        st_ref[0:HEADS] = jnp.zeros((HEADS, HEAD_DIM, HEAD_DIM), _f32)

    @pl.when(first_b_ref[step] == 1)
    def _():
        st_ref[HEADS:2 * HEADS] = jnp.zeros((HEADS, HEAD_DIM, HEAD_DIM), _f32)

    nsub = q_f.shape[0] // c
    row = _iota2((c, c), 0)
    col = _iota2((c, c), 1)
    eye = (row == col).astype(_f32)
    neg_a = cst_ref[0:1, :]
    dt_bias = cst_ref[1:2, :]
    sizes = _levels(c)[::-1]
    units = []
    for d, (q_ref, k_ref, v_ref, gb_ref) in enumerate(((q_f, k_f, v_f, gb_f), (q_b, k_b, v_b, gb_b))):
        reverse = d == 1
        incl, strict = _chunk_masks(c, reverse)
        tri = incl.astype(_bf16)
        masks = [_level_mask(c, blk, reverse) for blk in sizes]
        gb = gb_ref[...]
        beta_t = jax.nn.sigmoid(gb)
        zz = gb + dt_bias
        g_in = neg_a * (jnp.maximum(zz, 0.0) + jnp.log1p(jnp.exp(-jnp.abs(zz))))
        tot_row = 0 if reverse else c - 1
        for s in (range(nsub - 1, -1, -1) if reverse else range(nsub)):
            rs = slice(s * c, (s + 1) * c)
            gc_t = _exact_dot(tri, _split3(g_in[rs]))
            gc_tt = jnp.transpose(gc_t)
            for h in range(HEADS):
                sl = slice(h * HEAD_DIM, (h + 1) * HEAD_DIM)
                j = 2 * HEADS + d * HEADS + h
                g_col = gc_t[:, j:j + 1]
                decay = jnp.where(incl, jnp.exp(jnp.minimum(g_col - gc_tt[j:j + 1, :], 0.0)), 0.0)
                units.append(dict(q=q_ref[rs, sl], k=k_ref[rs, sl], v=v_ref[rs, sl], rows=rs,
                                  g_col=g_col, decay=decay, strict=strict, masks=masks,
                                  beta=beta_t[rs, d * HEADS + h:d * HEADS + h + 1],
                                  g_tot=gc_t[tot_row:tot_row + 1, j:j + 1]))

    a = [jnp.where(u["strict"], _dot_nt(u["k"], u["k"]) * u["decay"] * u["beta"], 0.0) for u in units]
    qk = [_dot_nt(u["q"], u["k"]) * u["decay"] for u in units]
    inv = [eye - jnp.where(u["masks"][0], a[i], 0.0) for i, u in enumerate(units)]
    for l in range(1, len(sizes)):
        t1 = [_dot(jnp.where(u["masks"][l], a[i], 0.0), inv[i]) for i, u in enumerate(units)]
        inv = [inv[i] - _dot(inv[i], t1[i]) for i in range(len(units))]
    e_g = [jnp.exp(u["g_col"]) for u in units]
    uu = [_dot(inv[i], u["v"] * u["beta"]) for i, u in enumerate(units)]
    ww = [_dot(inv[i], u["k"] * (u["beta"] * e_g[i])) for i, u in enumerate(units)]
    qd = [u["q"] * e_g[i] for i, u in enumerate(units)]
    kd = [u["k"] * jnp.exp(u["g_tot"] - u["g_col"]) for u in units]

    st = [st_ref[i] for i in range(2 * HEADS)]
    for slot in range(nsub):
        for d in range(2):
            ids = [(d * nsub + slot) * HEADS + h for h in range(HEADS)]
            v_new = [uu[i] - _dot_nt(ww[i], st[d * HEADS + h]) for h, i in enumerate(ids)]
            out = [_dot_nt(qd[i], st[d * HEADS + h]) + _dot(qk[i], v_new[h]) for h, i in enumerate(ids)]
            for h, i in enumerate(ids):
                st[d * HEADS + h] = (st[d * HEADS + h] * jnp.exp(units[i]["g_tot"])
                                     + _dot_tn(v_new[h], kd[i]))
                (o_f, o_b)[d][units[i]["rows"], h * HEAD_DIM:(h + 1) * HEAD_DIM] = out[h]
    for i in range(2 * HEADS):
        st_ref[i] = st[i]


def _gdn(qkv, h, consts, first_f, first_b, rows):
    t = qkv.shape[0]
    n = t // rows
    fwd = lambda w, colblk: pl.BlockSpec((rows, w), lambda i, a, b: (i, colblk))
    bwd = lambda w, colblk: pl.BlockSpec((rows, w), lambda i, a, b: (n - 1 - i, colblk))
    gcol = COL_GB // LANES
    out = jax.ShapeDtypeStruct((t, WIDTH), _f32)
    return pl.pallas_call(
        _gdn_body,
        grid_spec=pltpu.PrefetchScalarGridSpec(
            num_scalar_prefetch=2,
            grid=(n,),
            in_specs=[fwd(WIDTH, 0), fwd(WIDTH, 1), fwd(WIDTH, 2), fwd(LANES, gcol),
                      bwd(WIDTH, 0), bwd(WIDTH, 1), bwd(WIDTH, 2), bwd(LANES, gcol),
                      pl.BlockSpec((8, LANES), lambda i, a, b: (0, 0))],
            out_specs=[fwd(WIDTH, 0), bwd(WIDTH, 0)],
            scratch_shapes=[pltpu.VMEM((2 * HEADS, HEAD_DIM, HEAD_DIM), _f32)]),
        out_shape=[out, out],
        compiler_params=_params(("arbitrary",)),
        name="gdn",
    )(first_f, first_b, qkv, qkv, qkv, h, qkv, qkv, qkv, h, consts)


def _rope(x, cos, sin_signed):
    lane = _iota2(x.shape, 1) % B_HEAD_DIM
    half = B_HEAD_DIM // 2
    rot = jnp.where(lane < half, pltpu.roll(x, LANES - half, 1), pltpu.roll(x, half, 1))
    return x * cos + rot * sin_signed


def _swa_body(first_ref, last_ref, pos_ref, q_ref, kp_ref, kc_ref, kn_ref, vp_ref, vc_ref, vn_ref,
              cp_ref, sp_ref, cc_ref, sc_ref, cn_ref, sn_ref, sink_ref, o_ref):
    i = pl.program_id(0)
    blk = B_BLOCK
    has_prev = first_ref[i] == 0
    has_next = last_ref[i] == 0
    k_all = jnp.concatenate([_rope(kp_ref[...], cp_ref[...], sp_ref[...]),
                             _rope(kc_ref[...], cc_ref[...], sc_ref[...]),
                             _rope(kn_ref[...], cn_ref[...], sn_ref[...])], axis=0)
    v_all = jnp.concatenate([vp_ref[...], vc_ref[...], vn_ref[...]], axis=0)
    nq = B_GROUP * blk
    kpos = _iota2((3 * blk, nq), 0)
    qpos = _iota2((3 * blk, nq), 1) % blk
    rel = kpos - qpos
    ok = (rel >= 0) & (rel <= 2 * WINDOW)
    ok = ok & (has_prev | (kpos >= blk)) & (has_next | (kpos < 2 * blk))
    cos, sin_signed = cc_ref[...], sc_ref[...]
    scale = B_HEAD_DIM ** -0.5
    qr = [_rope(q_ref[:, p * LANES:(p + 1) * LANES], cos, sin_signed) * scale for p in range(B_Q_HEADS // 2)]
    heads = [qr[hq // 2][:, (hq % 2) * B_HEAD_DIM:(hq % 2 + 1) * B_HEAD_DIM] for hq in range(B_Q_HEADS)]
    sink_row = sink_ref[0:1, :]
    kvs = range(B_KV_HEADS)
    q4 = [jnp.concatenate(heads[kv * B_GROUP:(kv + 1) * B_GROUP], axis=0) for kv in kvs]
    sink = [jnp.concatenate([jnp.broadcast_to(sink_row[:, kv * B_GROUP + g:kv * B_GROUP + g + 1], (1, blk))
                             for g in range(B_GROUP)], axis=1) for kv in kvs]
    kh = [k_all[:, kv * B_HEAD_DIM:(kv + 1) * B_HEAD_DIM] for kv in kvs]
    vh = [v_all[:, kv * B_HEAD_DIM:(kv + 1) * B_HEAD_DIM] for kv in kvs]
    s = [jnp.where(ok, _dot_nt(kh[kv], q4[kv]), -jnp.inf) for kv in kvs]
    m = [jnp.maximum(jnp.max(s[kv], axis=0, keepdims=True), sink[kv]) for kv in kvs]
    p = [jnp.exp(s[kv] - m[kv]) for kv in kvs]
    denom = [jnp.sum(p[kv], axis=0, keepdims=True) + jnp.exp(sink[kv] - m[kv]) for kv in kvs]
    o_t = [_dot_tn(vh[kv], p[kv]) / denom[kv] for kv in kvs]
    for kv in kvs:
        for g in range(0, B_GROUP, 2):
            hq = kv * B_GROUP + g
            pair = jnp.concatenate([o_t[kv][:, g * blk:(g + 1) * blk],
                                    o_t[kv][:, (g + 1) * blk:(g + 2) * blk]], axis=0)
            o_ref[:, hq * B_HEAD_DIM:(hq + 2) * B_HEAD_DIM] = jnp.transpose(pair)


def _swa(h, cos_t, sin_t, sink, first, last, pos):
    t = h.shape[0]
    n = t // B_BLOCK
    npos = cos_t.shape[0] // B_BLOCK
    prev = lambda i: jnp.maximum(i - 1, 0)
    nxt = lambda i: jnp.minimum(i + 1, n - 1)
    kcol, vcol = COL_B_K // LANES, COL_B_V // LANES
    hspec = lambda f, colblk: pl.BlockSpec((B_BLOCK, LANES), lambda i, a, b, p: (f(i), colblk))
    tspec = lambda delta: pl.BlockSpec(
        (B_BLOCK, LANES), lambda i, a, b, p: (jnp.clip(p[i] + delta, 0, npos - 1), 0))
    ident = lambda i: i
    qw = B_Q_HEADS * B_HEAD_DIM
    return pl.pallas_call(
        _swa_body,
        grid_spec=pltpu.PrefetchScalarGridSpec(
            num_scalar_prefetch=3,
            grid=(n,),
            in_specs=[pl.BlockSpec((B_BLOCK, qw), lambda i, a, b, p: (i, COL_B_Q // qw)),
                      hspec(prev, kcol), hspec(ident, kcol), hspec(nxt, kcol),
                      hspec(prev, vcol), hspec(ident, vcol), hspec(nxt, vcol),
                      tspec(-1), tspec(-1), tspec(0), tspec(0), tspec(1), tspec(1),
                      pl.BlockSpec((8, LANES), lambda i, a, b, p: (0, 0))],
            out_specs=pl.BlockSpec((B_BLOCK, qw), lambda i, a, b, p: (i, 0))),
        out_shape=jax.ShapeDtypeStruct((t, qw), _f32),
        compiler_params=_params(("parallel",)),
        name="window_attn",
    )(first, last, pos, h, h, h, h, h, h, h, cos_t, sin_t, cos_t, sin_t, cos_t, sin_t, sink)


def _residual_ln(x, y, g, b):
    z = DN_ALPHA * x + y
    mu = jnp.mean(z, axis=1, keepdims=True)
    zc = z - mu
    var = jnp.mean(zc * zc, axis=1, keepdims=True)
    return zc * lax.rsqrt(var + 1e-5) * g + b


def _gated_rms(o, gate, g):
    outs = []
    for h in range(HEADS):
        sl = slice(h * HEAD_DIM, (h + 1) * HEAD_DIM)
        x = o[:, sl]
        x = x * lax.rsqrt(jnp.mean(x * x, axis=1, keepdims=True) + 1e-6) * g
        outs.append(x * _silu(gate[:, sl]))
    return jnp.concatenate(outs, axis=1)


def _merge_body(x_ref, af_ref, ab_ref, ag_ref, ob_ref, cf_ref, cb_ref, cg_ref, ma_ref, mb_ref, mc_ref,
                wa_ref, wb_ref, wc_ref, wo_ref, nrm_ref, ln_ref, o_ref):
    oa = _gated_rms(af_ref[...] + ab_ref[...], ag_ref[...], nrm_ref[0:1, :])
    oc = _gated_rms(cf_ref[...] + cb_ref[...], cg_ref[...], nrm_ref[1:2, :])
    pa = jnp.dot(oa.astype(_bf16), wa_ref[...], preferred_element_type=_f32)
    pb = jnp.dot(ob_ref[...].astype(_bf16), wb_ref[...], preferred_element_type=_f32)
    pc = jnp.dot(oc.astype(_bf16), wc_ref[...], preferred_element_type=_f32)
    mix = (jax.nn.sigmoid(ma_ref[...]) * pa + jax.nn.sigmoid(mb_ref[...]) * pb
           + jax.nn.sigmoid(mc_ref[...]) * pc)
    y = jnp.dot(mix.astype(_bf16), wo_ref[...], preferred_element_type=_f32)
    o_ref[...] = _residual_ln(x_ref[...], y, ln_ref[0:1, :], ln_ref[1:2, :])


def _merge(x, h, oa_f, oa_b, ob, oc_f, oc_b, wa, wb, wc, wo, nrm, ln, tm):
    t = x.shape[0]
    row = lambda w, colblk=0: pl.BlockSpec((tm, w), lambda i: (i, colblk))
    full = lambda a: pl.BlockSpec(a.shape, lambda i: (0, 0))
    mcol = COL_MERGE // D_MODEL
    return pl.pallas_call(
        _merge_body,
        grid=(t // tm,),
        in_specs=[row(D_MODEL), row(WIDTH), row(WIDTH), row(WIDTH, COL_A_G // WIDTH), row(WIDTH),
                  row(WIDTH), row(WIDTH), row(WIDTH, COL_C_GATE // WIDTH),
                  row(D_MODEL, mcol), row(D_MODEL, mcol + 1), row(D_MODEL, mcol + 2),
                  full(wa), full(wb), full(wc), full(wo), full(nrm), full(ln)],
        out_specs=row(D_MODEL),
        out_shape=jax.ShapeDtypeStruct((t, D_MODEL), _f32),
        compiler_params=_params(("parallel",)),
        name="merge_mix",
    )(x, oa_f, oa_b, h, ob, oc_f, oc_b, h, h, h, h, wa, wb, wc, wo, nrm, ln)


def _xattn_body(seq_ref, x_ref, kv_ref, wq_ref, wo_ref, ln_ref, o_ref):
    x = x_ref[...]
    q = jnp.dot(x.astype(_bf16), wq_ref[...], preferred_element_type=_f32)
    outs = []
    for h in range(MEM_HEADS):
        lo = h * MEM_HEAD_DIM
        qh = q[:, lo:lo + MEM_HEAD_DIM]
        kh = kv_ref[0, :, lo:lo + MEM_HEAD_DIM]
        vh = kv_ref[0, :, D_MODEL + lo:D_MODEL + lo + MEM_HEAD_DIM]
        s = _dot_nt(qh, kh) * (MEM_HEAD_DIM ** -0.5)
        p = jnp.exp(s - jnp.max(s, axis=1, keepdims=True))
        p = p / jnp.sum(p, axis=1, keepdims=True)
        outs.append(_dot(p, vh))
    o = jnp.concatenate(outs, axis=1)
    y = jnp.dot(o.astype(_bf16), wo_ref[...], preferred_element_type=_f32)
    o_ref[...] = _residual_ln(x, y, ln_ref[0:1, :], ln_ref[1:2, :])


def _xattn(x, kv, wq, wo, ln, seq_of_tile, tm):
    t = x.shape[0]
    full = lambda a: pl.BlockSpec(a.shape, lambda i, s: (0, 0))
    return pl.pallas_call(
        _xattn_body,
        grid_spec=pltpu.PrefetchScalarGridSpec(
            num_scalar_prefetch=1,
            grid=(t // tm,),
            in_specs=[pl.BlockSpec((tm, D_MODEL), lambda i, s: (i, 0)),
                      pl.BlockSpec((1,) + kv.shape[1:], lambda i, s: (s[i], 0, 0)),
                      full(wq), full(wo), full(ln)],
            out_specs=pl.BlockSpec((tm, D_MODEL), lambda i, s: (i, 0))),
        out_shape=jax.ShapeDtypeStruct((t, D_MODEL), _f32),
        compiler_params=_params(("parallel",)),
        name="mem_xattn",
    )(seq_of_tile, x, kv, wq, wo, ln)


def _ffn_body(x_ref, w1_ref, w2_ref, ln_ref, o_ref):
    x = x_ref[...]
    xb = x.astype(_bf16)
    gate = jnp.dot(xb, w1_ref[0], preferred_element_type=_f32)
    up = jnp.dot(xb, w1_ref[1], preferred_element_type=_f32)
    y = jnp.dot((_silu(gate) * up).astype(_bf16), w2_ref[...], preferred_element_type=_f32)
    o_ref[...] = _residual_ln(x, y, ln_ref[0:1, :], ln_ref[1:2, :])


def _ffn(x, w1, w2, ln, tm):
    t = x.shape[0]
    return pl.pallas_call(
        _ffn_body,
        grid=(t // tm,),
        in_specs=[pl.BlockSpec((tm, D_MODEL), lambda i: (i, 0)),
                  _resident(w1.shape), _resident(w2.shape), _resident(ln.shape)],
        out_specs=pl.BlockSpec((tm, D_MODEL), lambda i: (i, 0)),
        out_shape=jax.ShapeDtypeStruct((t, D_MODEL), _f32),
        compiler_params=_params(("parallel",)),
        name="swiglu_ffn",
    )(x, w1, w2, ln)


def _boundary_tables(seq_lens, tile):
    first, last, pos, seq = [], [], [], []
    for sid, length in enumerate(seq_lens):
        n = length // tile
        for b in range(n):
            first.append(int(b == 0))
            last.append(int(b == n - 1))
            pos.append(b)
            seq.append(sid)
    as_i32 = lambda v: jnp.asarray(np.asarray(v, np.int32))
    return as_i32(first), as_i32(last), as_i32(pos), as_i32(seq)


def _permute_in_cols(w):
    pad = jnp.zeros(w.shape[:-1] + (IN_COLS_PAD - 8464,), w.dtype)
    return jnp.concatenate([w[..., 0:3072], w[..., 3328:4864], w[..., 4864:5376], w[..., 5392:8464],
                            w[..., 3072:3200], w[..., 3200:3328], w[..., 5376:5392], pad], axis=-1)


def _rows8(*rows):
    n = rows[0].shape[-1]
    out = jnp.zeros((8, n), _f32)
    for r, v in enumerate(rows):
        out = out.at[r].set(v.astype(_f32))
    return out


def _lane_pad(v, offset):
    return jnp.zeros((LANES,), _f32).at[offset:offset + v.shape[0]].set(v.astype(_f32))


def kernel(x_prompt, x_sample, mem_prompt, mem_sample, w_in, hgrn_lb_logits, hgrn_norm_g, attn_sink,
           gdn_conv_w, gdn_a_log, gdn_dt_bias, gdn_norm_g, w_branch_a, w_branch_b, w_branch_c, w_mix_out,
           w_mem_q, w_mem_kv, w_mem_o, w_ffn_in, w_ffn_out, ln_g, ln_b):
    depth = w_in.shape[0]
    d = x_prompt.shape[-1]
    seq_lens = (x_prompt.shape[1],) * x_prompt.shape[0] + (x_sample.shape[1],) * x_sample.shape[0]
    n_prompt = x_prompt.shape[0] * x_prompt.shape[1]
    x = jnp.concatenate([x_prompt.reshape(-1, d), x_sample.reshape(-1, d)], axis=0)
    mem = jnp.concatenate([mem_prompt, mem_sample], axis=0)
    n_seq, n_mem, _ = mem.shape
    t = x.shape[0]

    tm = math.gcd(512, *seq_lens)
    tm_proj = math.gcd(1024, t)
    first_c, last_c, _, _ = _boundary_tables(seq_lens, CHUNK)
    last_c_rev = last_c[::-1]
    rows_g = math.gcd(GDN_CHUNKS_PER_STEP * CHUNK, *seq_lens)
    first_g, last_g, _, _ = _boundary_tables(seq_lens, rows_g)
    last_g_rev = last_g[::-1]
    first_b, last_b, pos_b, _ = _boundary_tables(seq_lens, B_BLOCK)
    first_t, last_t, _, seq_t = _boundary_tables(seq_lens, tm)

    s_max = max(seq_lens)
    inv = ROPE_THETA ** (-jnp.arange(0, B_HEAD_DIM, 2, dtype=_f32) / B_HEAD_DIM)
    ang = jnp.arange(s_max, dtype=_f32)[:, None] * inv[None, :]
    cos_t = jnp.tile(jnp.cos(ang), (1, 4))
    sin_t = jnp.tile(jnp.concatenate([-jnp.sin(ang), jnp.sin(ang)], axis=1), (1, 2))

    cum = jnp.cumsum(jax.nn.softmax(hgrn_lb_logits.astype(_f32), axis=1), axis=1)
    lb = cum - cum[:, :1]

    w_in_p = _permute_in_cols(w_in).astype(_bf16)
    bf = lambda w: w.astype(_bf16)
    w_a, w_b, w_c, w_mix = bf(w_branch_a), bf(w_branch_b), bf(w_branch_c), bf(w_mix_out)
    w_q, w_kv, w_o = bf(w_mem_q), bf(w_mem_kv), bf(w_mem_o)
    w_f1, w_f2 = bf(w_ffn_in), bf(w_ffn_out)
    mem2 = mem.reshape(n_seq * n_mem, d)

    for l in range(depth):
        h = _matmul(x, _col_tiles(w_in_p[l], 512), tm_proj)
        lb_rows = []
        for dirn in range(2):
            lbd = lb[dirn, l]
            lb_rows += [jnp.log(lbd), jnp.log1p(-lbd), 1.0 - lbd]
        oa_f, oa_b = _hgrn(h, _rows8(*lb_rows), first_c, last_c_rev)
        o_b = _swa(h, cos_t, sin_t, _rows8(_lane_pad(attn_sink[l], 0)), first_b, last_b, pos_b)
        qkv = _gdn_prep(h, _rows8(*[gdn_conv_w[l, j] for j in range(CONV_K)]), first_t, last_t, tm)
        gconst = _rows8(_lane_pad(-jnp.exp(gdn_a_log[l].astype(_f32)).reshape(-1), 2 * HEADS),
                        _lane_pad(gdn_dt_bias[l].reshape(-1), 2 * HEADS))
        oc_f, oc_b = _gdn(qkv, h, gconst, first_g, last_g_rev, rows_g)
        nrm = _rows8(hgrn_norm_g[l], gdn_norm_g[l])
        x = _merge(x, h, oa_f, oa_b, o_b, oc_f, oc_b, w_a[l], w_b[l], w_c[l], w_mix[l], nrm,
                   _rows8(ln_g[l, 0], ln_b[l, 0]), tm)
        kv = _matmul(mem2, _col_tiles(w_kv[l], 512), n_mem).reshape(n_seq, n_mem, 2 * d)
        x = _xattn(x, kv, w_q[l], w_o[l], _rows8(ln_g[l, 1], ln_b[l, 1]), seq_t, tm)
        x = _ffn(x, _col_tiles(w_f1[l], FFN_HIDDEN), w_f2[l], _rows8(ln_g[l, 2], ln_b[l, 2]), tm)

    y_prompt = x[:n_prompt].reshape(x_prompt.shape)
    y_sample = x[n_prompt:].reshape(x_sample.shape)
    return (y_prompt, y_sample)
```

```python
import math

import numpy as np
import jax
import jax.numpy as jnp
from jax import lax
from jax.experimental import pallas as pl
from jax.experimental.pallas import tpu as pltpu

D_MODEL = 1024
DEPTH = 4
HEADS = 4
HEAD_DIM = 128
WIDTH = HEADS * HEAD_DIM
B_Q_HEADS = 8
B_KV_HEADS = 2
B_GROUP = B_Q_HEADS // B_KV_HEADS
B_HEAD_DIM = 64
WINDOW = 128
B_BLOCK = 128
ROPE_THETA = 10000.0
CONV_K = 5
MEM_HEADS = 4
MEM_HEAD_DIM = D_MODEL // MEM_HEADS
FFN_HIDDEN = 2816
DN_ALPHA = (2 * DEPTH) ** 0.25

CHUNK = 64
GDN_CHUNKS_PER_STEP = 4
HGRN_CHUNKS_PER_STEP = 4
LANES = 128
LOG2_E = math.log2(math.e)
VMEM_LIMIT = 56 * 1024 * 1024

COL_A_Q, COL_A_FF, COL_A_FB, COL_A_I, COL_A_G = 0, 512, 1024, 1536, 2048
COL_B_Q = 2560
COL_C_Q, COL_C_K, COL_C_V = 3072, 3584, 4096
COL_C_GATE = 4608
COL_MERGE = 5120
COL_B_K, COL_B_V = 8192, 8320
COL_GB = 8448
IN_COLS_PAD = 8704

_f32 = jnp.float32
_bf16 = jnp.bfloat16


def _dot(a, b):
    return jnp.dot(a.astype(_bf16), b.astype(_bf16), preferred_element_type=_f32)


def _dot_nt(a, b):
    return lax.dot_general(a.astype(_bf16), b.astype(_bf16), (((1,), (1,)), ((), ())),
                           preferred_element_type=_f32)


def _dot_tn(a, b):
    return lax.dot_general(a.astype(_bf16), b.astype(_bf16), (((0,), (0,)), ((), ())),
                           preferred_element_type=_f32)


def _split3(x):
    hi = x.astype(_bf16)
    r1 = x - hi.astype(_f32)
    mid = r1.astype(_bf16)
    lo = (r1 - mid.astype(_f32)).astype(_bf16)
    return hi, mid, lo


def _exact_dot(sel, parts):
    dot = lambda p: jnp.dot(sel, p, preferred_element_type=_f32)
    return dot(parts[0]) + dot(parts[1]) + dot(parts[2])


def _silu(x):
    return x * jax.nn.sigmoid(x)


def _iota2(shape, dim):
    return lax.broadcasted_iota(jnp.int32, shape, dim)


def _params(sem):
    return pltpu.CompilerParams(dimension_semantics=sem, vmem_limit_bytes=VMEM_LIMIT)


def _resident(shape):
    return pl.BlockSpec(shape, lambda *_: (0,) * len(shape), pipeline_mode=pl.Buffered(1))


def _col_tiles(w, tn):
    k, n = w.shape
    return w.reshape(k, n // tn, tn).transpose(1, 0, 2)


def _matmul_body(x_ref, w_ref, o_ref, xb_ref):
    j = pl.program_id(1)

    @pl.when(j == 0)
    def _():
        xb_ref[...] = x_ref[...].astype(_bf16)

    o_ref[...] = jnp.dot(xb_ref[...], w_ref[j], preferred_element_type=_f32)


def _matmul(x, w_tiles, tm):
    t, k = x.shape
    nt, _, tn = w_tiles.shape
    n = nt * tn
    return pl.pallas_call(
        _matmul_body,
        grid=(t // tm, nt),
        in_specs=[pl.BlockSpec((tm, k), lambda i, j: (i, 0)), _resident(w_tiles.shape)],
        out_specs=pl.BlockSpec((tm, tn), lambda i, j: (i, j)),
        out_shape=jax.ShapeDtypeStruct((t, n), _f32),
        scratch_shapes=[pltpu.VMEM((tm, k), _bf16)],
        compiler_params=_params(("parallel", "arbitrary")),
        name="dense_proj",
    )(x, w_tiles)


def _chunk_masks(c, reverse):
    row = _iota2((c, c), 0)
    col = _iota2((c, c), 1)
    if reverse:
        return col >= row, col > row
    return col <= row, col < row


def _level_mask(c, blk, reverse):
    row = _iota2((c, c), 0)
    col = _iota2((c, c), 1)
    half = blk // 2
    same = (row // blk) == (col // blk)
    r_hi = (row % blk) >= half
    c_hi = (col % blk) >= half
    if reverse:
        return same & jnp.logical_not(r_hi) & c_hi
    return same & r_hi & jnp.logical_not(c_hi)


def _block_ref_rows(g, blk, reverse):
    c, n = g.shape
    idx = blk // 2 if reverse else blk // 2 - 1
    rows = max(blk, 8)
    g3 = g.reshape(c // rows, rows, n)
    pick = lambda r: jnp.broadcast_to(g3[:, r:r + 1, :], g3.shape)
    out = pick(idx)
    if blk < rows:
        sub = lax.broadcasted_iota(jnp.int32, g3.shape, 1)
        for b in range(1, rows // blk):
            out = jnp.where(sub >= b * blk, pick(b * blk + idx), out)
    return out.reshape(c, n)


def _levels(c):
    out, blk = [], c
    while blk >= 2:
        out.append(blk)
        blk //= 2
    return out


def _hgrn_body(first_f_ref, first_b_ref, zq_f, zf_f, zi_f, zq_b, zf_b, zi_b, lb_ref, o_f, o_b, st_ref):
    c = CHUNK
    step = pl.program_id(0)

    @pl.when(first_f_ref[step] == 1)
    def _():
        st_ref[0:HEADS] = jnp.zeros((HEADS, HEAD_DIM, HEAD_DIM), _f32)

    @pl.when(first_b_ref[step] == 1)
    def _():
        st_ref[HEADS:2 * HEADS] = jnp.zeros((HEADS, HEAD_DIM, HEAD_DIM), _f32)

    nsub = zq_f.shape[0] // c
    row = _iota2((c, c), 0)
    col = _iota2((c, c), 1)
    eye = row == col
    blks = _levels(c)
    units = []
    for d, (zq_ref, zf_ref, zi_ref) in enumerate(((zq_f, zf_f, zi_f), (zq_b, zf_b, zi_b))):
        reverse = d == 1
        incl, _ = _chunk_masks(c, reverse)
        tri = incl.astype(_bf16)
        masks = [_level_mask(c, blk, reverse) for blk in blks]
        zf = zf_ref[...]
        log_lb = lb_ref[3 * d:3 * d + 1, :]
        log_1m_lb = lb_ref[3 * d + 1:3 * d + 2, :]
        one_m_lb = lb_ref[3 * d + 2:3 * d + 3, :]
        ez = jnp.exp(-jnp.abs(zf))
        one_p = 1.0 + ez
        log_sig = jnp.minimum(zf, 0.0) - jnp.log(one_p)
        b = log_1m_lb + log_sig
        log_f = jnp.maximum(log_lb, b) + jnp.log(1.0 + jnp.exp(-jnp.abs(log_lb - b)))
        key_all = one_m_lb * (jnp.where(zf >= 0.0, ez, 1.0) / one_p)
        rpos = _iota2((c, HEAD_DIM), 0)
        later = [((rpos % blk) < blk // 2) if reverse else ((rpos % blk) >= blk // 2) for blk in blks]
        sgn = [jnp.where(m, LOG2_E, -LOG2_E) for m in later]
        q_all = _silu(zq_ref[...]) * (HEAD_DIM ** -0.5)
        v_all = zi_ref[...]
        tot_row = 0 if reverse else c - 1
        for s in (range(nsub - 1, -1, -1) if reverse else range(nsub)):
            rs = slice(s * c, (s + 1) * c)
            gc_all = _exact_dot(tri, _split3(log_f[rs]))
            ref_all = [_block_ref_rows(gc_all, blk, reverse) for blk in blks]
            for h in range(HEADS):
                sl = slice(h * HEAD_DIM, (h + 1) * HEAD_DIM)
                units.append(dict(q=q_all[rs, sl], k=key_all[rs, sl], v=v_all[rs, sl], gc=gc_all[:, sl],
                                  refs=[r[:, sl] for r in ref_all], masks=masks, sgn=sgn, rows=rs,
                                  g_tot=gc_all[tot_row:tot_row + 1, sl]))

    scores = [jnp.where(eye, _dot_nt(u["q"], u["k"]), 0.0) for u in units]
    for l in range(len(blks)):
        e = [jnp.exp2((u["gc"] - u["refs"][l]) * u["sgn"][l]) for u in units]
        scores = [jnp.where(u["masks"][l], _dot_nt(u["q"] * e[i], u["k"] * e[i]), scores[i])
                  for i, u in enumerate(units)]
    intra = [_dot(scores[i], u["v"]) for i, u in enumerate(units)]
    qd = [u["q"] * jnp.exp(u["gc"]) for u in units]
    kd = [u["k"] * jnp.exp(u["g_tot"] - u["gc"]) for u in units]

    st = [st_ref[i] for i in range(2 * HEADS)]
    for slot in range(nsub):
        for d in range(2):
            for h in range(HEADS):
                i, j = (d * nsub + slot) * HEADS + h, d * HEADS + h
                out = intra[i] + _dot_nt(qd[i], st[j])
                st[j] = st[j] * jnp.exp(units[i]["g_tot"]) + _dot_tn(units[i]["v"], kd[i])
                (o_f, o_b)[d][units[i]["rows"], h * HEAD_DIM:(h + 1) * HEAD_DIM] = out
    for i in range(2 * HEADS):
        st_ref[i] = st[i]


def _hgrn(h, lb_consts, first_f, first_b, rows):
    t = h.shape[0]
    n = t // rows
    fwd = lambda col: pl.BlockSpec((rows, WIDTH), lambda i, a, b: (i, col // WIDTH))
    bwd = lambda col: pl.BlockSpec((rows, WIDTH), lambda i, a, b: (n - 1 - i, col // WIDTH))
    out = jax.ShapeDtypeStruct((t, WIDTH), _f32)
    return pl.pallas_call(
        _hgrn_body,
        grid_spec=pltpu.PrefetchScalarGridSpec(
            num_scalar_prefetch=2,
            grid=(n,),
            in_specs=[fwd(COL_A_Q), fwd(COL_A_FF), fwd(COL_A_I), bwd(COL_A_Q), bwd(COL_A_FB), bwd(COL_A_I),
                      pl.BlockSpec((8, WIDTH), lambda i, a, b: (0, 0))],
            out_specs=[fwd(0), bwd(0)],
            scratch_shapes=[pltpu.VMEM((2 * HEADS, HEAD_DIM, HEAD_DIM), _f32)]),
        out_shape=[out, out],
        compiler_params=_params(("arbitrary",)),
        name="hgrn",
    )(first_f, first_b, h, h, h, h, h, h, lb_consts)


def _gdn_prep_body(first_ref, last_ref, prev_ref, cur_ref, next_ref, w_ref, o_ref, ext_ref):
    i = pl.program_id(0)
    tm = cur_ref.shape[0]
    halo = CONV_K // 2
    ext_ref[0:8, :] = jnp.where(first_ref[i] == 1, 0.0, prev_ref[...])
    ext_ref[8:8 + tm, :] = cur_ref[...]
    ext_ref[8 + tm:16 + tm, :] = jnp.where(last_ref[i] == 1, 0.0, next_ref[...])
    acc = cur_ref[...] * w_ref[halo:halo + 1, :]
    for j in range(CONV_K):
        if j != halo:
            off = 8 - halo + j
            acc = acc + ext_ref[off:off + tm, :] * w_ref[j:j + 1, :]
    y = _silu(acc)
    for part in range(3):
        for h in range(HEADS):
            lo = part * WIDTH + h * HEAD_DIM
            x = y[:, lo:lo + HEAD_DIM]
            if part < 2:
                x = x * lax.rsqrt(jnp.sum(x * x, axis=1, keepdims=True) + 1e-6)
            if part == 0:
                x = x * (HEAD_DIM ** -0.5)
            o_ref[:, lo:lo + HEAD_DIM] = x


def _gdn_prep(h, conv_w, first, last, tm):
    t = h.shape[0]
    n = t // tm
    r8 = tm // 8
    cb = COL_C_Q // (3 * WIDTH)
    return pl.pallas_call(
        _gdn_prep_body,
        grid_spec=pltpu.PrefetchScalarGridSpec(
            num_scalar_prefetch=2,
            grid=(n,),
            in_specs=[pl.BlockSpec((8, 3 * WIDTH), lambda i, f, l: (jnp.maximum(i * r8 - 1, 0), cb)),
                      pl.BlockSpec((tm, 3 * WIDTH), lambda i, f, l: (i, cb)),
                      pl.BlockSpec((8, 3 * WIDTH), lambda i, f, l: (jnp.minimum((i + 1) * r8, n * r8 - 1), cb)),
                      pl.BlockSpec((8, 3 * WIDTH), lambda i, f, l: (0, 0))],
            out_specs=pl.BlockSpec((tm, 3 * WIDTH), lambda i, f, l: (i, 0)),
            scratch_shapes=[pltpu.VMEM((tm + 16, 3 * WIDTH), _f32)]),
        out_shape=jax.ShapeDtypeStruct((t, 3 * WIDTH), _f32),
        compiler_params=_params(("parallel",)),
        name="gdn_prep",
    )(first, last, h, h, h, conv_w)


def _gdn_body(first_f_ref, first_b_ref, q_f, k_f, v_f, gb_f, q_b, k_b, v_b, gb_b, cst_ref, o_f, o_b, st_ref):
    c = CHUNK
    step = pl.program_id(0)

    @pl.when(first_f_ref[step] == 1)
    def _():
        st_ref[0:HEADS] = jnp.zeros((HEADS, HEAD_DIM, HEAD_DIM), _f32)

    @pl.when(first_b_ref[step] == 1)
    def _():
        st_ref[HEADS:2 * HEADS] = jnp.zeros((HEADS, HEAD_DIM, HEAD_DIM), _f32)

    nsub = q_f.shape[0] // c
    row = _iota2((c, c), 0)
    col = _iota2((c, c), 1)
    eye = (row == col).astype(_f32)
    neg_a = cst_ref[0:1, :]
    dt_bias = cst_ref[1:2, :]
    sizes = _levels(c)[::-1]
    units = []
    for d, (q_ref, k_ref, v_ref, gb_ref) in enumerate(((q_f, k_f, v_f, gb_f), (q_b, k_b, v_b, gb_b))):
        reverse = d == 1
        incl, strict = _chunk_masks(c, reverse)
        tri = incl.astype(_bf16)
        masks = [_level_mask(c, blk, reverse) for blk in sizes]
        gb = gb_ref[...]
        beta_t = jax.nn.sigmoid(gb)
        zz = gb + dt_bias
        g_in = neg_a * (jnp.maximum(zz, 0.0) + jnp.log1p(jnp.exp(-jnp.abs(zz))))
        tot_row = 0 if reverse else c - 1
        for s in (range(nsub - 1, -1, -1) if reverse else range(nsub)):
            rs = slice(s * c, (s + 1) * c)
            gc_t = _exact_dot(tri, _split3(g_in[rs]))
            gc_tt = jnp.transpose(gc_t)
            for h in range(HEADS):
                sl = slice(h * HEAD_DIM, (h + 1) * HEAD_DIM)
                j = 2 * HEADS + d * HEADS + h
                g_col = gc_t[:, j:j + 1]
                decay = jnp.where(incl, jnp.exp(jnp.minimum(g_col - gc_tt[j:j + 1, :], 0.0)), 0.0)
                units.append(dict(q=q_ref[rs, sl], k=k_ref[rs, sl], v=v_ref[rs, sl], rows=rs,
                                  g_col=g_col, decay=decay, strict=strict, masks=masks,
                                  beta=beta_t[rs, d * HEADS + h:d * HEADS + h + 1],
                                  g_tot=gc_t[tot_row:tot_row + 1, j:j + 1]))

    k_b = [u["k"].astype(_bf16) for u in units]
    a = [jnp.where(u["strict"], _dot_nt(k_b[i], k_b[i]) * u["decay"] * u["beta"], 0.0) for i, u in enumerate(units)]
    qk = [_dot_nt(u["q"], k_b[i]) * u["decay"] for i, u in enumerate(units)]
    inv = [eye - jnp.where(u["masks"][0], a[i], 0.0) for i, u in enumerate(units)]
    for l in range(1, len(sizes)):
        inv_b = [m.astype(_bf16) for m in inv]
        t1 = [_dot(jnp.where(u["masks"][l], a[i], 0.0), inv_b[i]) for i, u in enumerate(units)]
        inv = [inv[i] - _dot(inv_b[i], t1[i]) for i in range(len(units))]
    inv_b = [m.astype(_bf16) for m in inv]
    e_g = [jnp.exp(u["g_col"]) for u in units]
    uu = [_dot(inv_b[i], u["v"] * u["beta"]) for i, u in enumerate(units)]
    ww = [_dot(inv_b[i], u["k"] * (u["beta"] * e_g[i])) for i, u in enumerate(units)]
    qd = [u["q"] * e_g[i] for i, u in enumerate(units)]
    kd = [u["k"] * jnp.exp(u["g_tot"] - u["g_col"]) for u in units]

    st = [st_ref[i] for i in range(2 * HEADS)]
    for slot in range(nsub):
        for d in range(2):
            ids = [(d * nsub + slot) * HEADS + h for h in range(HEADS)]
            st_b = [st[d * HEADS + h].astype(_bf16) for h in range(HEADS)]
            v_new = [uu[i] - _dot_nt(ww[i], st_b[h]) for h, i in enumerate(ids)]
            out = [_dot_nt(qd[i], st_b[h]) + _dot(qk[i], v_new[h]) for h, i in enumerate(ids)]
            for h, i in enumerate(ids):
                st[d * HEADS + h] = (st[d * HEADS + h] * jnp.exp(units[i]["g_tot"])
                                     + _dot_tn(v_new[h], kd[i]))
                (o_f, o_b)[d][units[i]["rows"], h * HEAD_DIM:(h + 1) * HEAD_DIM] = out[h]
    for i in range(2 * HEADS):
        st_ref[i] = st[i]


def _gdn(qkv, h, consts, first_f, first_b, rows):
    t = qkv.shape[0]
    n = t // rows
    fwd = lambda w, colblk: pl.BlockSpec((rows, w), lambda i, a, b: (i, colblk))
    bwd = lambda w, colblk: pl.BlockSpec((rows, w), lambda i, a, b: (n - 1 - i, colblk))
    gcol = COL_GB // LANES
    out = jax.ShapeDtypeStruct((t, WIDTH), _f32)
    return pl.pallas_call(
        _gdn_body,
        grid_spec=pltpu.PrefetchScalarGridSpec(
            num_scalar_prefetch=2,
            grid=(n,),
            in_specs=[fwd(WIDTH, 0), fwd(WIDTH, 1), fwd(WIDTH, 2), fwd(LANES, gcol),
                      bwd(WIDTH, 0), bwd(WIDTH, 1), bwd(WIDTH, 2), bwd(LANES, gcol),
                      pl.BlockSpec((8, LANES), lambda i, a, b: (0, 0))],
            out_specs=[fwd(WIDTH, 0), bwd(WIDTH, 0)],
            scratch_shapes=[pltpu.VMEM((2 * HEADS, HEAD_DIM, HEAD_DIM), _f32)]),
        out_shape=[out, out],
        compiler_params=_params(("arbitrary",)),
        name="gdn",
    )(first_f, first_b, qkv, qkv, qkv, h, qkv, qkv, qkv, h, consts)


def _rope(x, cos, sin_signed):
    lane = _iota2(x.shape, 1) % B_HEAD_DIM
    half = B_HEAD_DIM // 2
    rot = jnp.where(lane < half, pltpu.roll(x, LANES - half, 1), pltpu.roll(x, half, 1))
    return x * cos + rot * sin_signed


def _swa_body(first_ref, last_ref, pos_ref, q_ref, kp_ref, kc_ref, kn_ref, vp_ref, vc_ref, vn_ref,
              cp_ref, sp_ref, cc_ref, sc_ref, cn_ref, sn_ref, sink_ref, o_ref):
    i = pl.program_id(0)
    blk = B_BLOCK
    has_prev = first_ref[i] == 0
    has_next = last_ref[i] == 0
    k_all = jnp.concatenate([_rope(kp_ref[...], cp_ref[...], sp_ref[...]),
                             _rope(kc_ref[...], cc_ref[...], sc_ref[...]),
                             _rope(kn_ref[...], cn_ref[...], sn_ref[...])], axis=0)
    v_all = jnp.concatenate([vp_ref[...], vc_ref[...], vn_ref[...]], axis=0)
    nq = B_GROUP * blk
    kpos = _iota2((3 * blk, nq), 0)
    qpos = _iota2((3 * blk, nq), 1) % blk
    rel = kpos - qpos
    ok = (rel >= 0) & (rel <= 2 * WINDOW)
    ok = ok & (has_prev | (kpos >= blk)) & (has_next | (kpos < 2 * blk))
    cos, sin_signed = cc_ref[...], sc_ref[...]
    scale = B_HEAD_DIM ** -0.5
    qr = [_rope(q_ref[:, p * LANES:(p + 1) * LANES], cos, sin_signed) * scale for p in range(B_Q_HEADS // 2)]
    heads = [qr[hq // 2][:, (hq % 2) * B_HEAD_DIM:(hq % 2 + 1) * B_HEAD_DIM] for hq in range(B_Q_HEADS)]
    sink_row = sink_ref[0:1, :]
    kvs = range(B_KV_HEADS)
    q4 = [jnp.concatenate(heads[kv * B_GROUP:(kv + 1) * B_GROUP], axis=0) for kv in kvs]
    sink = [jnp.concatenate([jnp.broadcast_to(sink_row[:, kv * B_GROUP + g:kv * B_GROUP + g + 1], (1, blk))
                             for g in range(B_GROUP)], axis=1) for kv in kvs]
    kh = [k_all[:, kv * B_HEAD_DIM:(kv + 1) * B_HEAD_DIM] for kv in kvs]
    vh = [v_all[:, kv * B_HEAD_DIM:(kv + 1) * B_HEAD_DIM] for kv in kvs]
    s = [jnp.where(ok, _dot_nt(kh[kv], q4[kv]), -jnp.inf) for kv in kvs]
    m = [jnp.maximum(jnp.max(s[kv], axis=0, keepdims=True), sink[kv]) for kv in kvs]
    p = [jnp.exp(s[kv] - m[kv]) for kv in kvs]
    denom = [jnp.sum(p[kv], axis=0, keepdims=True) + jnp.exp(sink[kv] - m[kv]) for kv in kvs]
    o_t = [_dot_tn(vh[kv], p[kv]) / denom[kv] for kv in kvs]
    for kv in kvs:
        for g in range(0, B_GROUP, 2):
            hq = kv * B_GROUP + g
            pair = jnp.concatenate([o_t[kv][:, g * blk:(g + 1) * blk],
                                    o_t[kv][:, (g + 1) * blk:(g + 2) * blk]], axis=0)
            o_ref[:, hq * B_HEAD_DIM:(hq + 2) * B_HEAD_DIM] = jnp.transpose(pair)


def _swa(h, cos_t, sin_t, sink, first, last, pos):
    t = h.shape[0]
    n = t // B_BLOCK
    npos = cos_t.shape[0] // B_BLOCK
    prev = lambda i: jnp.maximum(i - 1, 0)
    nxt = lambda i: jnp.minimum(i + 1, n - 1)
    kcol, vcol = COL_B_K // LANES, COL_B_V // LANES
    hspec = lambda f, colblk: pl.BlockSpec((B_BLOCK, LANES), lambda i, a, b, p: (f(i), colblk))
    tspec = lambda delta: pl.BlockSpec(
        (B_BLOCK, LANES), lambda i, a, b, p: (jnp.clip(p[i] + delta, 0, npos - 1), 0))
    ident = lambda i: i
    qw = B_Q_HEADS * B_HEAD_DIM
    return pl.pallas_call(
        _swa_body,
        grid_spec=pltpu.PrefetchScalarGridSpec(
            num_scalar_prefetch=3,
            grid=(n,),
            in_specs=[pl.BlockSpec((B_BLOCK, qw), lambda i, a, b, p: (i, COL_B_Q // qw)),
                      hspec(prev, kcol), hspec(ident, kcol), hspec(nxt, kcol),
                      hspec(prev, vcol), hspec(ident, vcol), hspec(nxt, vcol),
                      tspec(-1), tspec(-1), tspec(0), tspec(0), tspec(1), tspec(1),
                      pl.BlockSpec((8, LANES), lambda i, a, b, p: (0, 0))],
            out_specs=pl.BlockSpec((B_BLOCK, qw), lambda i, a, b, p: (i, 0))),
        out_shape=jax.ShapeDtypeStruct((t, qw), _f32),
        compiler_params=_params(("parallel",)),
        name="window_attn",
    )(first, last, pos, h, h, h, h, h, h, h, cos_t, sin_t, cos_t, sin_t, cos_t, sin_t, sink)


def _residual_ln(x, y, g, b):
    z = DN_ALPHA * x + y
    mu = jnp.mean(z, axis=1, keepdims=True)
    zc = z - mu
    var = jnp.mean(zc * zc, axis=1, keepdims=True)
    return zc * lax.rsqrt(var + 1e-5) * g + b


def _gated_rms(o, gate, g):
    outs = []
    for h in range(HEADS):
        sl = slice(h * HEAD_DIM, (h + 1) * HEAD_DIM)
        x = o[:, sl]
        x = x * lax.rsqrt(jnp.mean(x * x, axis=1, keepdims=True) + 1e-6) * g
        outs.append(x * _silu(gate[:, sl]))
    return jnp.concatenate(outs, axis=1)


def _merge_body(x_ref, af_ref, ab_ref, ag_ref, ob_ref, cf_ref, cb_ref, cg_ref, ma_ref, mb_ref, mc_ref,
                wa_ref, wb_ref, wc_ref, wo_ref, nrm_ref, ln_ref, o_ref):
    oa = _gated_rms(af_ref[...] + ab_ref[...], ag_ref[...], nrm_ref[0:1, :])
    oc = _gated_rms(cf_ref[...] + cb_ref[...], cg_ref[...], nrm_ref[1:2, :])
    pa = jnp.dot(oa.astype(_bf16), wa_ref[...], preferred_element_type=_f32)
    pb = jnp.dot(ob_ref[...].astype(_bf16), wb_ref[...], preferred_element_type=_f32)
    pc = jnp.dot(oc.astype(_bf16), wc_ref[...], preferred_element_type=_f32)
    mix = (jax.nn.sigmoid(ma_ref[...]) * pa + jax.nn.sigmoid(mb_ref[...]) * pb
           + jax.nn.sigmoid(mc_ref[...]) * pc)
    y = jnp.dot(mix.astype(_bf16), wo_ref[...], preferred_element_type=_f32)
    o_ref[...] = _residual_ln(x_ref[...], y, ln_ref[0:1, :], ln_ref[1:2, :])


def _merge(x, h, oa_f, oa_b, ob, oc_f, oc_b, wa, wb, wc, wo, nrm, ln, tm):
    t = x.shape[0]
    row = lambda w, colblk=0: pl.BlockSpec((tm, w), lambda i: (i, colblk))
    full = lambda a: pl.BlockSpec(a.shape, lambda i: (0, 0))
    mcol = COL_MERGE // D_MODEL
    return pl.pallas_call(
        _merge_body,
        grid=(t // tm,),
        in_specs=[row(D_MODEL), row(WIDTH), row(WIDTH), row(WIDTH, COL_A_G // WIDTH), row(WIDTH),
                  row(WIDTH), row(WIDTH), row(WIDTH, COL_C_GATE // WIDTH),
                  row(D_MODEL, mcol), row(D_MODEL, mcol + 1), row(D_MODEL, mcol + 2),
                  full(wa), full(wb), full(wc), full(wo), full(nrm), full(ln)],
        out_specs=row(D_MODEL),
        out_shape=jax.ShapeDtypeStruct((t, D_MODEL), _f32),
        compiler_params=_params(("parallel",)),
        name="merge_mix",
    )(x, oa_f, oa_b, h, ob, oc_f, oc_b, h, h, h, h, wa, wb, wc, wo, nrm, ln)


def _xattn_body(seq_ref, x_ref, kv_ref, wq_ref, wo_ref, ln_ref, o_ref):
    x = x_ref[...]
    q = jnp.dot(x.astype(_bf16), wq_ref[...], preferred_element_type=_f32)
    hs = range(MEM_HEADS)
    cols = [slice(h * MEM_HEAD_DIM, (h + 1) * MEM_HEAD_DIM) for h in hs]
    s = [_dot_nt(q[:, cols[h]], kv_ref[0, :, cols[h]]) * (MEM_HEAD_DIM ** -0.5) for h in hs]
    p = [jnp.exp(s[h] - jnp.max(s[h], axis=1, keepdims=True)) for h in hs]
    p = [p[h] / jnp.sum(p[h], axis=1, keepdims=True) for h in hs]
    o = jnp.concatenate([_dot(p[h], kv_ref[0, :, D_MODEL + h * MEM_HEAD_DIM:D_MODEL + (h + 1) * MEM_HEAD_DIM])
                         for h in hs], axis=1)
    y = jnp.dot(o.astype(_bf16), wo_ref[...], preferred_element_type=_f32)
    o_ref[...] = _residual_ln(x, y, ln_ref[0:1, :], ln_ref[1:2, :])


def _xattn(x, kv, wq, wo, ln, seq_of_tile, tm):
    t = x.shape[0]
    full = lambda a: pl.BlockSpec(a.shape, lambda i, s: (0, 0))
    return pl.pallas_call(
        _xattn_body,
        grid_spec=pltpu.PrefetchScalarGridSpec(
            num_scalar_prefetch=1,
            grid=(t // tm,),
            in_specs=[pl.BlockSpec((tm, D_MODEL), lambda i, s: (i, 0)),
                      pl.BlockSpec((1,) + kv.shape[1:], lambda i, s: (s[i], 0, 0)),
                      full(wq), full(wo), full(ln)],
            out_specs=pl.BlockSpec((tm, D_MODEL), lambda i, s: (i, 0))),
        out_shape=jax.ShapeDtypeStruct((t, D_MODEL), _f32),
        compiler_params=_params(("parallel",)),
        name="mem_xattn",
    )(seq_of_tile, x, kv, wq, wo, ln)


def _ffn_body(x_ref, w1_ref, w2_ref, ln_ref, o_ref):
    x = x_ref[...]
    xb = x.astype(_bf16)
    gate = jnp.dot(xb, w1_ref[0], preferred_element_type=_f32)
    up = jnp.dot(xb, w1_ref[1], preferred_element_type=_f32)
    y = jnp.dot((_silu(gate) * up).astype(_bf16), w2_ref[...], preferred_element_type=_f32)
    o_ref[...] = _residual_ln(x, y, ln_ref[0:1, :], ln_ref[1:2, :])


def _ffn(x, w1, w2, ln, tm):
    t = x.shape[0]
    return pl.pallas_call(
        _ffn_body,
        grid=(t // tm,),
        in_specs=[pl.BlockSpec((tm, D_MODEL), lambda i: (i, 0)),
                  _resident(w1.shape), _resident(w2.shape), _resident(ln.shape)],
        out_specs=pl.BlockSpec((tm, D_MODEL), lambda i: (i, 0)),
        out_shape=jax.ShapeDtypeStruct((t, D_MODEL), _f32),
        compiler_params=_params(("parallel",)),
        name="swiglu_ffn",
    )(x, w1, w2, ln)


def _boundary_tables(seq_lens, tile):
    first, last, pos, seq = [], [], [], []
    for sid, length in enumerate(seq_lens):
        n = length // tile
        for b in range(n):
            first.append(int(b == 0))
            last.append(int(b == n - 1))
            pos.append(b)
            seq.append(sid)
    as_i32 = lambda v: jnp.asarray(np.asarray(v, np.int32))
    return as_i32(first), as_i32(last), as_i32(pos), as_i32(seq)


def _permute_in_cols(w):
    pad = jnp.zeros(w.shape[:-1] + (IN_COLS_PAD - 8464,), w.dtype)
    return jnp.concatenate([w[..., 0:3072], w[..., 3328:4864], w[..., 4864:5376], w[..., 5392:8464],
                            w[..., 3072:3200], w[..., 3200:3328], w[..., 5376:5392], pad], axis=-1)


def _rows8(*rows):
    n = rows[0].shape[-1]
    out = jnp.zeros((8, n), _f32)
    for r, v in enumerate(rows):
        out = out.at[r].set(v.astype(_f32))
    return out


def _lane_pad(v, offset):
    return jnp.zeros((LANES,), _f32).at[offset:offset + v.shape[0]].set(v.astype(_f32))


def kernel(x_prompt, x_sample, mem_prompt, mem_sample, w_in, hgrn_lb_logits, hgrn_norm_g, attn_sink,
           gdn_conv_w, gdn_a_log, gdn_dt_bias, gdn_norm_g, w_branch_a, w_branch_b, w_branch_c, w_mix_out,
           w_mem_q, w_mem_kv, w_mem_o, w_ffn_in, w_ffn_out, ln_g, ln_b):
    depth = w_in.shape[0]
    d = x_prompt.shape[-1]
    seq_lens = (x_prompt.shape[1],) * x_prompt.shape[0] + (x_sample.shape[1],) * x_sample.shape[0]
    n_prompt = x_prompt.shape[0] * x_prompt.shape[1]
    x = jnp.concatenate([x_prompt.reshape(-1, d), x_sample.reshape(-1, d)], axis=0)
    mem = jnp.concatenate([mem_prompt, mem_sample], axis=0)
    n_seq, n_mem, _ = mem.shape
    t = x.shape[0]

    tm = math.gcd(512, *seq_lens)
    tm_proj = math.gcd(512, t)
    rows_a = math.gcd(HGRN_CHUNKS_PER_STEP * CHUNK, *seq_lens)
    first_a, last_a, _, _ = _boundary_tables(seq_lens, rows_a)
    last_a_rev = last_a[::-1]
    rows_g = math.gcd(GDN_CHUNKS_PER_STEP * CHUNK, *seq_lens)
    first_g, last_g, _, _ = _boundary_tables(seq_lens, rows_g)
    last_g_rev = last_g[::-1]
    first_b, last_b, pos_b, _ = _boundary_tables(seq_lens, B_BLOCK)
    first_t, last_t, _, seq_t = _boundary_tables(seq_lens, tm)

    s_max = max(seq_lens)
    inv = ROPE_THETA ** (-jnp.arange(0, B_HEAD_DIM, 2, dtype=_f32) / B_HEAD_DIM)
    ang = jnp.arange(s_max, dtype=_f32)[:, None] * inv[None, :]
    cos_t = jnp.tile(jnp.cos(ang), (1, 4))
    sin_t = jnp.tile(jnp.concatenate([-jnp.sin(ang), jnp.sin(ang)], axis=1), (1, 2))

    cum = jnp.cumsum(jax.nn.softmax(hgrn_lb_logits.astype(_f32), axis=1), axis=1)
    lb = cum - cum[:, :1]

    w_in_p = _permute_in_cols(w_in).astype(_bf16)
    bf = lambda w: w.astype(_bf16)
    w_a, w_b, w_c, w_mix = bf(w_branch_a), bf(w_branch_b), bf(w_branch_c), bf(w_mix_out)
    w_q, w_kv, w_o = bf(w_mem_q), bf(w_mem_kv), bf(w_mem_o)
    w_f1, w_f2 = bf(w_ffn_in), bf(w_ffn_out)
    mem2 = mem.reshape(n_seq * n_mem, d)

    for l in range(depth):
        h = _matmul(x, _col_tiles(w_in_p[l], IN_COLS_PAD // 2), tm_proj)
        lb_rows = []
        for dirn in range(2):
            lbd = lb[dirn, l]
            lb_rows += [jnp.log(lbd), jnp.log1p(-lbd), 1.0 - lbd]
        oa_f, oa_b = _hgrn(h, _rows8(*lb_rows), first_a, last_a_rev, rows_a)
        o_b = _swa(h, cos_t, sin_t, _rows8(_lane_pad(attn_sink[l], 0)), first_b, last_b, pos_b)
        qkv = _gdn_prep(h, _rows8(*[gdn_conv_w[l, j] for j in range(CONV_K)]), first_t, last_t, tm)
        gconst = _rows8(_lane_pad(-jnp.exp(gdn_a_log[l].astype(_f32)).reshape(-1), 2 * HEADS),
                        _lane_pad(gdn_dt_bias[l].reshape(-1), 2 * HEADS))
        oc_f, oc_b = _gdn(qkv, h, gconst, first_g, last_g_rev, rows_g)
        nrm = _rows8(hgrn_norm_g[l], gdn_norm_g[l])
        x = _merge(x, h, oa_f, oa_b, o_b, oc_f, oc_b, w_a[l], w_b[l], w_c[l], w_mix[l], nrm,
                   _rows8(ln_g[l, 0], ln_b[l, 0]), tm)
        kv = _matmul(mem2, _col_tiles(w_kv[l], 512), n_mem).reshape(n_seq, n_mem, 2 * d)
        x = _xattn(x, kv, w_q[l], w_o[l], _rows8(ln_g[l, 1], ln_b[l, 1]), seq_t, tm)
        x = _ffn(x, _col_tiles(w_f1[l], FFN_HIDDEN), w_f2[l], _rows8(ln_g[l, 2], ln_b[l, 2]), tm)

    y_prompt = x[:n_prompt].reshape(x_prompt.shape)
    y_sample = x[n_prompt:].reshape(x_sample.shape)
    return (y_prompt, y_sample)
```

```python
import math

import numpy as np
import jax
import jax.numpy as jnp
from jax import lax
from jax.experimental import pallas as pl
from jax.experimental.pallas import tpu as pltpu

D_MODEL = 1024
DEPTH = 4
HEADS = 4
HEAD_DIM = 128
WIDTH = HEADS * HEAD_DIM
B_Q_HEADS = 8
B_KV_HEADS = 2
B_GROUP = B_Q_HEADS // B_KV_HEADS
B_HEAD_DIM = 64
WINDOW = 128
B_BLOCK = 128
ROPE_THETA = 10000.0
CONV_K = 5
MEM_HEADS = 4
MEM_HEAD_DIM = D_MODEL // MEM_HEADS
FFN_HIDDEN = 2816
DN_ALPHA = (2 * DEPTH) ** 0.25

CHUNK = 64
GDN_CHUNKS_PER_STEP = 4
HGRN_CHUNKS_PER_STEP = 4
SWA_BLOCKS_PER_STEP = 4
LANES = 128
LOG2_E = math.log2(math.e)
VMEM_LIMIT = 56 * 1024 * 1024

COL_A_Q, COL_A_FF, COL_A_FB, COL_A_I, COL_A_G = 0, 512, 1024, 1536, 2048
COL_B_Q = 2560
COL_C_Q, COL_C_K, COL_C_V = 3072, 3584, 4096
COL_C_GATE = 4608
COL_MERGE = 5120
COL_B_K, COL_B_V = 8192, 8320
COL_GB = 8448
IN_COLS_PAD = 8704

_f32 = jnp.float32
_bf16 = jnp.bfloat16
BRANCH_DTYPE = _bf16


def _dot(a, b):
    return jnp.dot(a.astype(_bf16), b.astype(_bf16), preferred_element_type=_f32)


def _dot_nt(a, b):
    return lax.dot_general(a.astype(_bf16), b.astype(_bf16), (((1,), (1,)), ((), ())),
                           preferred_element_type=_f32)


def _dot_tn(a, b):
    return lax.dot_general(a.astype(_bf16), b.astype(_bf16), (((0,), (0,)), ((), ())),
                           preferred_element_type=_f32)


def _split3(x):
    hi = x.astype(_bf16)
    r1 = x - hi.astype(_f32)
    mid = r1.astype(_bf16)
    lo = (r1 - mid.astype(_f32)).astype(_bf16)
    return hi, mid, lo


def _exact_dot(sel, parts):
    dot = lambda p: jnp.dot(sel, p, preferred_element_type=_f32)
    return dot(parts[0]) + dot(parts[1]) + dot(parts[2])


def _silu(x):
    return x * jax.nn.sigmoid(x)


def _iota2(shape, dim):
    return lax.broadcasted_iota(jnp.int32, shape, dim)


def _params(sem):
    return pltpu.CompilerParams(dimension_semantics=sem, vmem_limit_bytes=VMEM_LIMIT)


def _resident(shape):
    return pl.BlockSpec(shape, lambda *_: (0,) * len(shape), pipeline_mode=pl.Buffered(1))


def _col_tiles(w, tn):
    k, n = w.shape
    return w.reshape(k, n // tn, tn).transpose(1, 0, 2)


def _matmul_body(x_ref, w_ref, o_ref, xb_ref):
    j = pl.program_id(1)

    @pl.when(j == 0)
    def _():
        xb_ref[...] = x_ref[...].astype(_bf16)

    o_ref[...] = jnp.dot(xb_ref[...], w_ref[j], preferred_element_type=_f32)


def _matmul(x, w_tiles, tm):
    t, k = x.shape
    nt, _, tn = w_tiles.shape
    n = nt * tn
    return pl.pallas_call(
        _matmul_body,
        grid=(t // tm, nt),
        in_specs=[pl.BlockSpec((tm, k), lambda i, j: (i, 0)), _resident(w_tiles.shape)],
        out_specs=pl.BlockSpec((tm, tn), lambda i, j: (i, j)),
        out_shape=jax.ShapeDtypeStruct((t, n), _f32),
        scratch_shapes=[pltpu.VMEM((tm, k), _bf16)],
        compiler_params=_params(("parallel", "arbitrary")),
        name="dense_proj",
    )(x, w_tiles)


def _chunk_masks(c, reverse):
    row = _iota2((c, c), 0)
    col = _iota2((c, c), 1)
    if reverse:
        return col >= row, col > row
    return col <= row, col < row


def _level_mask(c, blk, reverse):
    row = _iota2((c, c), 0)
    col = _iota2((c, c), 1)
    half = blk // 2
    same = (row // blk) == (col // blk)
    r_hi = (row % blk) >= half
    c_hi = (col % blk) >= half
    if reverse:
        return same & jnp.logical_not(r_hi) & c_hi
    return same & r_hi & jnp.logical_not(c_hi)


def _block_ref_rows(g, blk, reverse):
    c, n = g.shape
    idx = blk // 2 if reverse else blk // 2 - 1
    rows = max(blk, 8)
    g3 = g.reshape(c // rows, rows, n)
    pick = lambda r: jnp.broadcast_to(g3[:, r:r + 1, :], g3.shape)
    out = pick(idx)
    if blk < rows:
        sub = lax.broadcasted_iota(jnp.int32, g3.shape, 1)
        for b in range(1, rows // blk):
            out = jnp.where(sub >= b * blk, pick(b * blk + idx), out)
    return out.reshape(c, n)


def _levels(c):
    out, blk = [], c
    while blk >= 2:
        out.append(blk)
        blk //= 2
    return out


def _hgrn_body(first_f_ref, first_b_ref, zq_f, zf_f, zi_f, zq_b, zf_b, zi_b, lb_ref, o_f, o_b, st_ref):
    c = CHUNK
    step = pl.program_id(0)

    @pl.when(first_f_ref[step] == 1)
    def _():
        st_ref[0:HEADS] = jnp.zeros((HEADS, HEAD_DIM, HEAD_DIM), _f32)

    @pl.when(first_b_ref[step] == 1)
    def _():
        st_ref[HEADS:2 * HEADS] = jnp.zeros((HEADS, HEAD_DIM, HEAD_DIM), _f32)

    nsub = zq_f.shape[0] // c
    row = _iota2((c, c), 0)
    col = _iota2((c, c), 1)
    eye = row == col
    blks = _levels(c)
    units = []
    for d, (zq_ref, zf_ref, zi_ref) in enumerate(((zq_f, zf_f, zi_f), (zq_b, zf_b, zi_b))):
        reverse = d == 1
        incl, _ = _chunk_masks(c, reverse)
        tri = incl.astype(_bf16)
        masks = [_level_mask(c, blk, reverse) for blk in blks]
        zf = zf_ref[...]
        log_lb = lb_ref[3 * d:3 * d + 1, :]
        log_1m_lb = lb_ref[3 * d + 1:3 * d + 2, :]
        one_m_lb = lb_ref[3 * d + 2:3 * d + 3, :]
        ez = jnp.exp(-jnp.abs(zf))
        one_p = 1.0 + ez
        log_sig = jnp.minimum(zf, 0.0) - jnp.log(one_p)
        b = log_1m_lb + log_sig
        log_f = jnp.maximum(log_lb, b) + jnp.log(1.0 + jnp.exp(-jnp.abs(log_lb - b)))
        key_all = one_m_lb * (jnp.where(zf >= 0.0, ez, 1.0) / one_p)
        rpos = _iota2((c, HEAD_DIM), 0)
        later = [((rpos % blk) < blk // 2) if reverse else ((rpos % blk) >= blk // 2) for blk in blks]
        sgn = [jnp.where(m, LOG2_E, -LOG2_E) for m in later]
        q_all = _silu(zq_ref[...]) * (HEAD_DIM ** -0.5)
        v_all = zi_ref[...]
        tot_row = 0 if reverse else c - 1
        for s in (range(nsub - 1, -1, -1) if reverse else range(nsub)):
            rs = slice(s * c, (s + 1) * c)
            gc_all = _exact_dot(tri, _split3(log_f[rs]))
            ref_all = [_block_ref_rows(gc_all, blk, reverse) for blk in blks]
            for h in range(HEADS):
                sl = slice(h * HEAD_DIM, (h + 1) * HEAD_DIM)
                units.append(dict(q=q_all[rs, sl], k=key_all[rs, sl], v=v_all[rs, sl], gc=gc_all[:, sl],
                                  refs=[r[:, sl] for r in ref_all], masks=masks, sgn=sgn, rows=rs,
                                  g_tot=gc_all[tot_row:tot_row + 1, sl]))

    scores = [jnp.where(eye, _dot_nt(u["q"], u["k"]), 0.0) for u in units]
    for l in range(len(blks)):
        e = [jnp.exp2((u["gc"] - u["refs"][l]) * u["sgn"][l]) for u in units]
        scores = [jnp.where(u["masks"][l], _dot_nt(u["q"] * e[i], u["k"] * e[i]), scores[i])
                  for i, u in enumerate(units)]
    intra = [_dot(scores[i], u["v"]) for i, u in enumerate(units)]
    qd = [u["q"] * jnp.exp(u["gc"]) for u in units]
    kd = [u["k"] * jnp.exp(u["g_tot"] - u["gc"]) for u in units]

    st = [st_ref[i] for i in range(2 * HEADS)]
    for slot in range(nsub):
        for d in range(2):
            for h in range(HEADS):
                i, j = (d * nsub + slot) * HEADS + h, d * HEADS + h
                out = intra[i] + _dot_nt(qd[i], st[j])
                st[j] = st[j] * jnp.exp(units[i]["g_tot"]) + _dot_tn(units[i]["v"], kd[i])
                (o_f, o_b)[d][units[i]["rows"], h * HEAD_DIM:(h + 1) * HEAD_DIM] = out.astype(o_f.dtype)
    for i in range(2 * HEADS):
        st_ref[i] = st[i]


def _hgrn(h, lb_consts, first_f, first_b, rows):
    t = h.shape[0]
    n = t // rows
    fwd = lambda col: pl.BlockSpec((rows, WIDTH), lambda i, a, b: (i, col // WIDTH))
    bwd = lambda col: pl.BlockSpec((rows, WIDTH), lambda i, a, b: (n - 1 - i, col // WIDTH))
    out = jax.ShapeDtypeStruct((t, WIDTH), BRANCH_DTYPE)
    return pl.pallas_call(
        _hgrn_body,
        grid_spec=pltpu.PrefetchScalarGridSpec(
            num_scalar_prefetch=2,
            grid=(n,),
            in_specs=[fwd(COL_A_Q), fwd(COL_A_FF), fwd(COL_A_I), bwd(COL_A_Q), bwd(COL_A_FB), bwd(COL_A_I),
                      pl.BlockSpec((8, WIDTH), lambda i, a, b: (0, 0))],
            out_specs=[fwd(0), bwd(0)],
            scratch_shapes=[pltpu.VMEM((2 * HEADS, HEAD_DIM, HEAD_DIM), _f32)]),
        out_shape=[out, out],
        compiler_params=_params(("arbitrary",)),
        name="hgrn",
    )(first_f, first_b, h, h, h, h, h, h, lb_consts)


def _gdn_prep_body(first_ref, last_ref, prev_ref, cur_ref, next_ref, w_ref, o_ref, ext_ref):
    i = pl.program_id(0)
    tm = cur_ref.shape[0]
    halo = CONV_K // 2
    ext_ref[0:8, :] = jnp.where(first_ref[i] == 1, 0.0, prev_ref[...])
    ext_ref[8:8 + tm, :] = cur_ref[...]
    ext_ref[8 + tm:16 + tm, :] = jnp.where(last_ref[i] == 1, 0.0, next_ref[...])
    acc = cur_ref[...] * w_ref[halo:halo + 1, :]
    for j in range(CONV_K):
        if j != halo:
            off = 8 - halo + j
            acc = acc + ext_ref[off:off + tm, :] * w_ref[j:j + 1, :]
    y = _silu(acc)
    for part in range(3):
        for h in range(HEADS):
            lo = part * WIDTH + h * HEAD_DIM
            x = y[:, lo:lo + HEAD_DIM]
            if part < 2:
                x = x * lax.rsqrt(jnp.sum(x * x, axis=1, keepdims=True) + 1e-6)
            if part == 0:
                x = x * (HEAD_DIM ** -0.5)
            o_ref[:, lo:lo + HEAD_DIM] = x


def _gdn_prep(h, conv_w, first, last, tm):
    t = h.shape[0]
    n = t // tm
    r8 = tm // 8
    cb = COL_C_Q // (3 * WIDTH)
    return pl.pallas_call(
        _gdn_prep_body,
        grid_spec=pltpu.PrefetchScalarGridSpec(
            num_scalar_prefetch=2,
            grid=(n,),
            in_specs=[pl.BlockSpec((8, 3 * WIDTH), lambda i, f, l: (jnp.maximum(i * r8 - 1, 0), cb)),
                      pl.BlockSpec((tm, 3 * WIDTH), lambda i, f, l: (i, cb)),
                      pl.BlockSpec((8, 3 * WIDTH), lambda i, f, l: (jnp.minimum((i + 1) * r8, n * r8 - 1), cb)),
                      pl.BlockSpec((8, 3 * WIDTH), lambda i, f, l: (0, 0))],
            out_specs=pl.BlockSpec((tm, 3 * WIDTH), lambda i, f, l: (i, 0)),
            scratch_shapes=[pltpu.VMEM((tm + 16, 3 * WIDTH), _f32)]),
        out_shape=jax.ShapeDtypeStruct((t, 3 * WIDTH), _f32),
        compiler_params=_params(("parallel",)),
        name="gdn_prep",
    )(first, last, h, h, h, conv_w)


def _gdn_body(first_f_ref, first_b_ref, q_f, k_f, v_f, gb_f, q_b, k_b, v_b, gb_b, cst_ref, o_f, o_b, st_ref):
    c = CHUNK
    step = pl.program_id(0)

    @pl.when(first_f_ref[step] == 1)
    def _():
        st_ref[0:HEADS] = jnp.zeros((HEADS, HEAD_DIM, HEAD_DIM), _f32)

    @pl.when(first_b_ref[step] == 1)
    def _():
        st_ref[HEADS:2 * HEADS] = jnp.zeros((HEADS, HEAD_DIM, HEAD_DIM), _f32)

    nsub = q_f.shape[0] // c
    row = _iota2((c, c), 0)
    col = _iota2((c, c), 1)
    eye = (row == col).astype(_f32)
    neg_a = cst_ref[0:1, :]
    dt_bias = cst_ref[1:2, :]
    sizes = _levels(c)[::-1]
    units = []
    for d, (q_ref, k_ref, v_ref, gb_ref) in enumerate(((q_f, k_f, v_f, gb_f), (q_b, k_b, v_b, gb_b))):
        reverse = d == 1
        incl, strict = _chunk_masks(c, reverse)
        tri = incl.astype(_bf16)
        masks = [_level_mask(c, blk, reverse) for blk in sizes]
        gb = gb_ref[...]
        beta_t = jax.nn.sigmoid(gb)
        zz = gb + dt_bias
        g_in = neg_a * (jnp.maximum(zz, 0.0) + jnp.log1p(jnp.exp(-jnp.abs(zz))))
        tot_row = 0 if reverse else c - 1
        for s in (range(nsub - 1, -1, -1) if reverse else range(nsub)):
            rs = slice(s * c, (s + 1) * c)
            gc_t = _exact_dot(tri, _split3(g_in[rs]))
            gc_tt = jnp.transpose(gc_t)
            for h in range(HEADS):
                sl = slice(h * HEAD_DIM, (h + 1) * HEAD_DIM)
                j = 2 * HEADS + d * HEADS + h
                g_col = gc_t[:, j:j + 1]
                decay = jnp.where(incl, jnp.exp(jnp.minimum(g_col - gc_tt[j:j + 1, :], 0.0)), 0.0)
                units.append(dict(q=q_ref[rs, sl], k=k_ref[rs, sl], v=v_ref[rs, sl], rows=rs,
                                  g_col=g_col, decay=decay, strict=strict, masks=masks,
                                  beta=beta_t[rs, d * HEADS + h:d * HEADS + h + 1],
                                  g_tot=gc_t[tot_row:tot_row + 1, j:j + 1]))

    k_b = [u["k"].astype(_bf16) for u in units]
    a = [jnp.where(u["strict"], _dot_nt(k_b[i], k_b[i]) * u["decay"] * u["beta"], 0.0) for i, u in enumerate(units)]
    qk = [_dot_nt(u["q"], k_b[i]) * u["decay"] for i, u in enumerate(units)]
    inv = [eye - jnp.where(u["masks"][0], a[i], 0.0) for i, u in enumerate(units)]
    for l in range(1, len(sizes)):
        inv_b = [m.astype(_bf16) for m in inv]
        t1 = [_dot(jnp.where(u["masks"][l], a[i], 0.0), inv_b[i]) for i, u in enumerate(units)]
        inv = [inv[i] - _dot(inv_b[i], t1[i]) for i in range(len(units))]
    inv_b = [m.astype(_bf16) for m in inv]
    e_g = [jnp.exp(u["g_col"]) for u in units]
    sol = [_dot(inv_b[i], jnp.concatenate([u["v"] * u["beta"], u["k"] * (u["beta"] * e_g[i])], axis=1))
           for i, u in enumerate(units)]
    uu = [m[:, :HEAD_DIM] for m in sol]
    ww = [m[:, HEAD_DIM:] for m in sol]
    qd = [u["q"] * e_g[i] for i, u in enumerate(units)]
    kd = [u["k"] * jnp.exp(u["g_tot"] - u["g_col"]) for u in units]

    st = [st_ref[i] for i in range(2 * HEADS)]
    for slot in range(nsub):
        for d in range(2):
            ids = [(d * nsub + slot) * HEADS + h for h in range(HEADS)]
            st_b = [st[d * HEADS + h].astype(_bf16) for h in range(HEADS)]
            v_new = [uu[i] - _dot_nt(ww[i], st_b[h]) for h, i in enumerate(ids)]
            out = [_dot_nt(qd[i], st_b[h]) + _dot(qk[i], v_new[h]) for h, i in enumerate(ids)]
            for h, i in enumerate(ids):
                st[d * HEADS + h] = (st[d * HEADS + h] * jnp.exp(units[i]["g_tot"])
                                     + _dot_tn(v_new[h], kd[i]))
                (o_f, o_b)[d][units[i]["rows"], h * HEAD_DIM:(h + 1) * HEAD_DIM] = out[h].astype(o_f.dtype)
    for i in range(2 * HEADS):
        st_ref[i] = st[i]


def _gdn(qkv, h, consts, first_f, first_b, rows):
    t = qkv.shape[0]
    n = t // rows
    fwd = lambda w, colblk: pl.BlockSpec((rows, w), lambda i, a, b: (i, colblk))
    bwd = lambda w, colblk: pl.BlockSpec((rows, w), lambda i, a, b: (n - 1 - i, colblk))
    gcol = COL_GB // LANES
    out = jax.ShapeDtypeStruct((t, WIDTH), BRANCH_DTYPE)
    return pl.pallas_call(
        _gdn_body,
        grid_spec=pltpu.PrefetchScalarGridSpec(
            num_scalar_prefetch=2,
            grid=(n,),
            in_specs=[fwd(WIDTH, 0), fwd(WIDTH, 1), fwd(WIDTH, 2), fwd(LANES, gcol),
                      bwd(WIDTH, 0), bwd(WIDTH, 1), bwd(WIDTH, 2), bwd(LANES, gcol),
                      pl.BlockSpec((8, LANES), lambda i, a, b: (0, 0))],
            out_specs=[fwd(WIDTH, 0), bwd(WIDTH, 0)],
            scratch_shapes=[pltpu.VMEM((2 * HEADS, HEAD_DIM, HEAD_DIM), _f32)]),
        out_shape=[out, out],
        compiler_params=_params(("arbitrary",)),
        name="gdn",
    )(first_f, first_b, qkv, qkv, qkv, h, qkv, qkv, qkv, h, consts)


def _rope(x, cos, sin_signed):
    src = _iota2((LANES, LANES), 0)
    dst = _iota2((LANES, LANES), 1)
    half = B_HEAD_DIM // 2
    partner = jnp.where(dst % B_HEAD_DIM < half, dst + half, dst - half)
    rot = jnp.dot(x.astype(_bf16), (src == partner).astype(_bf16), preferred_element_type=_f32)
    return x * cos + rot * sin_signed


def _swa_body(first_ref, last_ref, pos_ref, q_ref, kp_ref, kc_ref, kn_ref, vp_ref, vc_ref, vn_ref,
              cp_ref, sp_ref, cc_ref, sc_ref, cn_ref, sn_ref, sink_ref, o_ref):
    i = pl.program_id(0)
    blk = B_BLOCK
    nb = q_ref.shape[0] // blk
    has_prev = first_ref[i] == 0
    has_next = last_ref[i] == 0
    k_all = jnp.concatenate([_rope(kp_ref[...], cp_ref[...], sp_ref[...]),
                             _rope(kc_ref[...], cc_ref[...], sc_ref[...]),
                             _rope(kn_ref[...], cn_ref[...], sn_ref[...])], axis=0)
    v_all = jnp.concatenate([vp_ref[...], vc_ref[...], vn_ref[...]], axis=0)
    nq = B_GROUP * blk
    kpos = _iota2((3 * blk, nq), 0)
    qpos = _iota2((3 * blk, nq), 1) % blk
    rel = kpos - qpos
    in_window = (rel >= 0) & (rel <= 2 * WINDOW)
    cos, sin_signed = cc_ref[...], sc_ref[...]
    scale = B_HEAD_DIM ** -0.5
    qr = [_rope(q_ref[:, p * LANES:(p + 1) * LANES], cos, sin_signed) * scale for p in range(B_Q_HEADS // 2)]
    sink_row = sink_ref[0:1, :]
    chains = [(b, kv) for b in range(nb) for kv in range(B_KV_HEADS)]
    q4, sink, kh, vh, ok = [], [], [], [], []
    for b, kv in chains:
        rows = slice(b * blk, (b + 1) * blk)
        heads = [qr[hq // 2][rows, (hq % 2) * B_HEAD_DIM:(hq % 2 + 1) * B_HEAD_DIM]
                 for hq in range(kv * B_GROUP, (kv + 1) * B_GROUP)]
        q4.append(jnp.concatenate(heads, axis=0))
        sink.append(jnp.concatenate([jnp.broadcast_to(sink_row[:, kv * B_GROUP + g:kv * B_GROUP + g + 1], (1, blk))
                                     for g in range(B_GROUP)], axis=1))
        kh.append(k_all[b * blk:(b + 3) * blk, kv * B_HEAD_DIM:(kv + 1) * B_HEAD_DIM])
        vh.append(v_all[b * blk:(b + 3) * blk, kv * B_HEAD_DIM:(kv + 1) * B_HEAD_DIM])
        m_ok = in_window
        if b == 0:
            m_ok = m_ok & (has_prev | (kpos >= blk))
        if b == nb - 1:
            m_ok = m_ok & (has_next | (kpos < 2 * blk))
        ok.append(m_ok)
    cs = range(len(chains))
    s = [jnp.where(ok[c], _dot_nt(kh[c], q4[c]), -jnp.inf) for c in cs]
    m = [jnp.maximum(jnp.max(s[c], axis=0, keepdims=True), sink[c]) for c in cs]
    p = [jnp.exp(s[c] - m[c]) for c in cs]
    denom = [jnp.sum(p[c], axis=0, keepdims=True) + jnp.exp(sink[c] - m[c]) for c in cs]
    o_t = [_dot_tn(vh[c], p[c]) / denom[c] for c in cs]
    for c, (b, kv) in enumerate(chains):
        for g in range(0, B_GROUP, 2):
            hq = kv * B_GROUP + g
            pair = jnp.concatenate([o_t[c][:, g * blk:(g + 1) * blk],
                                    o_t[c][:, (g + 1) * blk:(g + 2) * blk]], axis=0)
            o_ref[b * blk:(b + 1) * blk, hq * B_HEAD_DIM:(hq + 2) * B_HEAD_DIM] = (
                jnp.transpose(pair).astype(o_ref.dtype))


def _swa(h, cos_t, sin_t, sink, first, last, pos, nb):
    t = h.shape[0]
    rows = nb * B_BLOCK
    n = t // rows
    n128 = t // B_BLOCK
    npos = cos_t.shape[0] // B_BLOCK
    kcol, vcol = COL_B_K // LANES, COL_B_V // LANES
    before = lambda colblk: pl.BlockSpec((B_BLOCK, LANES), lambda i, a, b, p: (jnp.maximum(i * nb - 1, 0), colblk))
    own = lambda colblk: pl.BlockSpec((rows, LANES), lambda i, a, b, p: (i, colblk))
    after = lambda colblk: pl.BlockSpec((B_BLOCK, LANES),
                                        lambda i, a, b, p: (jnp.minimum((i + 1) * nb, n128 - 1), colblk))
    t_before = pl.BlockSpec((B_BLOCK, LANES), lambda i, a, b, p: (jnp.maximum(p[i] * nb - 1, 0), 0))
    t_own = pl.BlockSpec((rows, LANES), lambda i, a, b, p: (p[i], 0))
    t_after = pl.BlockSpec((B_BLOCK, LANES), lambda i, a, b, p: (jnp.minimum((p[i] + 1) * nb, npos - 1), 0))
    qw = B_Q_HEADS * B_HEAD_DIM
    return pl.pallas_call(
        _swa_body,
        grid_spec=pltpu.PrefetchScalarGridSpec(
            num_scalar_prefetch=3,
            grid=(n,),
            in_specs=[pl.BlockSpec((rows, qw), lambda i, a, b, p: (i, COL_B_Q // qw)),
                      before(kcol), own(kcol), after(kcol), before(vcol), own(vcol), after(vcol),
                      t_before, t_before, t_own, t_own, t_after, t_after,
                      pl.BlockSpec((8, LANES), lambda i, a, b, p: (0, 0))],
            out_specs=pl.BlockSpec((rows, qw), lambda i, a, b, p: (i, 0))),
        out_shape=jax.ShapeDtypeStruct((t, qw), BRANCH_DTYPE),
        compiler_params=_params(("parallel",)),
        name="window_attn",
    )(first, last, pos, h, h, h, h, h, h, h, cos_t, sin_t, cos_t, sin_t, cos_t, sin_t, sink)


def _residual_ln(x, y, g, b):
    z = DN_ALPHA * x + y
    mu = jnp.mean(z, axis=1, keepdims=True)
    zc = z - mu
    var = jnp.mean(zc * zc, axis=1, keepdims=True)
    return zc * lax.rsqrt(var + 1e-5) * g + b


def _gated_rms(o, gate, g):
    outs = []
    for h in range(HEADS):
        sl = slice(h * HEAD_DIM, (h + 1) * HEAD_DIM)
        x = o[:, sl]
        x = x * lax.rsqrt(jnp.mean(x * x, axis=1, keepdims=True) + 1e-6) * g
        outs.append(x * _silu(gate[:, sl]))
    return jnp.concatenate(outs, axis=1)


def _merge_body(x_ref, af_ref, ab_ref, ag_ref, ob_ref, cf_ref, cb_ref, cg_ref, ma_ref, mb_ref, mc_ref,
                wa_ref, wb_ref, wc_ref, wo_ref, nrm_ref, ln_ref, o_ref):
    up = lambda ref: ref[...].astype(_f32)
    oa = _gated_rms(up(af_ref) + up(ab_ref), ag_ref[...], nrm_ref[0:1, :])
    oc = _gated_rms(up(cf_ref) + up(cb_ref), cg_ref[...], nrm_ref[1:2, :])
    pa = jnp.dot(oa.astype(_bf16), wa_ref[...], preferred_element_type=_f32)
    pb = jnp.dot(ob_ref[...].astype(_bf16), wb_ref[...], preferred_element_type=_f32)
    pc = jnp.dot(oc.astype(_bf16), wc_ref[...], preferred_element_type=_f32)
    mix = (jax.nn.sigmoid(ma_ref[...]) * pa + jax.nn.sigmoid(mb_ref[...]) * pb
           + jax.nn.sigmoid(mc_ref[...]) * pc)
    y = jnp.dot(mix.astype(_bf16), wo_ref[...], preferred_element_type=_f32)
    o_ref[...] = _residual_ln(x_ref[...], y, ln_ref[0:1, :], ln_ref[1:2, :])


def _merge(x, h, oa_f, oa_b, ob, oc_f, oc_b, wa, wb, wc, wo, nrm, ln, tm):
    t = x.shape[0]
    row = lambda w, colblk=0: pl.BlockSpec((tm, w), lambda i: (i, colblk))
    full = lambda a: pl.BlockSpec(a.shape, lambda i: (0, 0))
    mcol = COL_MERGE // D_MODEL
    return pl.pallas_call(
        _merge_body,
        grid=(t // tm,),
        in_specs=[row(D_MODEL), row(WIDTH), row(WIDTH), row(WIDTH, COL_A_G // WIDTH), row(WIDTH),
                  row(WIDTH), row(WIDTH), row(WIDTH, COL_C_GATE // WIDTH),
                  row(D_MODEL, mcol), row(D_MODEL, mcol + 1), row(D_MODEL, mcol + 2),
                  full(wa), full(wb), full(wc), full(wo), full(nrm), full(ln)],
        out_specs=row(D_MODEL),
        out_shape=jax.ShapeDtypeStruct((t, D_MODEL), _f32),
        compiler_params=_params(("parallel",)),
        name="merge_mix",
    )(x, oa_f, oa_b, h, ob, oc_f, oc_b, h, h, h, h, wa, wb, wc, wo, nrm, ln)


def _xattn_body(seq_ref, x_ref, kv_ref, wq_ref, wo_ref, ln_ref, o_ref):
    x = x_ref[...]
    q = jnp.dot(x.astype(_bf16), wq_ref[...], preferred_element_type=_f32)
    hs = range(MEM_HEADS)
    cols = [slice(h * MEM_HEAD_DIM, (h + 1) * MEM_HEAD_DIM) for h in hs]
    s = [_dot_nt(q[:, cols[h]], kv_ref[0, :, cols[h]]) * (MEM_HEAD_DIM ** -0.5) for h in hs]
    p = [jnp.exp(s[h] - jnp.max(s[h], axis=1, keepdims=True)) for h in hs]
    p = [p[h] / jnp.sum(p[h], axis=1, keepdims=True) for h in hs]
    o = jnp.concatenate([_dot(p[h], kv_ref[0, :, D_MODEL + h * MEM_HEAD_DIM:D_MODEL + (h + 1) * MEM_HEAD_DIM])
                         for h in hs], axis=1)
    y = jnp.dot(o.astype(_bf16), wo_ref[...], preferred_element_type=_f32)
    o_ref[...] = _residual_ln(x, y, ln_ref[0:1, :], ln_ref[1:2, :])


def _xattn(x, kv, wq, wo, ln, seq_of_tile, tm):
    t = x.shape[0]
    full = lambda a: pl.BlockSpec(a.shape, lambda i, s: (0, 0))
    return pl.pallas_call(
        _xattn_body,
        grid_spec=pltpu.PrefetchScalarGridSpec(
            num_scalar_prefetch=1,
            grid=(t // tm,),
            in_specs=[pl.BlockSpec((tm, D_MODEL), lambda i, s: (i, 0)),
                      pl.BlockSpec((1,) + kv.shape[1:], lambda i, s: (s[i], 0, 0)),
                      full(wq), full(wo), full(ln)],
            out_specs=pl.BlockSpec((tm, D_MODEL), lambda i, s: (i, 0))),
        out_shape=jax.ShapeDtypeStruct((t, D_MODEL), _f32),
        compiler_params=_params(("parallel",)),
        name="mem_xattn",
    )(seq_of_tile, x, kv, wq, wo, ln)


def _ffn_body(x_ref, w1_ref, w2_ref, ln_ref, o_ref):
    x = x_ref[...]
    xb = x.astype(_bf16)
    gate = jnp.dot(xb, w1_ref[0], preferred_element_type=_f32)
    up = jnp.dot(xb, w1_ref[1], preferred_element_type=_f32)
    y = jnp.dot((_silu(gate) * up).astype(_bf16), w2_ref[...], preferred_element_type=_f32)
    o_ref[...] = _residual_ln(x, y, ln_ref[0:1, :], ln_ref[1:2, :])


def _ffn(x, w1, w2, ln, tm):
    t = x.shape[0]
    return pl.pallas_call(
        _ffn_body,
        grid=(t // tm,),
        in_specs=[pl.BlockSpec((tm, D_MODEL), lambda i: (i, 0)),
                  _resident(w1.shape), _resident(w2.shape), _resident(ln.shape)],
        out_specs=pl.BlockSpec((tm, D_MODEL), lambda i: (i, 0)),
        out_shape=jax.ShapeDtypeStruct((t, D_MODEL), _f32),
        compiler_params=_params(("parallel",)),
        name="swiglu_ffn",
    )(x, w1, w2, ln)


def _boundary_tables(seq_lens, tile):
    first, last, pos, seq = [], [], [], []
    for sid, length in enumerate(seq_lens):
        n = length // tile
        for b in range(n):
            first.append(int(b == 0))
            last.append(int(b == n - 1))
            pos.append(b)
            seq.append(sid)
    as_i32 = lambda v: jnp.asarray(np.asarray(v, np.int32))
    return as_i32(first), as_i32(last), as_i32(pos), as_i32(seq)


def _permute_in_cols(w):
    pad = jnp.zeros(w.shape[:-1] + (IN_COLS_PAD - 8464,), w.dtype)
    return jnp.concatenate([w[..., 0:3072], w[..., 3328:4864], w[..., 4864:5376], w[..., 5392:8464],
                            w[..., 3072:3200], w[..., 3200:3328], w[..., 5376:5392], pad], axis=-1)


def _rows8(*rows):
    n = rows[0].shape[-1]
    out = jnp.zeros((8, n), _f32)
    for r, v in enumerate(rows):
        out = out.at[r].set(v.astype(_f32))
    return out


def _lane_pad(v, offset):
    return jnp.zeros((LANES,), _f32).at[offset:offset + v.shape[0]].set(v.astype(_f32))


def kernel(x_prompt, x_sample, mem_prompt, mem_sample, w_in, hgrn_lb_logits, hgrn_norm_g, attn_sink,
           gdn_conv_w, gdn_a_log, gdn_dt_bias, gdn_norm_g, w_branch_a, w_branch_b, w_branch_c, w_mix_out,
           w_mem_q, w_mem_kv, w_mem_o, w_ffn_in, w_ffn_out, ln_g, ln_b):
    depth = w_in.shape[0]
    d = x_prompt.shape[-1]
    seq_lens = (x_prompt.shape[1],) * x_prompt.shape[0] + (x_sample.shape[1],) * x_sample.shape[0]
    n_prompt = x_prompt.shape[0] * x_prompt.shape[1]
    x = jnp.concatenate([x_prompt.reshape(-1, d), x_sample.reshape(-1, d)], axis=0)
    mem = jnp.concatenate([mem_prompt, mem_sample], axis=0)
    n_seq, n_mem, _ = mem.shape
    t = x.shape[0]

    tm = math.gcd(512, *seq_lens)
    tm_proj = math.gcd(512, t)
    rows_a = math.gcd(HGRN_CHUNKS_PER_STEP * CHUNK, *seq_lens)
    first_a, last_a, _, _ = _boundary_tables(seq_lens, rows_a)
    last_a_rev = last_a[::-1]
    rows_g = math.gcd(GDN_CHUNKS_PER_STEP * CHUNK, *seq_lens)
    first_g, last_g, _, _ = _boundary_tables(seq_lens, rows_g)
    last_g_rev = last_g[::-1]
    nb_swa = math.gcd(SWA_BLOCKS_PER_STEP * B_BLOCK, *seq_lens) // B_BLOCK
    first_w, last_w, pos_w, _ = _boundary_tables(seq_lens, nb_swa * B_BLOCK)
    first_t, last_t, _, seq_t = _boundary_tables(seq_lens, tm)

    s_max = max(seq_lens)
    inv = ROPE_THETA ** (-jnp.arange(0, B_HEAD_DIM, 2, dtype=_f32) / B_HEAD_DIM)
    ang = jnp.arange(s_max, dtype=_f32)[:, None] * inv[None, :]
    cos_t = jnp.tile(jnp.cos(ang), (1, 4))
    sin_t = jnp.tile(jnp.concatenate([-jnp.sin(ang), jnp.sin(ang)], axis=1), (1, 2))

    cum = jnp.cumsum(jax.nn.softmax(hgrn_lb_logits.astype(_f32), axis=1), axis=1)
    lb = cum - cum[:, :1]

    w_in_p = _permute_in_cols(w_in).astype(_bf16)
    bf = lambda w: w.astype(_bf16)
    w_a, w_b, w_c, w_mix = bf(w_branch_a), bf(w_branch_b), bf(w_branch_c), bf(w_mix_out)
    w_q, w_kv, w_o = bf(w_mem_q), bf(w_mem_kv), bf(w_mem_o)
    w_f1, w_f2 = bf(w_ffn_in), bf(w_ffn_out)
    mem2 = mem.reshape(n_seq * n_mem, d)

    for l in range(depth):
        h = _matmul(x, _col_tiles(w_in_p[l], IN_COLS_PAD // 2), tm_proj)
        lb_rows = []
        for dirn in range(2):
            lbd = lb[dirn, l]
            lb_rows += [jnp.log(lbd), jnp.log1p(-lbd), 1.0 - lbd]
        oa_f, oa_b = _hgrn(h, _rows8(*lb_rows), first_a, last_a_rev, rows_a)
        o_b = _swa(h, cos_t, sin_t, _rows8(_lane_pad(attn_sink[l], 0)), first_w, last_w, pos_w, nb_swa)
        qkv = _gdn_prep(h, _rows8(*[gdn_conv_w[l, j] for j in range(CONV_K)]), first_t, last_t, tm)
        gconst = _rows8(_lane_pad(-jnp.exp(gdn_a_log[l].astype(_f32)).reshape(-1), 2 * HEADS),
                        _lane_pad(gdn_dt_bias[l].reshape(-1), 2 * HEADS))
        oc_f, oc_b = _gdn(qkv, h, gconst, first_g, last_g_rev, rows_g)
        nrm = _rows8(hgrn_norm_g[l], gdn_norm_g[l])
        x = _merge(x, h, oa_f, oa_b, o_b, oc_f, oc_b, w_a[l], w_b[l], w_c[l], w_mix[l], nrm,
                   _rows8(ln_g[l, 0], ln_b[l, 0]), tm)
        kv = _matmul(mem2, _col_tiles(w_kv[l], 512), n_mem).reshape(n_seq, n_mem, 2 * d)
        x = _xattn(x, kv, w_q[l], w_o[l], _rows8(ln_g[l, 1], ln_b[l, 1]), seq_t, tm)
        x = _ffn(x, _col_tiles(w_f1[l], FFN_HIDDEN), w_f2[l], _rows8(ln_g[l, 2], ln_b[l, 2]), tm)

    y_prompt = x[:n_prompt].reshape(x_prompt.shape)
    y_sample = x[n_prompt:].reshape(x_sample.shape)
    return (y_prompt, y_sample)
```

```python
import math

import numpy as np
import jax
import jax.numpy as jnp
from jax import lax
from jax.experimental import pallas as pl
from jax.experimental.pallas import tpu as pltpu

D_MODEL = 1024
DEPTH = 4
HEADS = 4
HEAD_DIM = 128
WIDTH = HEADS * HEAD_DIM
B_Q_HEADS = 8
B_KV_HEADS = 2
B_GROUP = B_Q_HEADS // B_KV_HEADS
B_HEAD_DIM = 64
WINDOW = 128
B_BLOCK = 128
ROPE_THETA = 10000.0
CONV_K = 5
MEM_HEADS = 4
MEM_HEAD_DIM = D_MODEL // MEM_HEADS
FFN_HIDDEN = 2816
DN_ALPHA = (2 * DEPTH) ** 0.25

CHUNK = 64
GDN_CHUNKS_PER_STEP = 4
HGRN_CHUNKS_PER_STEP = 4
SWA_BLOCKS_PER_STEP = 4
LANES = 128
LOG2_E = math.log2(math.e)
VMEM_LIMIT = 56 * 1024 * 1024

COL_A_Q, COL_A_FF, COL_A_FB, COL_A_I, COL_A_G = 0, 512, 1024, 1536, 2048
COL_B_Q = 2560
COL_C_Q, COL_C_K, COL_C_V = 3072, 3584, 4096
COL_C_GATE = 4608
COL_MERGE = 5120
COL_B_K, COL_B_V = 8192, 8320
COL_GB = 8448
IN_COLS_PAD = 8704

_f32 = jnp.float32
_bf16 = jnp.bfloat16
BRANCH_DTYPE = _bf16


def _dot(a, b):
    return jnp.dot(a.astype(_bf16), b.astype(_bf16), preferred_element_type=_f32)


def _dot_nt(a, b):
    return lax.dot_general(a.astype(_bf16), b.astype(_bf16), (((1,), (1,)), ((), ())),
                           preferred_element_type=_f32)


def _dot_tn(a, b):
    return lax.dot_general(a.astype(_bf16), b.astype(_bf16), (((0,), (0,)), ((), ())),
                           preferred_element_type=_f32)


def _split3(x):
    hi = x.astype(_bf16)
    r1 = x - hi.astype(_f32)
    mid = r1.astype(_bf16)
    lo = (r1 - mid.astype(_f32)).astype(_bf16)
    return hi, mid, lo


def _exact_dot(sel, parts):
    dot = lambda p: jnp.dot(sel, p, preferred_element_type=_f32)
    return dot(parts[0]) + dot(parts[1]) + dot(parts[2])


def _silu(x):
    return x * jax.nn.sigmoid(x)


def _iota2(shape, dim):
    return lax.broadcasted_iota(jnp.int32, shape, dim)


def _params(sem):
    return pltpu.CompilerParams(dimension_semantics=sem, vmem_limit_bytes=VMEM_LIMIT)


def _resident(shape):
    return pl.BlockSpec(shape, lambda *_: (0,) * len(shape), pipeline_mode=pl.Buffered(1))


def _col_tiles(w, tn):
    k, n = w.shape
    return w.reshape(k, n // tn, tn).transpose(1, 0, 2)


def _matmul_body(x_ref, w_ref, o_ref, xb_ref):
    j = pl.program_id(1)

    @pl.when(j == 0)
    def _():
        xb_ref[...] = x_ref[...].astype(_bf16)

    o_ref[...] = jnp.dot(xb_ref[...], w_ref[j], preferred_element_type=_f32)


def _matmul(x, w_tiles, tm):
    t, k = x.shape
    nt, _, tn = w_tiles.shape
    n = nt * tn
    return pl.pallas_call(
        _matmul_body,
        grid=(t // tm, nt),
        in_specs=[pl.BlockSpec((tm, k), lambda i, j: (i, 0)), _resident(w_tiles.shape)],
        out_specs=pl.BlockSpec((tm, tn), lambda i, j: (i, j)),
        out_shape=jax.ShapeDtypeStruct((t, n), _f32),
        scratch_shapes=[pltpu.VMEM((tm, k), _bf16)],
        compiler_params=_params(("parallel", "arbitrary")),
        name="dense_proj",
    )(x, w_tiles)


def _chunk_masks(c, reverse):
    row = _iota2((c, c), 0)
    col = _iota2((c, c), 1)
    if reverse:
        return col >= row, col > row
    return col <= row, col < row


def _level_mask(c, blk, reverse, reps=1):
    row = _iota2((c, reps * c), 0)
    col = _iota2((c, reps * c), 1) % c
    half = blk // 2
    same = (row // blk) == (col // blk)
    r_hi = (row % blk) >= half
    c_hi = (col % blk) >= half
    if reverse:
        return same & jnp.logical_not(r_hi) & c_hi
    return same & r_hi & jnp.logical_not(c_hi)


def _block_ref_rows(g, blk, reverse):
    c, n = g.shape
    idx = blk // 2 if reverse else blk // 2 - 1
    rows = max(blk, 8)
    g3 = g.reshape(c // rows, rows, n)
    pick = lambda r: jnp.broadcast_to(g3[:, r:r + 1, :], g3.shape)
    out = pick(idx)
    if blk < rows:
        sub = lax.broadcasted_iota(jnp.int32, g3.shape, 1)
        for b in range(1, rows // blk):
            out = jnp.where(sub >= b * blk, pick(b * blk + idx), out)
    return out.reshape(c, n)


def _levels(c):
    out, blk = [], c
    while blk >= 2:
        out.append(blk)
        blk //= 2
    return out


def _hgrn_body(first_f_ref, first_b_ref, zq_f, zf_f, zi_f, zq_b, zf_b, zi_b, lb_ref, o_f, o_b, st_ref):
    c = CHUNK
    step = pl.program_id(0)

    @pl.when(first_f_ref[step] == 1)
    def _():
        st_ref[0:HEADS] = jnp.zeros((HEADS, HEAD_DIM, HEAD_DIM), _f32)

    @pl.when(first_b_ref[step] == 1)
    def _():
        st_ref[HEADS:2 * HEADS] = jnp.zeros((HEADS, HEAD_DIM, HEAD_DIM), _f32)

    nsub = zq_f.shape[0] // c
    row = _iota2((c, c), 0)
    col = _iota2((c, c), 1)
    eye = row == col
    blks = _levels(c)
    units = []
    for d, (zq_ref, zf_ref, zi_ref) in enumerate(((zq_f, zf_f, zi_f), (zq_b, zf_b, zi_b))):
        reverse = d == 1
        incl, _ = _chunk_masks(c, reverse)
        tri = incl.astype(_bf16)
        masks = [_level_mask(c, blk, reverse) for blk in blks]
        zf = zf_ref[...]
        log_lb = lb_ref[3 * d:3 * d + 1, :]
        log_1m_lb = lb_ref[3 * d + 1:3 * d + 2, :]
        one_m_lb = lb_ref[3 * d + 2:3 * d + 3, :]
        ez = jnp.exp(-jnp.abs(zf))
        one_p = 1.0 + ez
        log_sig = jnp.minimum(zf, 0.0) - jnp.log(one_p)
        b = log_1m_lb + log_sig
        log_f = jnp.maximum(log_lb, b) + jnp.log(1.0 + jnp.exp(-jnp.abs(log_lb - b)))
        key_all = one_m_lb * (jnp.where(zf >= 0.0, ez, 1.0) / one_p)
        rpos = _iota2((c, HEAD_DIM), 0)
        later = [((rpos % blk) < blk // 2) if reverse else ((rpos % blk) >= blk // 2) for blk in blks]
        sgn = [jnp.where(m, LOG2_E, -LOG2_E) for m in later]
        q_all = _silu(zq_ref[...]) * (HEAD_DIM ** -0.5)
        v_all = zi_ref[...]
        tot_row = 0 if reverse else c - 1
        for s in (range(nsub - 1, -1, -1) if reverse else range(nsub)):
            rs = slice(s * c, (s + 1) * c)
            gc_all = _exact_dot(tri, _split3(log_f[rs]))
            ref_all = [_block_ref_rows(gc_all, blk, reverse) for blk in blks]
            for h in range(HEADS):
                sl = slice(h * HEAD_DIM, (h + 1) * HEAD_DIM)
                units.append(dict(q=q_all[rs, sl], k=key_all[rs, sl], v=v_all[rs, sl], gc=gc_all[:, sl],
                                  refs=[r[:, sl] for r in ref_all], masks=masks, sgn=sgn, rows=rs,
                                  g_tot=gc_all[tot_row:tot_row + 1, sl]))

    scores = [jnp.where(eye, _dot_nt(u["q"], u["k"]), 0.0) for u in units]
    for l in range(len(blks)):
        e = [jnp.exp2((u["gc"] - u["refs"][l]) * u["sgn"][l]) for u in units]
        scores = [jnp.where(u["masks"][l], _dot_nt(u["q"] * e[i], u["k"] * e[i]), scores[i])
                  for i, u in enumerate(units)]
    intra = [_dot(scores[i], u["v"]) for i, u in enumerate(units)]
    qd = [u["q"] * jnp.exp(u["gc"]) for u in units]
    kd = [u["k"] * jnp.exp(u["g_tot"] - u["gc"]) for u in units]

    st = [st_ref[i] for i in range(2 * HEADS)]
    for slot in range(nsub):
        for d in range(2):
            for h in range(HEADS):
                i, j = (d * nsub + slot) * HEADS + h, d * HEADS + h
                out = intra[i] + _dot_nt(qd[i], st[j])
                st[j] = st[j] * jnp.exp(units[i]["g_tot"]) + _dot_tn(units[i]["v"], kd[i])
                (o_f, o_b)[d][units[i]["rows"], h * HEAD_DIM:(h + 1) * HEAD_DIM] = out.astype(o_f.dtype)
    for i in range(2 * HEADS):
        st_ref[i] = st[i]


def _hgrn(h, lb_consts, first_f, first_b, rows):
    t = h.shape[0]
    n = t // rows
    fwd = lambda col: pl.BlockSpec((rows, WIDTH), lambda i, a, b: (i, col // WIDTH))
    bwd = lambda col: pl.BlockSpec((rows, WIDTH), lambda i, a, b: (n - 1 - i, col // WIDTH))
    out = jax.ShapeDtypeStruct((t, WIDTH), BRANCH_DTYPE)
    return pl.pallas_call(
        _hgrn_body,
        grid_spec=pltpu.PrefetchScalarGridSpec(
            num_scalar_prefetch=2,
            grid=(n,),
            in_specs=[fwd(COL_A_Q), fwd(COL_A_FF), fwd(COL_A_I), bwd(COL_A_Q), bwd(COL_A_FB), bwd(COL_A_I),
                      pl.BlockSpec((8, WIDTH), lambda i, a, b: (0, 0))],
            out_specs=[fwd(0), bwd(0)],
            scratch_shapes=[pltpu.VMEM((2 * HEADS, HEAD_DIM, HEAD_DIM), _f32)]),
        out_shape=[out, out],
        compiler_params=_params(("arbitrary",)),
        name="hgrn",
    )(first_f, first_b, h, h, h, h, h, h, lb_consts)


def _gdn_prep_body(first_ref, last_ref, prev_ref, cur_ref, next_ref, w_ref, o_ref, ext_ref):
    i = pl.program_id(0)
    tm = cur_ref.shape[0]
    halo = CONV_K // 2
    ext_ref[0:8, :] = jnp.where(first_ref[i] == 1, 0.0, prev_ref[...])
    ext_ref[8:8 + tm, :] = cur_ref[...]
    ext_ref[8 + tm:16 + tm, :] = jnp.where(last_ref[i] == 1, 0.0, next_ref[...])
    acc = cur_ref[...] * w_ref[halo:halo + 1, :]
    for j in range(CONV_K):
        if j != halo:
            off = 8 - halo + j
            acc = acc + ext_ref[off:off + tm, :] * w_ref[j:j + 1, :]
    y = _silu(acc)
    for part in range(3):
        for h in range(HEADS):
            lo = part * WIDTH + h * HEAD_DIM
            x = y[:, lo:lo + HEAD_DIM]
            if part < 2:
                x = x * lax.rsqrt(jnp.sum(x * x, axis=1, keepdims=True) + 1e-6)
            if part == 0:
                x = x * (HEAD_DIM ** -0.5)
            o_ref[:, lo:lo + HEAD_DIM] = x


def _gdn_prep(h, conv_w, first, last, tm):
    t = h.shape[0]
    n = t // tm
    r8 = tm // 8
    cb = COL_C_Q // (3 * WIDTH)
    return pl.pallas_call(
        _gdn_prep_body,
        grid_spec=pltpu.PrefetchScalarGridSpec(
            num_scalar_prefetch=2,
            grid=(n,),
            in_specs=[pl.BlockSpec((8, 3 * WIDTH), lambda i, f, l: (jnp.maximum(i * r8 - 1, 0), cb)),
                      pl.BlockSpec((tm, 3 * WIDTH), lambda i, f, l: (i, cb)),
                      pl.BlockSpec((8, 3 * WIDTH), lambda i, f, l: (jnp.minimum((i + 1) * r8, n * r8 - 1), cb)),
                      pl.BlockSpec((8, 3 * WIDTH), lambda i, f, l: (0, 0))],
            out_specs=pl.BlockSpec((tm, 3 * WIDTH), lambda i, f, l: (i, 0)),
            scratch_shapes=[pltpu.VMEM((tm + 16, 3 * WIDTH), _f32)]),
        out_shape=jax.ShapeDtypeStruct((t, 3 * WIDTH), _f32),
        compiler_params=_params(("parallel",)),
        name="gdn_prep",
    )(first, last, h, h, h, conv_w)


def _gdn_body(first_f_ref, first_b_ref, q_f, k_f, v_f, gb_f, q_b, k_b, v_b, gb_b, cst_ref, o_f, o_b, st_ref):
    c = CHUNK
    step = pl.program_id(0)

    @pl.when(first_f_ref[step] == 1)
    def _():
        st_ref[0:HEADS] = jnp.zeros((HEADS, HEAD_DIM, HEAD_DIM), _f32)

    @pl.when(first_b_ref[step] == 1)
    def _():
        st_ref[HEADS:2 * HEADS] = jnp.zeros((HEADS, HEAD_DIM, HEAD_DIM), _f32)

    nsub = q_f.shape[0] // c
    row = _iota2((c, c), 0)
    col = _iota2((c, c), 1)
    eye = (row == col).astype(_f32)
    neg_a = cst_ref[0:1, :]
    dt_bias = cst_ref[1:2, :]
    sizes = _levels(c)[::-1]
    units = []
    for d, (q_ref, k_ref, v_ref, gb_ref) in enumerate(((q_f, k_f, v_f, gb_f), (q_b, k_b, v_b, gb_b))):
        reverse = d == 1
        incl, strict = _chunk_masks(c, reverse)
        tri = incl.astype(_bf16)
        masks = [_level_mask(c, blk, reverse) for blk in sizes]
        gb = gb_ref[...]
        beta_t = jax.nn.sigmoid(gb)
        zz = gb + dt_bias
        g_in = neg_a * (jnp.maximum(zz, 0.0) + jnp.log1p(jnp.exp(-jnp.abs(zz))))
        tot_row = 0 if reverse else c - 1
        for s in (range(nsub - 1, -1, -1) if reverse else range(nsub)):
            rs = slice(s * c, (s + 1) * c)
            gc_t = _exact_dot(tri, _split3(g_in[rs]))
            gc_tt = jnp.transpose(gc_t)
            for h in range(HEADS):
                sl = slice(h * HEAD_DIM, (h + 1) * HEAD_DIM)
                j = 2 * HEADS + d * HEADS + h
                g_col = gc_t[:, j:j + 1]
                decay = jnp.where(incl, jnp.exp(jnp.minimum(g_col - gc_tt[j:j + 1, :], 0.0)), 0.0)
                units.append(dict(q=q_ref[rs, sl], k=k_ref[rs, sl], v=v_ref[rs, sl], rows=rs,
                                  g_col=g_col, decay=decay, strict=strict, masks=masks,
                                  beta=beta_t[rs, d * HEADS + h:d * HEADS + h + 1],
                                  g_tot=gc_t[tot_row:tot_row + 1, j:j + 1]))

    k_b = [u["k"].astype(_bf16) for u in units]
    a = [jnp.where(u["strict"], _dot_nt(k_b[i], k_b[i]) * u["decay"] * u["beta"], 0.0) for i, u in enumerate(units)]
    qk = [_dot_nt(u["q"], k_b[i]) * u["decay"] for i, u in enumerate(units)]
    inv = [eye - jnp.where(u["masks"][0], a[i], 0.0) for i, u in enumerate(units)]
    for l in range(1, len(sizes)):
        inv_b = [m.astype(_bf16) for m in inv]
        t1 = [_dot(jnp.where(u["masks"][l], a[i], 0.0), inv_b[i]) for i, u in enumerate(units)]
        inv = [inv[i] - _dot(inv_b[i], t1[i]) for i in range(len(units))]
    inv_b = [m.astype(_bf16) for m in inv]
    e_g = [jnp.exp(u["g_col"]) for u in units]
    sol = [_dot(inv_b[i], jnp.concatenate([u["v"] * u["beta"], u["k"] * (u["beta"] * e_g[i])], axis=1))
           for i, u in enumerate(units)]
    uu = [m[:, :HEAD_DIM] for m in sol]
    ww = [m[:, HEAD_DIM:] for m in sol]
    qd = [u["q"] * e_g[i] for i, u in enumerate(units)]
    kd = [u["k"] * jnp.exp(u["g_tot"] - u["g_col"]) for u in units]
    qk_sol = [_dot(qk[i], sol[i]) for i in range(len(units))]
    out0 = [m[:, :HEAD_DIM] for m in qk_sol]
    q_eff = [(qd[i] - qk_sol[i][:, HEAD_DIM:]).astype(_bf16) for i in range(len(units))]
    kd_sol = [_dot_tn(kd[i], sol[i]) for i in range(len(units))]
    gain = [m[:, :HEAD_DIM] for m in kd_sol]
    trans = [m[:, HEAD_DIM:].astype(_bf16) for m in kd_sol]

    st = [st_ref[i] for i in range(2 * HEADS)]
    for slot in range(nsub):
        for d in range(2):
            for h in range(HEADS):
                i, j = (d * nsub + slot) * HEADS + h, d * HEADS + h
                st_b = st[j].astype(_bf16)
                out = out0[i] + _dot(q_eff[i], st_b)
                st[j] = st[j] * jnp.exp(units[i]["g_tot"]) - _dot(trans[i], st_b) + gain[i]
                (o_f, o_b)[d][units[i]["rows"], h * HEAD_DIM:(h + 1) * HEAD_DIM] = out.astype(o_f.dtype)
    for i in range(2 * HEADS):
        st_ref[i] = st[i]


def _gdn(qkv, h, consts, first_f, first_b, rows):
    t = qkv.shape[0]
    n = t // rows
    fwd = lambda w, colblk: pl.BlockSpec((rows, w), lambda i, a, b: (i, colblk))
    bwd = lambda w, colblk: pl.BlockSpec((rows, w), lambda i, a, b: (n - 1 - i, colblk))
    gcol = COL_GB // LANES
    out = jax.ShapeDtypeStruct((t, WIDTH), BRANCH_DTYPE)
    return pl.pallas_call(
        _gdn_body,
        grid_spec=pltpu.PrefetchScalarGridSpec(
            num_scalar_prefetch=2,
            grid=(n,),
            in_specs=[fwd(WIDTH, 0), fwd(WIDTH, 1), fwd(WIDTH, 2), fwd(LANES, gcol),
                      bwd(WIDTH, 0), bwd(WIDTH, 1), bwd(WIDTH, 2), bwd(LANES, gcol),
                      pl.BlockSpec((8, LANES), lambda i, a, b: (0, 0))],
            out_specs=[fwd(WIDTH, 0), bwd(WIDTH, 0)],
            scratch_shapes=[pltpu.VMEM((2 * HEADS, HEAD_DIM, HEAD_DIM), _f32)]),
        out_shape=[out, out],
        compiler_params=_params(("arbitrary",)),
        name="gdn",
    )(first_f, first_b, qkv, qkv, qkv, h, qkv, qkv, qkv, h, consts)


def _rope(x, cos, sin_signed):
    src = _iota2((LANES, LANES), 0)
    dst = _iota2((LANES, LANES), 1)
    half = B_HEAD_DIM // 2
    partner = jnp.where(dst % B_HEAD_DIM < half, dst + half, dst - half)
    rot = jnp.dot(x.astype(_bf16), (src == partner).astype(_bf16), preferred_element_type=_f32)
    return x * cos + rot * sin_signed


def _swa_body(first_ref, last_ref, pos_ref, q_ref, kp_ref, kc_ref, kn_ref, vp_ref, vc_ref, vn_ref,
              cp_ref, sp_ref, cc_ref, sc_ref, cn_ref, sn_ref, sink_ref, o_ref):
    i = pl.program_id(0)
    blk = B_BLOCK
    nb = q_ref.shape[0] // blk
    has_prev = first_ref[i] == 0
    has_next = last_ref[i] == 0
    k_all = jnp.concatenate([_rope(kp_ref[...], cp_ref[...], sp_ref[...]),
                             _rope(kc_ref[...], cc_ref[...], sc_ref[...]),
                             _rope(kn_ref[...], cn_ref[...], sn_ref[...])], axis=0)
    v_all = jnp.concatenate([vp_ref[...], vc_ref[...], vn_ref[...]], axis=0)
    nq = B_GROUP * blk
    kpos = _iota2((3 * blk, nq), 0)
    qpos = _iota2((3 * blk, nq), 1) % blk
    rel = kpos - qpos
    in_window = (rel >= 0) & (rel <= 2 * WINDOW)
    cos, sin_signed = cc_ref[...], sc_ref[...]
    scale = B_HEAD_DIM ** -0.5
    qr = [_rope(q_ref[:, p * LANES:(p + 1) * LANES], cos, sin_signed) * scale for p in range(B_Q_HEADS // 2)]
    sink_row = sink_ref[0:1, :]
    chains = [(b, kv) for b in range(nb) for kv in range(B_KV_HEADS)]
    q4, sink, kh, vh, ok = [], [], [], [], []
    for b, kv in chains:
        rows = slice(b * blk, (b + 1) * blk)
        heads = [qr[hq // 2][rows, (hq % 2) * B_HEAD_DIM:(hq % 2 + 1) * B_HEAD_DIM]
                 for hq in range(kv * B_GROUP, (kv + 1) * B_GROUP)]
        q4.append(jnp.concatenate(heads, axis=0))
        sink.append(jnp.concatenate([jnp.broadcast_to(sink_row[:, kv * B_GROUP + g:kv * B_GROUP + g + 1], (1, blk))
                                     for g in range(B_GROUP)], axis=1))
        kh.append(k_all[b * blk:(b + 3) * blk, kv * B_HEAD_DIM:(kv + 1) * B_HEAD_DIM])
        vh.append(v_all[b * blk:(b + 3) * blk, kv * B_HEAD_DIM:(kv + 1) * B_HEAD_DIM])
        m_ok = in_window
        if b == 0:
            m_ok = m_ok & (has_prev | (kpos >= blk))
        if b == nb - 1:
            m_ok = m_ok & (has_next | (kpos < 2 * blk))
        ok.append(m_ok)
    cs = range(len(chains))
    s = [jnp.where(ok[c], _dot_nt(kh[c], q4[c]), -jnp.inf) for c in cs]
    m = [jnp.maximum(jnp.max(s[c], axis=0, keepdims=True), sink[c]) for c in cs]
    p = [jnp.exp(s[c] - m[c]) for c in cs]
    denom = [jnp.sum(p[c], axis=0, keepdims=True) + jnp.exp(sink[c] - m[c]) for c in cs]
    o_t = [_dot_tn(vh[c], p[c]) / denom[c] for c in cs]
    for c, (b, kv) in enumerate(chains):
        for g in range(0, B_GROUP, 2):
            hq = kv * B_GROUP + g
            pair = jnp.concatenate([o_t[c][:, g * blk:(g + 1) * blk],
                                    o_t[c][:, (g + 1) * blk:(g + 2) * blk]], axis=0)
            o_ref[b * blk:(b + 1) * blk, hq * B_HEAD_DIM:(hq + 2) * B_HEAD_DIM] = (
                jnp.transpose(pair).astype(o_ref.dtype))


def _swa(h, cos_t, sin_t, sink, first, last, pos, nb):
    t = h.shape[0]
    rows = nb * B_BLOCK
    n = t // rows
    n128 = t // B_BLOCK
    npos = cos_t.shape[0] // B_BLOCK
    kcol, vcol = COL_B_K // LANES, COL_B_V // LANES
    before = lambda colblk: pl.BlockSpec((B_BLOCK, LANES), lambda i, a, b, p: (jnp.maximum(i * nb - 1, 0), colblk))
    own = lambda colblk: pl.BlockSpec((rows, LANES), lambda i, a, b, p: (i, colblk))
    after = lambda colblk: pl.BlockSpec((B_BLOCK, LANES),
                                        lambda i, a, b, p: (jnp.minimum((i + 1) * nb, n128 - 1), colblk))
    t_before = pl.BlockSpec((B_BLOCK, LANES), lambda i, a, b, p: (jnp.maximum(p[i] * nb - 1, 0), 0))
    t_own = pl.BlockSpec((rows, LANES), lambda i, a, b, p: (p[i], 0))
    t_after = pl.BlockSpec((B_BLOCK, LANES), lambda i, a, b, p: (jnp.minimum((p[i] + 1) * nb, npos - 1), 0))
    qw = B_Q_HEADS * B_HEAD_DIM
    return pl.pallas_call(
        _swa_body,
        grid_spec=pltpu.PrefetchScalarGridSpec(
            num_scalar_prefetch=3,
            grid=(n,),
            in_specs=[pl.BlockSpec((rows, qw), lambda i, a, b, p: (i, COL_B_Q // qw)),
                      before(kcol), own(kcol), after(kcol), before(vcol), own(vcol), after(vcol),
                      t_before, t_before, t_own, t_own, t_after, t_after,
                      pl.BlockSpec((8, LANES), lambda i, a, b, p: (0, 0))],
            out_specs=pl.BlockSpec((rows, qw), lambda i, a, b, p: (i, 0))),
        out_shape=jax.ShapeDtypeStruct((t, qw), BRANCH_DTYPE),
        compiler_params=_params(("parallel",)),
        name="window_attn",
    )(first, last, pos, h, h, h, h, h, h, h, cos_t, sin_t, cos_t, sin_t, cos_t, sin_t, sink)


def _residual_ln(x, y, g, b):
    z = DN_ALPHA * x + y
    mu = jnp.mean(z, axis=1, keepdims=True)
    zc = z - mu
    var = jnp.mean(zc * zc, axis=1, keepdims=True)
    return zc * lax.rsqrt(var + 1e-5) * g + b


def _gated_rms(o, gate, g):
    outs = []
    for h in range(HEADS):
        sl = slice(h * HEAD_DIM, (h + 1) * HEAD_DIM)
        x = o[:, sl]
        x = x * lax.rsqrt(jnp.mean(x * x, axis=1, keepdims=True) + 1e-6) * g
        outs.append(x * _silu(gate[:, sl]))
    return jnp.concatenate(outs, axis=1)


def _merge_body(x_ref, af_ref, ab_ref, ag_ref, ob_ref, cf_ref, cb_ref, cg_ref, ma_ref, mb_ref, mc_ref,
                wa_ref, wb_ref, wc_ref, wo_ref, nrm_ref, ln_ref, o_ref):
    up = lambda ref: ref[...].astype(_f32)
    oa = _gated_rms(up(af_ref) + up(ab_ref), ag_ref[...], nrm_ref[0:1, :])
    oc = _gated_rms(up(cf_ref) + up(cb_ref), cg_ref[...], nrm_ref[1:2, :])
    pa = jnp.dot(oa.astype(_bf16), wa_ref[...], preferred_element_type=_f32)
    pb = jnp.dot(ob_ref[...].astype(_bf16), wb_ref[...], preferred_element_type=_f32)
    pc = jnp.dot(oc.astype(_bf16), wc_ref[...], preferred_element_type=_f32)
    mix = (jax.nn.sigmoid(ma_ref[...]) * pa + jax.nn.sigmoid(mb_ref[...]) * pb
           + jax.nn.sigmoid(mc_ref[...]) * pc)
    y = jnp.dot(mix.astype(_bf16), wo_ref[...], preferred_element_type=_f32)
    o_ref[...] = _residual_ln(x_ref[...], y, ln_ref[0:1, :], ln_ref[1:2, :])


def _merge(x, h, oa_f, oa_b, ob, oc_f, oc_b, wa, wb, wc, wo, nrm, ln, tm):
    t = x.shape[0]
    row = lambda w, colblk=0: pl.BlockSpec((tm, w), lambda i: (i, colblk))
    full = lambda a: pl.BlockSpec(a.shape, lambda i: (0, 0))
    mcol = COL_MERGE // D_MODEL
    return pl.pallas_call(
        _merge_body,
        grid=(t // tm,),
        in_specs=[row(D_MODEL), row(WIDTH), row(WIDTH), row(WIDTH, COL_A_G // WIDTH), row(WIDTH),
                  row(WIDTH), row(WIDTH), row(WIDTH, COL_C_GATE // WIDTH),
                  row(D_MODEL, mcol), row(D_MODEL, mcol + 1), row(D_MODEL, mcol + 2),
                  full(wa), full(wb), full(wc), full(wo), full(nrm), full(ln)],
        out_specs=row(D_MODEL),
        out_shape=jax.ShapeDtypeStruct((t, D_MODEL), _f32),
        compiler_params=_params(("parallel",)),
        name="merge_mix",
    )(x, oa_f, oa_b, h, ob, oc_f, oc_b, h, h, h, h, wa, wb, wc, wo, nrm, ln)


def _xattn_body(seq_ref, x_ref, kv_ref, wq_ref, wo_ref, ln_ref, o_ref):
    x = x_ref[...]
    q = jnp.dot(x.astype(_bf16), wq_ref[...], preferred_element_type=_f32)
    hs = range(MEM_HEADS)
    cols = [slice(h * MEM_HEAD_DIM, (h + 1) * MEM_HEAD_DIM) for h in hs]
    s = [_dot_nt(q[:, cols[h]], kv_ref[0, :, cols[h]]) * (MEM_HEAD_DIM ** -0.5) for h in hs]
    p = [jnp.exp(s[h] - jnp.max(s[h], axis=1, keepdims=True)) for h in hs]
    p = [p[h] / jnp.sum(p[h], axis=1, keepdims=True) for h in hs]
    o = jnp.concatenate([_dot(p[h], kv_ref[0, :, D_MODEL + h * MEM_HEAD_DIM:D_MODEL + (h + 1) * MEM_HEAD_DIM])
                         for h in hs], axis=1)
    y = jnp.dot(o.astype(_bf16), wo_ref[...], preferred_element_type=_f32)
    o_ref[...] = _residual_ln(x, y, ln_ref[0:1, :], ln_ref[1:2, :])


def _xattn(x, kv, wq, wo, ln, seq_of_tile, tm):
    t = x.shape[0]
    full = lambda a: pl.BlockSpec(a.shape, lambda i, s: (0, 0))
    return pl.pallas_call(
        _xattn_body,
        grid_spec=pltpu.PrefetchScalarGridSpec(
            num_scalar_prefetch=1,
            grid=(t // tm,),
            in_specs=[pl.BlockSpec((tm, D_MODEL), lambda i, s: (i, 0)),
                      pl.BlockSpec((1,) + kv.shape[1:], lambda i, s: (s[i], 0, 0)),
                      full(wq), full(wo), full(ln)],
            out_specs=pl.BlockSpec((tm, D_MODEL), lambda i, s: (i, 0))),
        out_shape=jax.ShapeDtypeStruct((t, D_MODEL), _f32),
        compiler_params=_params(("parallel",)),
        name="mem_xattn",
    )(seq_of_tile, x, kv, wq, wo, ln)


def _ffn_body(x_ref, w1_ref, w2_ref, ln_ref, o_ref):
    x = x_ref[...]
    xb = x.astype(_bf16)
    gate = jnp.dot(xb, w1_ref[0], preferred_element_type=_f32)
    up = jnp.dot(xb, w1_ref[1], preferred_element_type=_f32)
    y = jnp.dot((_silu(gate) * up).astype(_bf16), w2_ref[...], preferred_element_type=_f32)
    o_ref[...] = _residual_ln(x, y, ln_ref[0:1, :], ln_ref[1:2, :])


def _ffn(x, w1, w2, ln, tm):
    t = x.shape[0]
    return pl.pallas_call(
        _ffn_body,
        grid=(t // tm,),
        in_specs=[pl.BlockSpec((tm, D_MODEL), lambda i: (i, 0)),
                  _resident(w1.shape), _resident(w2.shape), _resident(ln.shape)],
        out_specs=pl.BlockSpec((tm, D_MODEL), lambda i: (i, 0)),
        out_shape=jax.ShapeDtypeStruct((t, D_MODEL), _f32),
        compiler_params=_params(("parallel",)),
        name="swiglu_ffn",
    )(x, w1, w2, ln)


def _boundary_tables(seq_lens, tile):
    first, last, pos, seq = [], [], [], []
    for sid, length in enumerate(seq_lens):
        n = length // tile
        for b in range(n):
            first.append(int(b == 0))
            last.append(int(b == n - 1))
            pos.append(b)
            seq.append(sid)
    as_i32 = lambda v: jnp.asarray(np.asarray(v, np.int32))
    return as_i32(first), as_i32(last), as_i32(pos), as_i32(seq)


def _permute_in_cols(w):
    pad = jnp.zeros(w.shape[:-1] + (IN_COLS_PAD - 8464,), w.dtype)
    return jnp.concatenate([w[..., 0:3072], w[..., 3328:4864], w[..., 4864:5376], w[..., 5392:8464],
                            w[..., 3072:3200], w[..., 3200:3328], w[..., 5376:5392], pad], axis=-1)


def _rows8(*rows):
    n = rows[0].shape[-1]
    out = jnp.zeros((8, n), _f32)
    for r, v in enumerate(rows):
        out = out.at[r].set(v.astype(_f32))
    return out


def _lane_pad(v, offset):
    return jnp.zeros((LANES,), _f32).at[offset:offset + v.shape[0]].set(v.astype(_f32))


def kernel(x_prompt, x_sample, mem_prompt, mem_sample, w_in, hgrn_lb_logits, hgrn_norm_g, attn_sink,
           gdn_conv_w, gdn_a_log, gdn_dt_bias, gdn_norm_g, w_branch_a, w_branch_b, w_branch_c, w_mix_out,
           w_mem_q, w_mem_kv, w_mem_o, w_ffn_in, w_ffn_out, ln_g, ln_b):
    depth = w_in.shape[0]
    d = x_prompt.shape[-1]
    seq_lens = (x_prompt.shape[1],) * x_prompt.shape[0] + (x_sample.shape[1],) * x_sample.shape[0]
    n_prompt = x_prompt.shape[0] * x_prompt.shape[1]
    x = jnp.concatenate([x_prompt.reshape(-1, d), x_sample.reshape(-1, d)], axis=0)
    mem = jnp.concatenate([mem_prompt, mem_sample], axis=0)
    n_seq, n_mem, _ = mem.shape
    t = x.shape[0]

    tm = math.gcd(512, *seq_lens)
    tm_proj = math.gcd(512, t)
    rows_a = math.gcd(HGRN_CHUNKS_PER_STEP * CHUNK, *seq_lens)
    first_a, last_a, _, _ = _boundary_tables(seq_lens, rows_a)
    last_a_rev = last_a[::-1]
    rows_g = math.gcd(GDN_CHUNKS_PER_STEP * CHUNK, *seq_lens)
    first_g, last_g, _, _ = _boundary_tables(seq_lens, rows_g)
    last_g_rev = last_g[::-1]
    nb_swa = math.gcd(SWA_BLOCKS_PER_STEP * B_BLOCK, *seq_lens) // B_BLOCK
    first_w, last_w, pos_w, _ = _boundary_tables(seq_lens, nb_swa * B_BLOCK)
    first_t, last_t, _, seq_t = _boundary_tables(seq_lens, tm)

    s_max = max(seq_lens)
    inv = ROPE_THETA ** (-jnp.arange(0, B_HEAD_DIM, 2, dtype=_f32) / B_HEAD_DIM)
    ang = jnp.arange(s_max, dtype=_f32)[:, None] * inv[None, :]
    cos_t = jnp.tile(jnp.cos(ang), (1, 4))
    sin_t = jnp.tile(jnp.concatenate([-jnp.sin(ang), jnp.sin(ang)], axis=1), (1, 2))

    cum = jnp.cumsum(jax.nn.softmax(hgrn_lb_logits.astype(_f32), axis=1), axis=1)
    lb = cum - cum[:, :1]

    w_in_p = _permute_in_cols(w_in).astype(_bf16)
    bf = lambda w: w.astype(_bf16)
    w_a, w_b, w_c, w_mix = bf(w_branch_a), bf(w_branch_b), bf(w_branch_c), bf(w_mix_out)
    w_q, w_kv, w_o = bf(w_mem_q), bf(w_mem_kv), bf(w_mem_o)
    w_f1, w_f2 = bf(w_ffn_in), bf(w_ffn_out)
    mem2 = mem.reshape(n_seq * n_mem, d)

    for l in range(depth):
        h = _matmul(x, _col_tiles(w_in_p[l], IN_COLS_PAD // 2), tm_proj)
        lb_rows = []
        for dirn in range(2):
            lbd = lb[dirn, l]
            lb_rows += [jnp.log(lbd), jnp.log1p(-lbd), 1.0 - lbd]
        oa_f, oa_b = _hgrn(h, _rows8(*lb_rows), first_a, last_a_rev, rows_a)
        o_b = _swa(h, cos_t, sin_t, _rows8(_lane_pad(attn_sink[l], 0)), first_w, last_w, pos_w, nb_swa)
        qkv = _gdn_prep(h, _rows8(*[gdn_conv_w[l, j] for j in range(CONV_K)]), first_t, last_t, tm)
        gconst = _rows8(_lane_pad(-jnp.exp(gdn_a_log[l].astype(_f32)).reshape(-1), 2 * HEADS),
                        _lane_pad(gdn_dt_bias[l].reshape(-1), 2 * HEADS))
        oc_f, oc_b = _gdn(qkv, h, gconst, first_g, last_g_rev, rows_g)
        nrm = _rows8(hgrn_norm_g[l], gdn_norm_g[l])
        x = _merge(x, h, oa_f, oa_b, o_b, oc_f, oc_b, w_a[l], w_b[l], w_c[l], w_mix[l], nrm,
                   _rows8(ln_g[l, 0], ln_b[l, 0]), tm)
        kv = _matmul(mem2, _col_tiles(w_kv[l], 512), n_mem).reshape(n_seq, n_mem, 2 * d)
        x = _xattn(x, kv, w_q[l], w_o[l], _rows8(ln_g[l, 1], ln_b[l, 1]), seq_t, tm)
        x = _ffn(x, _col_tiles(w_f1[l], FFN_HIDDEN), w_f2[l], _rows8(ln_g[l, 2], ln_b[l, 2]), tm)

    y_prompt = x[:n_prompt].reshape(x_prompt.shape)
    y_sample = x[n_prompt:].reshape(x_sample.shape)
    return (y_prompt, y_sample)
```

```python
import math

import numpy as np
import jax
import jax.numpy as jnp
from jax import lax
from jax.experimental import pallas as pl
from jax.experimental.pallas import tpu as pltpu

D_MODEL = 1024
DEPTH = 4
HEADS = 4
HEAD_DIM = 128
WIDTH = HEADS * HEAD_DIM
B_Q_HEADS = 8
B_KV_HEADS = 2
B_GROUP = B_Q_HEADS // B_KV_HEADS
B_HEAD_DIM = 64
WINDOW = 128
B_BLOCK = 128
ROPE_THETA = 10000.0
CONV_K = 5
MEM_HEADS = 4
MEM_HEAD_DIM = D_MODEL // MEM_HEADS
FFN_HIDDEN = 2816
DN_ALPHA = (2 * DEPTH) ** 0.25

CHUNK = 64
GDN_CHUNKS_PER_STEP = 4
HGRN_CHUNKS_PER_STEP = 4
SWA_BLOCKS_PER_STEP = 4
LANES = 128
LOG2_E = math.log2(math.e)
VMEM_LIMIT = 56 * 1024 * 1024

COL_A_Q, COL_A_FF, COL_A_FB, COL_A_I, COL_A_G = 0, 512, 1024, 1536, 2048
COL_B_Q = 2560
COL_C_GATE = 3072
COL_B_K, COL_B_V = 3584, 3712
COL_GB = 3840
HA_COLS = 4096
COL_MERGE = 0
COL_C_QKV = 3072
HB_COLS = 4608

_f32 = jnp.float32
_bf16 = jnp.bfloat16
BRANCH_DTYPE = _bf16


def _dot(a, b):
    return jnp.dot(a.astype(_bf16), b.astype(_bf16), preferred_element_type=_f32)


def _dot_nt(a, b):
    return lax.dot_general(a.astype(_bf16), b.astype(_bf16), (((1,), (1,)), ((), ())),
                           preferred_element_type=_f32)


def _dot_tn(a, b):
    return lax.dot_general(a.astype(_bf16), b.astype(_bf16), (((0,), (0,)), ((), ())),
                           preferred_element_type=_f32)


def _split3(x):
    hi = x.astype(_bf16)
    r1 = x - hi.astype(_f32)
    mid = r1.astype(_bf16)
    lo = (r1 - mid.astype(_f32)).astype(_bf16)
    return hi, mid, lo


def _exact_dot(sel, parts):
    dot = lambda p: jnp.dot(sel, p, preferred_element_type=_f32)
    return dot(parts[0]) + dot(parts[1]) + dot(parts[2])


def _silu(x):
    return x * jax.nn.sigmoid(x)


def _iota2(shape, dim):
    return lax.broadcasted_iota(jnp.int32, shape, dim)


def _params(sem):
    return pltpu.CompilerParams(dimension_semantics=sem, vmem_limit_bytes=VMEM_LIMIT)


def _resident(shape):
    return pl.BlockSpec(shape, lambda *_: (0,) * len(shape), pipeline_mode=pl.Buffered(1))


def _col_tiles(w, tn):
    k, n = w.shape
    return w.reshape(k, n // tn, tn).transpose(1, 0, 2)


def _matmul_body(x_ref, w_ref, o_ref, xb_ref):
    j = pl.program_id(1)

    @pl.when(j == 0)
    def _():
        xb_ref[...] = x_ref[...].astype(_bf16)

    o_ref[...] = jnp.dot(xb_ref[...], w_ref[j], preferred_element_type=_f32)


def _matmul(x, w_tiles, tm):
    t, k = x.shape
    nt, _, tn = w_tiles.shape
    n = nt * tn
    return pl.pallas_call(
        _matmul_body,
        grid=(t // tm, nt),
        in_specs=[pl.BlockSpec((tm, k), lambda i, j: (i, 0)), _resident(w_tiles.shape)],
        out_specs=pl.BlockSpec((tm, tn), lambda i, j: (i, j)),
        out_shape=jax.ShapeDtypeStruct((t, n), _f32),
        scratch_shapes=[pltpu.VMEM((tm, k), _bf16)],
        compiler_params=_params(("parallel", "arbitrary")),
        name="dense_proj",
    )(x, w_tiles)


def _in_proj_a_body(x_ref, w_ref, lb_ref, o_ref):
    xb = x_ref[...].astype(_bf16)
    proj = lambda lo, hi: jnp.dot(xb, w_ref[:, lo:hi], preferred_element_type=_f32)

    def log_f(d):
        def fn(z):
            log_sig = jnp.minimum(z, 0.0) - jnp.log(1.0 + jnp.exp(-jnp.abs(z)))
            log_lb = lb_ref[2 * d:2 * d + 1, :]
            b = lb_ref[2 * d + 1:2 * d + 2, :] + log_sig
            return jnp.maximum(log_lb, b) + jnp.log(1.0 + jnp.exp(-jnp.abs(log_lb - b)))
        return fn

    groups = [(COL_A_Q, COL_A_Q + WIDTH, lambda z: _silu(z) * (HEAD_DIM ** -0.5)),
              (COL_A_FF, COL_A_FF + WIDTH, log_f(0)),
              (COL_A_FB, COL_A_FB + WIDTH, log_f(1)),
              (COL_A_I, o_ref.shape[1], lambda z: z)]
    z_next = proj(groups[0][0], groups[0][1])
    for g, (lo, hi, fn) in enumerate(groups):
        z = z_next
        if g + 1 < len(groups):
            z_next = proj(groups[g + 1][0], groups[g + 1][1])
        o_ref[:, lo:hi] = fn(z)


def _in_proj_b_body(x_ref, w_ref, o_ref):
    o_ref[...] = jnp.dot(x_ref[...].astype(_bf16), w_ref[...], preferred_element_type=_f32)


def _in_proj(body, name, x, w, consts, tm):
    t, k = x.shape
    n = w.shape[1]
    return pl.pallas_call(
        body,
        grid=(t // tm,),
        in_specs=[pl.BlockSpec((tm, k), lambda i: (i, 0)), _resident(w.shape)] + [_resident(c.shape) for c in consts],
        out_specs=pl.BlockSpec((tm, n), lambda i: (i, 0)),
        out_shape=jax.ShapeDtypeStruct((t, n), _f32),
        compiler_params=_params(("parallel",)),
        name=name,
    )(x, w, *consts)


def _chunk_masks(c, reverse):
    row = _iota2((c, c), 0)
    col = _iota2((c, c), 1)
    if reverse:
        return col >= row, col > row
    return col <= row, col < row


def _level_mask(c, blk, reverse, reps=1):
    row = _iota2((c, reps * c), 0)
    col = _iota2((c, reps * c), 1) % c
    half = blk // 2
    same = (row // blk) == (col // blk)
    r_hi = (row % blk) >= half
    c_hi = (col % blk) >= half
    if reverse:
        return same & jnp.logical_not(r_hi) & c_hi
    return same & r_hi & jnp.logical_not(c_hi)


def _block_ref_rows(g, blk, reverse):
    c, n = g.shape
    idx = blk // 2 if reverse else blk // 2 - 1
    rows = max(blk, 8)
    g3 = g.reshape(c // rows, rows, n)
    pick = lambda r: jnp.broadcast_to(g3[:, r:r + 1, :], g3.shape)
    out = pick(idx)
    if blk < rows:
        sub = lax.broadcasted_iota(jnp.int32, g3.shape, 1)
        for b in range(1, rows // blk):
            out = jnp.where(sub >= b * blk, pick(b * blk + idx), out)
    return out.reshape(c, n)


def _levels(c):
    out, blk = [], c
    while blk >= 2:
        out.append(blk)
        blk //= 2
    return out


def _hgrn_body(first_f_ref, first_b_ref, q_f, lf_f, v_f, q_b, lf_b, v_b, o_f, o_b, st_ref):
    c = CHUNK
    step = pl.program_id(0)

    @pl.when(first_f_ref[step] == 1)
    def _():
        st_ref[0:HEADS] = jnp.zeros((HEADS, HEAD_DIM, HEAD_DIM), _f32)

    @pl.when(first_b_ref[step] == 1)
    def _():
        st_ref[HEADS:2 * HEADS] = jnp.zeros((HEADS, HEAD_DIM, HEAD_DIM), _f32)

    nsub = q_f.shape[0] // c
    row = _iota2((c, c), 0)
    col = _iota2((c, c), 1)
    eye = row == col
    blks = _levels(c)
    units = []
    for d, (q_ref, lf_ref, v_ref) in enumerate(((q_f, lf_f, v_f), (q_b, lf_b, v_b))):
        reverse = d == 1
        incl, _ = _chunk_masks(c, reverse)
        tri = incl.astype(_bf16)
        masks = [_level_mask(c, blk, reverse) for blk in blks]
        log_f = lf_ref[...]
        key_all = 1.0 - jnp.exp(log_f)
        rpos = _iota2((c, HEAD_DIM), 0)
        later = [((rpos % blk) < blk // 2) if reverse else ((rpos % blk) >= blk // 2) for blk in blks]
        sgn = [jnp.where(m, LOG2_E, -LOG2_E) for m in later]
        q_all = q_ref[...]
        v_all = v_ref[...]
        tot_row = 0 if reverse else c - 1
        for s in (range(nsub - 1, -1, -1) if reverse else range(nsub)):
            rs = slice(s * c, (s + 1) * c)
            gc_all = _exact_dot(tri, _split3(log_f[rs]))
            ref_all = [_block_ref_rows(gc_all, blk, reverse) for blk in blks]
            for h in range(HEADS):
                sl = slice(h * HEAD_DIM, (h + 1) * HEAD_DIM)
                units.append(dict(q=q_all[rs, sl], k=key_all[rs, sl], v=v_all[rs, sl], gc=gc_all[:, sl],
                                  refs=[r[:, sl] for r in ref_all], masks=masks, sgn=sgn, rows=rs,
                                  g_tot=gc_all[tot_row:tot_row + 1, sl]))

    scores = [jnp.where(eye, _dot_nt(u["q"], u["k"]), 0.0) for u in units]
    for l in range(len(blks)):
        e = [jnp.exp2((u["gc"] - u["refs"][l]) * u["sgn"][l]) for u in units]
        scores = [jnp.where(u["masks"][l], _dot_nt(u["q"] * e[i], u["k"] * e[i]), scores[i])
                  for i, u in enumerate(units)]
    intra = [_dot(scores[i], u["v"]) for i, u in enumerate(units)]
    qd = [u["q"] * jnp.exp(u["gc"]) for u in units]
    kd = [u["k"] * jnp.exp(u["g_tot"] - u["gc"]) for u in units]

    st = [st_ref[i] for i in range(2 * HEADS)]
    for slot in range(nsub):
        for d in range(2):
            for h in range(HEADS):
                i, j = (d * nsub + slot) * HEADS + h, d * HEADS + h
                out = intra[i] + _dot_nt(qd[i], st[j])
                st[j] = st[j] * jnp.exp(units[i]["g_tot"]) + _dot_tn(units[i]["v"], kd[i])
                (o_f, o_b)[d][units[i]["rows"], h * HEAD_DIM:(h + 1) * HEAD_DIM] = out.astype(o_f.dtype)
    for i in range(2 * HEADS):
        st_ref[i] = st[i]


def _hgrn(h, first_f, first_b, rows):
    t = h.shape[0]
    n = t // rows
    fwd = lambda col: pl.BlockSpec((rows, WIDTH), lambda i, a, b: (i, col // WIDTH))
    bwd = lambda col: pl.BlockSpec((rows, WIDTH), lambda i, a, b: (n - 1 - i, col // WIDTH))
    out = jax.ShapeDtypeStruct((t, WIDTH), BRANCH_DTYPE)
    return pl.pallas_call(
        _hgrn_body,
        grid_spec=pltpu.PrefetchScalarGridSpec(
            num_scalar_prefetch=2,
            grid=(n,),
            in_specs=[fwd(COL_A_Q), fwd(COL_A_FF), fwd(COL_A_I), bwd(COL_A_Q), bwd(COL_A_FB), bwd(COL_A_I)],
            out_specs=[fwd(0), bwd(0)],
            scratch_shapes=[pltpu.VMEM((2 * HEADS, HEAD_DIM, HEAD_DIM), _f32)]),
        out_shape=[out, out],
        compiler_params=_params(("arbitrary",)),
        name="hgrn",
    )(first_f, first_b, h, h, h, h, h, h)


def _gdn_prep_body(first_ref, last_ref, prev_ref, cur_ref, next_ref, w_ref, o_ref, ext_ref):
    i = pl.program_id(0)
    tm = cur_ref.shape[0]
    halo = CONV_K // 2
    ext_ref[0:8, :] = jnp.where(first_ref[i] == 1, 0.0, prev_ref[...])
    ext_ref[8:8 + tm, :] = cur_ref[...]
    ext_ref[8 + tm:16 + tm, :] = jnp.where(last_ref[i] == 1, 0.0, next_ref[...])
    acc = cur_ref[...] * w_ref[halo:halo + 1, :]
    for j in range(CONV_K):
        if j != halo:
            off = 8 - halo + j
            acc = acc + ext_ref[off:off + tm, :] * w_ref[j:j + 1, :]
    y = _silu(acc)
    for part in range(3):
        for h in range(HEADS):
            lo = part * WIDTH + h * HEAD_DIM
            x = y[:, lo:lo + HEAD_DIM]
            if part < 2:
                x = x * lax.rsqrt(jnp.sum(x * x, axis=1, keepdims=True) + 1e-6)
            if part == 0:
                x = x * (HEAD_DIM ** -0.5)
            o_ref[:, lo:lo + HEAD_DIM] = x


def _gdn_prep(h, conv_w, first, last, tm):
    t = h.shape[0]
    n = t // tm
    r8 = tm // 8
    cb = COL_C_QKV // (3 * WIDTH)
    return pl.pallas_call(
        _gdn_prep_body,
        grid_spec=pltpu.PrefetchScalarGridSpec(
            num_scalar_prefetch=2,
            grid=(n,),
            in_specs=[pl.BlockSpec((8, 3 * WIDTH), lambda i, f, l: (jnp.maximum(i * r8 - 1, 0), cb)),
                      pl.BlockSpec((tm, 3 * WIDTH), lambda i, f, l: (i, cb)),
                      pl.BlockSpec((8, 3 * WIDTH), lambda i, f, l: (jnp.minimum((i + 1) * r8, n * r8 - 1), cb)),
                      pl.BlockSpec((8, 3 * WIDTH), lambda i, f, l: (0, 0))],
            out_specs=pl.BlockSpec((tm, 3 * WIDTH), lambda i, f, l: (i, 0)),
            scratch_shapes=[pltpu.VMEM((tm + 16, 3 * WIDTH), _f32)]),
        out_shape=jax.ShapeDtypeStruct((t, 3 * WIDTH), _f32),
        compiler_params=_params(("parallel",)),
        name="gdn_prep",
    )(first, last, h, h, h, conv_w)


def _gdn_body(first_f_ref, first_b_ref, q_f, k_f, v_f, gb_f, q_b, k_b, v_b, gb_b, cst_ref, o_f, o_b, st_ref):
    c = CHUNK
    step = pl.program_id(0)

    @pl.when(first_f_ref[step] == 1)
    def _():
        st_ref[0:HEADS] = jnp.zeros((HEADS, HEAD_DIM, HEAD_DIM), _f32)

    @pl.when(first_b_ref[step] == 1)
    def _():
        st_ref[HEADS:2 * HEADS] = jnp.zeros((HEADS, HEAD_DIM, HEAD_DIM), _f32)

    nsub = q_f.shape[0] // c
    row = _iota2((c, c), 0)
    col = _iota2((c, c), 1)
    eye = (row == col).astype(_f32)
    neg_a = cst_ref[0:1, :]
    dt_bias = cst_ref[1:2, :]
    sizes = _levels(c)[::-1]
    units = []
    for d, (q_ref, k_ref, v_ref, gb_ref) in enumerate(((q_f, k_f, v_f, gb_f), (q_b, k_b, v_b, gb_b))):
        reverse = d == 1
        incl, strict = _chunk_masks(c, reverse)
        tri = incl.astype(_bf16)
        masks = [_level_mask(c, blk, reverse) for blk in sizes]
        gb = gb_ref[...]
        beta_t = jax.nn.sigmoid(gb)
        zz = gb + dt_bias
        g_in = neg_a * (jnp.maximum(zz, 0.0) + jnp.log1p(jnp.exp(-jnp.abs(zz))))
        tot_row = 0 if reverse else c - 1
        for s in (range(nsub - 1, -1, -1) if reverse else range(nsub)):
            rs = slice(s * c, (s + 1) * c)
            gc_t = _exact_dot(tri, _split3(g_in[rs]))
            gc_tt = jnp.transpose(gc_t)
            for h in range(HEADS):
                sl = slice(h * HEAD_DIM, (h + 1) * HEAD_DIM)
                j = 2 * HEADS + d * HEADS + h
                g_col = gc_t[:, j:j + 1]
                decay = jnp.where(incl, jnp.exp(jnp.minimum(g_col - gc_tt[j:j + 1, :], 0.0)), 0.0)
                units.append(dict(q=q_ref[rs, sl], k=k_ref[rs, sl], v=v_ref[rs, sl], rows=rs,
                                  g_col=g_col, decay=decay, strict=strict, masks=masks,
                                  beta=beta_t[rs, d * HEADS + h:d * HEADS + h + 1],
                                  g_tot=gc_t[tot_row:tot_row + 1, j:j + 1]))

    k_b = [u["k"].astype(_bf16) for u in units]
    a = [jnp.where(u["strict"], _dot_nt(k_b[i], k_b[i]) * u["decay"] * u["beta"], 0.0) for i, u in enumerate(units)]
    qk = [_dot_nt(u["q"], k_b[i]) * u["decay"] for i, u in enumerate(units)]
    inv = [eye - jnp.where(u["masks"][0], a[i], 0.0) for i, u in enumerate(units)]
    for l in range(1, len(sizes)):
        inv_b = [m.astype(_bf16) for m in inv]
        t1 = [_dot(jnp.where(u["masks"][l], a[i], 0.0), inv_b[i]) for i, u in enumerate(units)]
        inv = [inv[i] - _dot(inv_b[i], t1[i]) for i in range(len(units))]
    inv_b = [m.astype(_bf16) for m in inv]
    e_g = [jnp.exp(u["g_col"]) for u in units]
    sol = [_dot(inv_b[i], jnp.concatenate([u["v"] * u["beta"], u["k"] * (u["beta"] * e_g[i])], axis=1))
           for i, u in enumerate(units)]
    uu = [m[:, :HEAD_DIM] for m in sol]
    ww = [m[:, HEAD_DIM:] for m in sol]
    qd = [u["q"] * e_g[i] for i, u in enumerate(units)]
    kd = [u["k"] * jnp.exp(u["g_tot"] - u["g_col"]) for u in units]
    qk_sol = [_dot(qk[i], sol[i]) for i in range(len(units))]
    out0 = [m[:, :HEAD_DIM] for m in qk_sol]
    q_eff = [(qd[i] - qk_sol[i][:, HEAD_DIM:]).astype(_bf16) for i in range(len(units))]
    kd_sol = [_dot_tn(kd[i], sol[i]) for i in range(len(units))]
    gain = [m[:, :HEAD_DIM] for m in kd_sol]
    trans = [m[:, HEAD_DIM:].astype(_bf16) for m in kd_sol]

    st = [st_ref[i] for i in range(2 * HEADS)]
    for slot in range(nsub):
        for d in range(2):
            for h in range(HEADS):
                i, j = (d * nsub + slot) * HEADS + h, d * HEADS + h
                st_b = st[j].astype(_bf16)
                out = out0[i] + _dot(q_eff[i], st_b)
                st[j] = st[j] * jnp.exp(units[i]["g_tot"]) - _dot(trans[i], st_b) + gain[i]
                (o_f, o_b)[d][units[i]["rows"], h * HEAD_DIM:(h + 1) * HEAD_DIM] = out.astype(o_f.dtype)
    for i in range(2 * HEADS):
        st_ref[i] = st[i]


def _gdn(qkv, h, consts, first_f, first_b, rows):
    t = qkv.shape[0]
    n = t // rows
    fwd = lambda w, colblk: pl.BlockSpec((rows, w), lambda i, a, b: (i, colblk))
    bwd = lambda w, colblk: pl.BlockSpec((rows, w), lambda i, a, b: (n - 1 - i, colblk))
    gcol = COL_GB // LANES
    out = jax.ShapeDtypeStruct((t, WIDTH), BRANCH_DTYPE)
    return pl.pallas_call(
        _gdn_body,
        grid_spec=pltpu.PrefetchScalarGridSpec(
            num_scalar_prefetch=2,
            grid=(n,),
            in_specs=[fwd(WIDTH, 0), fwd(WIDTH, 1), fwd(WIDTH, 2), fwd(LANES, gcol),
                      bwd(WIDTH, 0), bwd(WIDTH, 1), bwd(WIDTH, 2), bwd(LANES, gcol),
                      pl.BlockSpec((8, LANES), lambda i, a, b: (0, 0))],
            out_specs=[fwd(WIDTH, 0), bwd(WIDTH, 0)],
            scratch_shapes=[pltpu.VMEM((2 * HEADS, HEAD_DIM, HEAD_DIM), _f32)]),
        out_shape=[out, out],
        compiler_params=_params(("arbitrary",)),
        name="gdn",
    )(first_f, first_b, qkv, qkv, qkv, h, qkv, qkv, qkv, h, consts)


def _rope(x, cos, sin_signed):
    src = _iota2((LANES, LANES), 0)
    dst = _iota2((LANES, LANES), 1)
    half = B_HEAD_DIM // 2
    partner = jnp.where(dst % B_HEAD_DIM < half, dst + half, dst - half)
    rot = jnp.dot(x.astype(_bf16), (src == partner).astype(_bf16), preferred_element_type=_f32)
    return x * cos + rot * sin_signed


def _swa_body(first_ref, last_ref, pos_ref, q_ref, kp_ref, kc_ref, kn_ref, vp_ref, vc_ref, vn_ref,
              cp_ref, sp_ref, cc_ref, sc_ref, cn_ref, sn_ref, sink_ref, o_ref):
    i = pl.program_id(0)
    blk = B_BLOCK
    nb = q_ref.shape[0] // blk
    has_prev = first_ref[i] == 0
    has_next = last_ref[i] == 0
    k_all = jnp.concatenate([_rope(kp_ref[...], cp_ref[...], sp_ref[...]),
                             _rope(kc_ref[...], cc_ref[...], sc_ref[...]),
                             _rope(kn_ref[...], cn_ref[...], sn_ref[...])], axis=0)
    v_all = jnp.concatenate([vp_ref[...], vc_ref[...], vn_ref[...]], axis=0)
    nq = B_GROUP * blk
    kpos = _iota2((3 * blk, nq), 0)
    qpos = _iota2((3 * blk, nq), 1) % blk
    rel = kpos - qpos
    in_window = (rel >= 0) & (rel <= 2 * WINDOW)
    cos, sin_signed = cc_ref[...], sc_ref[...]
    scale = B_HEAD_DIM ** -0.5
    qr = [_rope(q_ref[:, p * LANES:(p + 1) * LANES], cos, sin_signed) * scale for p in range(B_Q_HEADS // 2)]
    sink_row = sink_ref[0:1, :]
    chains = [(b, kv) for b in range(nb) for kv in range(B_KV_HEADS)]
    q4, sink, kh, vh, ok = [], [], [], [], []
    for b, kv in chains:
        rows = slice(b * blk, (b + 1) * blk)
        heads = [qr[hq // 2][rows, (hq % 2) * B_HEAD_DIM:(hq % 2 + 1) * B_HEAD_DIM]
                 for hq in range(kv * B_GROUP, (kv + 1) * B_GROUP)]
        q4.append(jnp.concatenate(heads, axis=0))
        sink.append(jnp.concatenate([jnp.broadcast_to(sink_row[:, kv * B_GROUP + g:kv * B_GROUP + g + 1], (1, blk))
                                     for g in range(B_GROUP)], axis=1))
        kh.append(k_all[b * blk:(b + 3) * blk, kv * B_HEAD_DIM:(kv + 1) * B_HEAD_DIM])
        vh.append(v_all[b * blk:(b + 3) * blk, kv * B_HEAD_DIM:(kv + 1) * B_HEAD_DIM])
        m_ok = in_window
        if b == 0:
            m_ok = m_ok & (has_prev | (kpos >= blk))
        if b == nb - 1:
            m_ok = m_ok & (has_next | (kpos < 2 * blk))
        ok.append(m_ok)
    cs = range(len(chains))
    s = [jnp.where(ok[c], _dot_nt(kh[c], q4[c]), -jnp.inf) for c in cs]
    m = [jnp.maximum(jnp.max(s[c], axis=0, keepdims=True), sink[c]) for c in cs]
    p = [jnp.exp(s[c] - m[c]) for c in cs]
    denom = [jnp.sum(p[c], axis=0, keepdims=True) + jnp.exp(sink[c] - m[c]) for c in cs]
    o_t = [_dot_tn(vh[c], p[c]) / denom[c] for c in cs]
    for c, (b, kv) in enumerate(chains):
        for g in range(0, B_GROUP, 2):
            hq = kv * B_GROUP + g
            pair = jnp.concatenate([o_t[c][:, g * blk:(g + 1) * blk],
                                    o_t[c][:, (g + 1) * blk:(g + 2) * blk]], axis=0)
            o_ref[b * blk:(b + 1) * blk, hq * B_HEAD_DIM:(hq + 2) * B_HEAD_DIM] = (
                jnp.transpose(pair).astype(o_ref.dtype))


def _swa(h, cos_t, sin_t, sink, first, last, pos, nb):
    t = h.shape[0]
    rows = nb * B_BLOCK
    n = t // rows
    n128 = t // B_BLOCK
    npos = cos_t.shape[0] // B_BLOCK
    kcol, vcol = COL_B_K // LANES, COL_B_V // LANES
    before = lambda colblk: pl.BlockSpec((B_BLOCK, LANES), lambda i, a, b, p: (jnp.maximum(i * nb - 1, 0), colblk))
    own = lambda colblk: pl.BlockSpec((rows, LANES), lambda i, a, b, p: (i, colblk))
    after = lambda colblk: pl.BlockSpec((B_BLOCK, LANES),
                                        lambda i, a, b, p: (jnp.minimum((i + 1) * nb, n128 - 1), colblk))
    t_before = pl.BlockSpec((B_BLOCK, LANES), lambda i, a, b, p: (jnp.maximum(p[i] * nb - 1, 0), 0))
    t_own = pl.BlockSpec((rows, LANES), lambda i, a, b, p: (p[i], 0))
    t_after = pl.BlockSpec((B_BLOCK, LANES), lambda i, a, b, p: (jnp.minimum((p[i] + 1) * nb, npos - 1), 0))
    qw = B_Q_HEADS * B_HEAD_DIM
    return pl.pallas_call(
        _swa_body,
        grid_spec=pltpu.PrefetchScalarGridSpec(
            num_scalar_prefetch=3,
            grid=(n,),
            in_specs=[pl.BlockSpec((rows, qw), lambda i, a, b, p: (i, COL_B_Q // qw)),
                      before(kcol), own(kcol), after(kcol), before(vcol), own(vcol), after(vcol),
                      t_before, t_before, t_own, t_own, t_after, t_after,
                      pl.BlockSpec((8, LANES), lambda i, a, b, p: (0, 0))],
            out_specs=pl.BlockSpec((rows, qw), lambda i, a, b, p: (i, 0))),
        out_shape=jax.ShapeDtypeStruct((t, qw), BRANCH_DTYPE),
        compiler_params=_params(("parallel",)),
        name="window_attn",
    )(first, last, pos, h, h, h, h, h, h, h, cos_t, sin_t, cos_t, sin_t, cos_t, sin_t, sink)


def _residual_ln(x, y, g, b):
    z = DN_ALPHA * x + y
    mu = jnp.mean(z, axis=1, keepdims=True)
    zc = z - mu
    var = jnp.mean(zc * zc, axis=1, keepdims=True)
    return zc * lax.rsqrt(var + 1e-5) * g + b


def _gated_rms(o, gate, g):
    outs = []
    for h in range(HEADS):
        sl = slice(h * HEAD_DIM, (h + 1) * HEAD_DIM)
        x = o[:, sl]
        x = x * lax.rsqrt(jnp.mean(x * x, axis=1, keepdims=True) + 1e-6) * g
        outs.append(x * _silu(gate[:, sl]))
    return jnp.concatenate(outs, axis=1)


def _merge_body(x_ref, af_ref, ab_ref, ag_ref, ob_ref, cf_ref, cb_ref, cg_ref, ma_ref, mb_ref, mc_ref,
                wa_ref, wb_ref, wc_ref, wo_ref, nrm_ref, ln_ref, o_ref):
    up = lambda ref: ref[...].astype(_f32)
    oa = _gated_rms(up(af_ref) + up(ab_ref), ag_ref[...], nrm_ref[0:1, :])
    oc = _gated_rms(up(cf_ref) + up(cb_ref), cg_ref[...], nrm_ref[1:2, :])
    pa = jnp.dot(oa.astype(_bf16), wa_ref[...], preferred_element_type=_f32)
    pb = jnp.dot(ob_ref[...].astype(_bf16), wb_ref[...], preferred_element_type=_f32)
    pc = jnp.dot(oc.astype(_bf16), wc_ref[...], preferred_element_type=_f32)
    mix = (jax.nn.sigmoid(ma_ref[...]) * pa + jax.nn.sigmoid(mb_ref[...]) * pb
           + jax.nn.sigmoid(mc_ref[...]) * pc)
    y = jnp.dot(mix.astype(_bf16), wo_ref[...], preferred_element_type=_f32)
    o_ref[...] = _residual_ln(x_ref[...], y, ln_ref[0:1, :], ln_ref[1:2, :])


def _merge(x, ha, hb, oa_f, oa_b, ob, oc_f, oc_b, wa, wb, wc, wo, nrm, ln, tm):
    t = x.shape[0]
    row = lambda w, colblk=0: pl.BlockSpec((tm, w), lambda i: (i, colblk))
    full = lambda a: pl.BlockSpec(a.shape, lambda i: (0, 0))
    mcol = COL_MERGE // D_MODEL
    return pl.pallas_call(
        _merge_body,
        grid=(t // tm,),
        in_specs=[row(D_MODEL), row(WIDTH), row(WIDTH), row(WIDTH, COL_A_G // WIDTH), row(WIDTH),
                  row(WIDTH), row(WIDTH), row(WIDTH, COL_C_GATE // WIDTH),
                  row(D_MODEL, mcol), row(D_MODEL, mcol + 1), row(D_MODEL, mcol + 2),
                  full(wa), full(wb), full(wc), full(wo), full(nrm), full(ln)],
        out_specs=row(D_MODEL),
        out_shape=jax.ShapeDtypeStruct((t, D_MODEL), _f32),
        compiler_params=_params(("parallel",)),
        name="merge_mix",
    )(x, oa_f, oa_b, ha, ob, oc_f, oc_b, ha, hb, hb, hb, wa, wb, wc, wo, nrm, ln)


def _xattn_body(seq_ref, x_ref, kv_ref, wq_ref, wo_ref, ln_ref, o_ref):
    x = x_ref[...]
    q = jnp.dot(x.astype(_bf16), wq_ref[...], preferred_element_type=_f32)
    hs = range(MEM_HEADS)
    cols = [slice(h * MEM_HEAD_DIM, (h + 1) * MEM_HEAD_DIM) for h in hs]
    s = [_dot_nt(q[:, cols[h]], kv_ref[0, :, cols[h]]) * (MEM_HEAD_DIM ** -0.5) for h in hs]
    p = [jnp.exp(s[h] - jnp.max(s[h], axis=1, keepdims=True)) for h in hs]
    p = [p[h] / jnp.sum(p[h], axis=1, keepdims=True) for h in hs]
    o = jnp.concatenate([_dot(p[h], kv_ref[0, :, D_MODEL + h * MEM_HEAD_DIM:D_MODEL + (h + 1) * MEM_HEAD_DIM])
                         for h in hs], axis=1)
    y = jnp.dot(o.astype(_bf16), wo_ref[...], preferred_element_type=_f32)
    o_ref[...] = _residual_ln(x, y, ln_ref[0:1, :], ln_ref[1:2, :])


def _xattn(x, kv, wq, wo, ln, seq_of_tile, tm):
    t = x.shape[0]
    full = lambda a: pl.BlockSpec(a.shape, lambda i, s: (0, 0))
    return pl.pallas_call(
        _xattn_body,
        grid_spec=pltpu.PrefetchScalarGridSpec(
            num_scalar_prefetch=1,
            grid=(t // tm,),
            in_specs=[pl.BlockSpec((tm, D_MODEL), lambda i, s: (i, 0)),
                      pl.BlockSpec((1,) + kv.shape[1:], lambda i, s: (s[i], 0, 0)),
                      full(wq), full(wo), full(ln)],
            out_specs=pl.BlockSpec((tm, D_MODEL), lambda i, s: (i, 0))),
        out_shape=jax.ShapeDtypeStruct((t, D_MODEL), _f32),
        compiler_params=_params(("parallel",)),
        name="mem_xattn",
    )(seq_of_tile, x, kv, wq, wo, ln)


def _ffn_body(x_ref, w1_ref, w2_ref, ln_ref, o_ref):
    x = x_ref[...]
    xb = x.astype(_bf16)
    gate = jnp.dot(xb, w1_ref[0], preferred_element_type=_f32)
    up = jnp.dot(xb, w1_ref[1], preferred_element_type=_f32)
    y = jnp.dot((_silu(gate) * up).astype(_bf16), w2_ref[...], preferred_element_type=_f32)
    o_ref[...] = _residual_ln(x, y, ln_ref[0:1, :], ln_ref[1:2, :])


def _ffn(x, w1, w2, ln, tm):
    t = x.shape[0]
    return pl.pallas_call(
        _ffn_body,
        grid=(t // tm,),
        in_specs=[pl.BlockSpec((tm, D_MODEL), lambda i: (i, 0)),
                  _resident(w1.shape), _resident(w2.shape), _resident(ln.shape)],
        out_specs=pl.BlockSpec((tm, D_MODEL), lambda i: (i, 0)),
        out_shape=jax.ShapeDtypeStruct((t, D_MODEL), _f32),
        compiler_params=_params(("parallel",)),
        name="swiglu_ffn",
    )(x, w1, w2, ln)


def _boundary_tables(seq_lens, tile):
    first, last, pos, seq = [], [], [], []
    for sid, length in enumerate(seq_lens):
        n = length // tile
        for b in range(n):
            first.append(int(b == 0))
            last.append(int(b == n - 1))
            pos.append(b)
            seq.append(sid)
    as_i32 = lambda v: jnp.asarray(np.asarray(v, np.int32))
    return as_i32(first), as_i32(last), as_i32(pos), as_i32(seq)


def _split_in_cols(w):
    pad = jnp.zeros(w.shape[:-1] + (HA_COLS - 3968,), w.dtype)
    wa = jnp.concatenate([w[..., 0:3072], w[..., 4864:5376], w[..., 3072:3200], w[..., 3200:3328],
                          w[..., 5376:5392], pad], axis=-1)
    wb = jnp.concatenate([w[..., 5392:8464], w[..., 3328:4864]], axis=-1)
    return wa, wb


def _rows8(*rows):
    n = rows[0].shape[-1]
    out = jnp.zeros((8, n), _f32)
    for r, v in enumerate(rows):
        out = out.at[r].set(v.astype(_f32))
    return out


def _lane_pad(v, offset):
    return jnp.zeros((LANES,), _f32).at[offset:offset + v.shape[0]].set(v.astype(_f32))


def kernel(x_prompt, x_sample, mem_prompt, mem_sample, w_in, hgrn_lb_logits, hgrn_norm_g, attn_sink,
           gdn_conv_w, gdn_a_log, gdn_dt_bias, gdn_norm_g, w_branch_a, w_branch_b, w_branch_c, w_mix_out,
           w_mem_q, w_mem_kv, w_mem_o, w_ffn_in, w_ffn_out, ln_g, ln_b):
    depth = w_in.shape[0]
    d = x_prompt.shape[-1]
    seq_lens = (x_prompt.shape[1],) * x_prompt.shape[0] + (x_sample.shape[1],) * x_sample.shape[0]
    n_prompt = x_prompt.shape[0] * x_prompt.shape[1]
    x = jnp.concatenate([x_prompt.reshape(-1, d), x_sample.reshape(-1, d)], axis=0)
    mem = jnp.concatenate([mem_prompt, mem_sample], axis=0)
    n_seq, n_mem, _ = mem.shape
    t = x.shape[0]

    tm = math.gcd(512, *seq_lens)
    tm_proj = math.gcd(512, t)
    rows_a = math.gcd(HGRN_CHUNKS_PER_STEP * CHUNK, *seq_lens)
    first_a, last_a, _, _ = _boundary_tables(seq_lens, rows_a)
    last_a_rev = last_a[::-1]
    rows_g = math.gcd(GDN_CHUNKS_PER_STEP * CHUNK, *seq_lens)
    first_g, last_g, _, _ = _boundary_tables(seq_lens, rows_g)
    last_g_rev = last_g[::-1]
    nb_swa = math.gcd(SWA_BLOCKS_PER_STEP * B_BLOCK, *seq_lens) // B_BLOCK
    first_w, last_w, pos_w, _ = _boundary_tables(seq_lens, nb_swa * B_BLOCK)
    first_t, last_t, _, seq_t = _boundary_tables(seq_lens, tm)

    s_max = max(seq_lens)
    inv = ROPE_THETA ** (-jnp.arange(0, B_HEAD_DIM, 2, dtype=_f32) / B_HEAD_DIM)
    ang = jnp.arange(s_max, dtype=_f32)[:, None] * inv[None, :]
    cos_t = jnp.tile(jnp.cos(ang), (1, 4))
    sin_t = jnp.tile(jnp.concatenate([-jnp.sin(ang), jnp.sin(ang)], axis=1), (1, 2))

    cum = jnp.cumsum(jax.nn.softmax(hgrn_lb_logits.astype(_f32), axis=1), axis=1)
    lb = cum - cum[:, :1]

    w_in_a, w_in_b = (w.astype(_bf16) for w in _split_in_cols(w_in))
    bf = lambda w: w.astype(_bf16)
    w_a, w_b, w_c, w_mix = bf(w_branch_a), bf(w_branch_b), bf(w_branch_c), bf(w_mix_out)
    w_q, w_kv, w_o = bf(w_mem_q), bf(w_mem_kv), bf(w_mem_o)
    w_f1, w_f2 = bf(w_ffn_in), bf(w_ffn_out)
    mem2 = mem.reshape(n_seq * n_mem, d)

    for l in range(depth):
        lb_rows = []
        for dirn in range(2):
            lbd = lb[dirn, l]
            lb_rows += [jnp.log(lbd), jnp.log1p(-lbd)]
        ha = _in_proj(_in_proj_a_body, "in_proj_a", x, w_in_a[l], [_rows8(*lb_rows)], tm_proj)
        hb = _in_proj(_in_proj_b_body, "in_proj_b", x, w_in_b[l], [], tm_proj)
        oa_f, oa_b = _hgrn(ha, first_a, last_a_rev, rows_a)
        o_b = _swa(ha, cos_t, sin_t, _rows8(_lane_pad(attn_sink[l], 0)), first_w, last_w, pos_w, nb_swa)
        qkv = _gdn_prep(hb, _rows8(*[gdn_conv_w[l, j] for j in range(CONV_K)]), first_t, last_t, tm)
        gconst = _rows8(_lane_pad(-jnp.exp(gdn_a_log[l].astype(_f32)).reshape(-1), 2 * HEADS),
                        _lane_pad(gdn_dt_bias[l].reshape(-1), 2 * HEADS))
        oc_f, oc_b = _gdn(qkv, ha, gconst, first_g, last_g_rev, rows_g)
        nrm = _rows8(hgrn_norm_g[l], gdn_norm_g[l])
        x = _merge(x, ha, hb, oa_f, oa_b, o_b, oc_f, oc_b, w_a[l], w_b[l], w_c[l], w_mix[l], nrm,
                   _rows8(ln_g[l, 0], ln_b[l, 0]), tm)
        kv = _matmul(mem2, _col_tiles(w_kv[l], 512), n_mem).reshape(n_seq, n_mem, 2 * d)
        x = _xattn(x, kv, w_q[l], w_o[l], _rows8(ln_g[l, 1], ln_b[l, 1]), seq_t, tm)
        x = _ffn(x, _col_tiles(w_f1[l], FFN_HIDDEN), w_f2[l], _rows8(ln_g[l, 2], ln_b[l, 2]), tm)

    y_prompt = x[:n_prompt].reshape(x_prompt.shape)
    y_sample = x[n_prompt:].reshape(x_sample.shape)
    return (y_prompt, y_sample)
```

```python
import functools
import math

import numpy as np
import jax
import jax.numpy as jnp
from jax import lax
from jax.experimental import pallas as pl
from jax.experimental.pallas import tpu as pltpu

D_MODEL = 1024
DEPTH = 4
HEADS = 4
HEAD_DIM = 128
WIDTH = HEADS * HEAD_DIM
B_Q_HEADS = 8
B_KV_HEADS = 2
B_GROUP = B_Q_HEADS // B_KV_HEADS
B_HEAD_DIM = 64
WINDOW = 128
B_BLOCK = 128
ROPE_THETA = 10000.0
CONV_K = 5
MEM_HEADS = 4
MEM_HEAD_DIM = D_MODEL // MEM_HEADS
FFN_HIDDEN = 2816
DN_ALPHA = (2 * DEPTH) ** 0.25

CHUNK = 64
GDN_CHUNKS_PER_STEP = 4
HGRN_CHUNKS_PER_STEP = 8
SWA_BLOCKS_PER_STEP = 8
LANES = 128
LOG2_E = math.log2(math.e)
VMEM_LIMIT = 56 * 1024 * 1024

COL_A_Q, COL_A_FF, COL_A_FB, COL_A_I, COL_A_G = 0, 512, 1024, 1536, 2048
COL_B_Q = 2560
COL_C_GATE = 3072
COL_B_K, COL_B_V = 3584, 3712
COL_GB = 3840
HA_COLS = 4096
COL_MERGE = 0
COL_C_QKV = 3072
HB_COLS = 4608

_f32 = jnp.float32
_bf16 = jnp.bfloat16
BRANCH_DTYPE = _bf16


def _dot(a, b):
    return jnp.dot(a.astype(_bf16), b.astype(_bf16), preferred_element_type=_f32)


def _dot_nt(a, b):
    return lax.dot_general(a.astype(_bf16), b.astype(_bf16), (((1,), (1,)), ((), ())),
                           preferred_element_type=_f32)


def _dot_tn(a, b):
    return lax.dot_general(a.astype(_bf16), b.astype(_bf16), (((0,), (0,)), ((), ())),
                           preferred_element_type=_f32)


def _split3(x):
    hi = x.astype(_bf16)
    r1 = x - hi.astype(_f32)
    mid = r1.astype(_bf16)
    lo = (r1 - mid.astype(_f32)).astype(_bf16)
    return hi, mid, lo


def _exact_dot(sel, parts):
    dot = lambda p: jnp.dot(sel, p, preferred_element_type=_f32)
    return dot(parts[0]) + dot(parts[1]) + dot(parts[2])


def _silu(x):
    return x * jax.nn.sigmoid(x)


def _iota2(shape, dim):
    return lax.broadcasted_iota(jnp.int32, shape, dim)


def _params(sem):
    return pltpu.CompilerParams(dimension_semantics=sem, vmem_limit_bytes=VMEM_LIMIT)


def _resident(shape):
    return pl.BlockSpec(shape, lambda *_: (0,) * len(shape), pipeline_mode=pl.Buffered(1))


def _col_tiles(w, tn):
    k, n = w.shape
    return w.reshape(k, n // tn, tn).transpose(1, 0, 2)


def _matmul_body(x_ref, w_ref, o_ref, xb_ref):
    j = pl.program_id(1)

    @pl.when(j == 0)
    def _():
        xb_ref[...] = x_ref[...].astype(_bf16)

    o_ref[...] = jnp.dot(xb_ref[...], w_ref[j], preferred_element_type=_f32)


def _matmul(x, w_tiles, tm):
    t, k = x.shape
    nt, _, tn = w_tiles.shape
    n = nt * tn
    return pl.pallas_call(
        _matmul_body,
        grid=(t // tm, nt),
        in_specs=[pl.BlockSpec((tm, k), lambda i, j: (i, 0)), _resident(w_tiles.shape)],
        out_specs=pl.BlockSpec((tm, tn), lambda i, j: (i, j)),
        out_shape=jax.ShapeDtypeStruct((t, n), _f32),
        scratch_shapes=[pltpu.VMEM((tm, k), _bf16)],
        compiler_params=_params(("parallel", "arbitrary")),
        name="dense_proj",
    )(x, w_tiles)


def _in_proj_a_body(x_ref, w_ref, lb_ref, o_ref):
    xb = x_ref[...].astype(_bf16)
    proj = lambda lo, hi: jnp.dot(xb, w_ref[:, lo:hi], preferred_element_type=_f32)

    def log_f(d):
        def fn(z):
            log_sig = jnp.minimum(z, 0.0) - jnp.log(1.0 + jnp.exp(-jnp.abs(z)))
            log_lb = lb_ref[2 * d:2 * d + 1, :]
            b = lb_ref[2 * d + 1:2 * d + 2, :] + log_sig
            return jnp.maximum(log_lb, b) + jnp.log(1.0 + jnp.exp(-jnp.abs(log_lb - b)))
        return fn

    groups = [(COL_A_Q, COL_A_Q + WIDTH, lambda z: _silu(z) * (HEAD_DIM ** -0.5)),
              (COL_A_FF, COL_A_FF + WIDTH, log_f(0)),
              (COL_A_FB, COL_A_FB + WIDTH, log_f(1)),
              (COL_A_I, o_ref.shape[1], lambda z: z)]
    z_next = proj(groups[0][0], groups[0][1])
    for g, (lo, hi, fn) in enumerate(groups):
        z = z_next
        if g + 1 < len(groups):
            z_next = proj(groups[g + 1][0], groups[g + 1][1])
        o_ref[:, lo:hi] = fn(z)


def _in_proj_b_body(x_ref, w_ref, o_ref):
    o_ref[...] = jnp.dot(x_ref[...].astype(_bf16), w_ref[...], preferred_element_type=_f32)


def _in_proj(body, name, x, w, consts, tm):
    t, k = x.shape
    n = w.shape[1]
    return pl.pallas_call(
        body,
        grid=(t // tm,),
        in_specs=[pl.BlockSpec((tm, k), lambda i: (i, 0)), _resident(w.shape)] + [_resident(c.shape) for c in consts],
        out_specs=pl.BlockSpec((tm, n), lambda i: (i, 0)),
        out_shape=jax.ShapeDtypeStruct((t, n), _f32),
        compiler_params=_params(("parallel",)),
        name=name,
    )(x, w, *consts)


def _chunk_masks(c, reverse):
    row = _iota2((c, c), 0)
    col = _iota2((c, c), 1)
    if reverse:
        return col >= row, col > row
    return col <= row, col < row


def _level_mask(c, blk, reverse, reps=1):
    row = _iota2((c, reps * c), 0)
    col = _iota2((c, reps * c), 1) % c
    half = blk // 2
    same = (row // blk) == (col // blk)
    r_hi = (row % blk) >= half
    c_hi = (col % blk) >= half
    if reverse:
        return same & jnp.logical_not(r_hi) & c_hi
    return same & r_hi & jnp.logical_not(c_hi)


def _block_ref_rows(g, blk, reverse):
    c, n = g.shape
    idx = blk // 2 if reverse else blk // 2 - 1
    rows = max(blk, 8)
    g3 = g.reshape(c // rows, rows, n)
    pick = lambda r: jnp.broadcast_to(g3[:, r:r + 1, :], g3.shape)
    out = pick(idx)
    if blk < rows:
        sub = lax.broadcasted_iota(jnp.int32, g3.shape, 1)
        for b in range(1, rows // blk):
            out = jnp.where(sub >= b * blk, pick(b * blk + idx), out)
    return out.reshape(c, n)


def _levels(c):
    out, blk = [], c
    while blk >= 2:
        out.append(blk)
        blk //= 2
    return out


def _hgrn_body(first_f_ref, first_b_ref, q_f, lf_f, v_f, q_b, lf_b, v_b, o_f, o_b, st_ref):
    c = CHUNK
    step = pl.program_id(0)

    @pl.when(first_f_ref[step] == 1)
    def _():
        st_ref[0:HEADS] = jnp.zeros((HEADS, HEAD_DIM, HEAD_DIM), _f32)

    @pl.when(first_b_ref[step] == 1)
    def _():
        st_ref[HEADS:2 * HEADS] = jnp.zeros((HEADS, HEAD_DIM, HEAD_DIM), _f32)

    nsub = q_f.shape[0] // c
    row = _iota2((c, c), 0)
    col = _iota2((c, c), 1)
    eye = row == col
    blks = _levels(c)
    units = []
    for d, (q_ref, lf_ref, v_ref) in enumerate(((q_f, lf_f, v_f), (q_b, lf_b, v_b))):
        reverse = d == 1
        incl, _ = _chunk_masks(c, reverse)
        tri = incl.astype(_bf16)
        masks = [_level_mask(c, blk, reverse) for blk in blks]
        log_f = lf_ref[...]
        key_all = 1.0 - jnp.exp(log_f)
        rpos = _iota2((c, HEAD_DIM), 0)
        later = [((rpos % blk) < blk // 2) if reverse else ((rpos % blk) >= blk // 2) for blk in blks]
        sgn = [jnp.where(m, LOG2_E, -LOG2_E) for m in later]
        q_all = q_ref[...]
        v_all = v_ref[...]
        tot_row = 0 if reverse else c - 1
        for s in (range(nsub - 1, -1, -1) if reverse else range(nsub)):
            rs = slice(s * c, (s + 1) * c)
            gc_all = _exact_dot(tri, _split3(log_f[rs]))
            ref_all = [_block_ref_rows(gc_all, blk, reverse) for blk in blks]
            for h in range(HEADS):
                sl = slice(h * HEAD_DIM, (h + 1) * HEAD_DIM)
                units.append(dict(q=q_all[rs, sl], k=key_all[rs, sl], v=v_all[rs, sl], gc=gc_all[:, sl],
                                  refs=[r[:, sl] for r in ref_all], masks=masks, sgn=sgn, rows=rs,
                                  g_tot=gc_all[tot_row:tot_row + 1, sl]))

    scores = [jnp.where(eye, _dot_nt(u["q"], u["k"]), 0.0) for u in units]
    for l in range(len(blks)):
        e = [jnp.exp2((u["gc"] - u["refs"][l]) * u["sgn"][l]) for u in units]
        scores = [jnp.where(u["masks"][l], _dot_nt(u["q"] * e[i], u["k"] * e[i]), scores[i])
                  for i, u in enumerate(units)]
    intra = [_dot(scores[i], u["v"]) for i, u in enumerate(units)]
    qd = [u["q"] * jnp.exp(u["gc"]) for u in units]
    kd = [u["k"] * jnp.exp(u["g_tot"] - u["gc"]) for u in units]

    st = [st_ref[i] for i in range(2 * HEADS)]
    for slot in range(nsub):
        for d in range(2):
            for h in range(HEADS):
                i, j = (d * nsub + slot) * HEADS + h, d * HEADS + h
                out = intra[i] + _dot_nt(qd[i], st[j])
                st[j] = st[j] * jnp.exp(units[i]["g_tot"]) + _dot_tn(units[i]["v"], kd[i])
                (o_f, o_b)[d][units[i]["rows"], h * HEAD_DIM:(h + 1) * HEAD_DIM] = out.astype(o_f.dtype)
    for i in range(2 * HEADS):
        st_ref[i] = st[i]


def _hgrn(h, first_f, first_b, rows):
    t = h.shape[0]
    n = t // rows
    fwd = lambda col: pl.BlockSpec((rows, WIDTH), lambda i, a, b: (i, col // WIDTH))
    bwd = lambda col: pl.BlockSpec((rows, WIDTH), lambda i, a, b: (n - 1 - i, col // WIDTH))
    out = jax.ShapeDtypeStruct((t, WIDTH), BRANCH_DTYPE)
    return pl.pallas_call(
        _hgrn_body,
        grid_spec=pltpu.PrefetchScalarGridSpec(
            num_scalar_prefetch=2,
            grid=(n,),
            in_specs=[fwd(COL_A_Q), fwd(COL_A_FF), fwd(COL_A_I), bwd(COL_A_Q), bwd(COL_A_FB), bwd(COL_A_I)],
            out_specs=[fwd(0), bwd(0)],
            scratch_shapes=[pltpu.VMEM((2 * HEADS, HEAD_DIM, HEAD_DIM), _f32)]),
        out_shape=[out, out],
        compiler_params=_params(("arbitrary",)),
        name="hgrn",
    )(first_f, first_b, h, h, h, h, h, h)


def _gdn_prep_body(first_ref, last_ref, prev_ref, cur_ref, next_ref, w_ref, o_ref, ext_ref):
    i = pl.program_id(0)
    tm = cur_ref.shape[0]
    halo = CONV_K // 2
    ext_ref[0:8, :] = jnp.where(first_ref[i] == 1, 0.0, prev_ref[...])
    ext_ref[8:8 + tm, :] = cur_ref[...]
    ext_ref[8 + tm:16 + tm, :] = jnp.where(last_ref[i] == 1, 0.0, next_ref[...])
    acc = cur_ref[...] * w_ref[halo:halo + 1, :]
    for j in range(CONV_K):
        if j != halo:
            off = 8 - halo + j
            acc = acc + ext_ref[off:off + tm, :] * w_ref[j:j + 1, :]
    y = _silu(acc)
    for part in range(3):
        for h in range(HEADS):
            lo = part * WIDTH + h * HEAD_DIM
            x = y[:, lo:lo + HEAD_DIM]
            if part < 2:
                x = x * lax.rsqrt(jnp.sum(x * x, axis=1, keepdims=True) + 1e-6)
            if part == 0:
                x = x * (HEAD_DIM ** -0.5)
            o_ref[:, lo:lo + HEAD_DIM] = x


def _gdn_prep(h, conv_w, first, last, tm):
    t = h.shape[0]
    n = t // tm
    r8 = tm // 8
    cb = COL_C_QKV // (3 * WIDTH)
    return pl.pallas_call(
        _gdn_prep_body,
        grid_spec=pltpu.PrefetchScalarGridSpec(
            num_scalar_prefetch=2,
            grid=(n,),
            in_specs=[pl.BlockSpec((8, 3 * WIDTH), lambda i, f, l: (jnp.maximum(i * r8 - 1, 0), cb)),
                      pl.BlockSpec((tm, 3 * WIDTH), lambda i, f, l: (i, cb)),
                      pl.BlockSpec((8, 3 * WIDTH), lambda i, f, l: (jnp.minimum((i + 1) * r8, n * r8 - 1), cb)),
                      pl.BlockSpec((8, 3 * WIDTH), lambda i, f, l: (0, 0))],
            out_specs=pl.BlockSpec((tm, 3 * WIDTH), lambda i, f, l: (i, 0)),
            scratch_shapes=[pltpu.VMEM((tm + 16, 3 * WIDTH), _f32)]),
        out_shape=jax.ShapeDtypeStruct((t, 3 * WIDTH), _f32),
        compiler_params=_params(("parallel",)),
        name="gdn_prep",
    )(first, last, h, h, h, conv_w)


def _gdn_body(first_f_ref, first_b_ref, q_f, k_f, v_f, gb_f, q_b, k_b, v_b, gb_b, cst_ref, o_f, o_b, st_ref):
    c = CHUNK
    step = pl.program_id(0)

    @pl.when(first_f_ref[step] == 1)
    def _():
        st_ref[0:HEADS] = jnp.zeros((HEADS, HEAD_DIM, HEAD_DIM), _f32)

    @pl.when(first_b_ref[step] == 1)
    def _():
        st_ref[HEADS:2 * HEADS] = jnp.zeros((HEADS, HEAD_DIM, HEAD_DIM), _f32)

    nsub = q_f.shape[0] // c
    row = _iota2((c, c), 0)
    col = _iota2((c, c), 1)
    eye = (row == col).astype(_f32)
    neg_a = cst_ref[0:1, :]
    dt_bias = cst_ref[1:2, :]
    sizes = _levels(c)[::-1]
    units = []
    for d, (q_ref, k_ref, v_ref, gb_ref) in enumerate(((q_f, k_f, v_f, gb_f), (q_b, k_b, v_b, gb_b))):
        reverse = d == 1
        incl, strict = _chunk_masks(c, reverse)
        tri = incl.astype(_bf16)
        masks = [_level_mask(c, blk, reverse) for blk in sizes]
        gb = gb_ref[...]
        beta_t = jax.nn.sigmoid(gb)
        zz = gb + dt_bias
        g_in = neg_a * (jnp.maximum(zz, 0.0) + jnp.log1p(jnp.exp(-jnp.abs(zz))))
        tot_row = 0 if reverse else c - 1
        for s in (range(nsub - 1, -1, -1) if reverse else range(nsub)):
            rs = slice(s * c, (s + 1) * c)
            gc_t = _exact_dot(tri, _split3(g_in[rs]))
            gc_tt = jnp.transpose(gc_t)
            for h in range(HEADS):
                sl = slice(h * HEAD_DIM, (h + 1) * HEAD_DIM)
                j = 2 * HEADS + d * HEADS + h
                g_col = gc_t[:, j:j + 1]
                decay = jnp.where(incl, jnp.exp(jnp.minimum(g_col - gc_tt[j:j + 1, :], 0.0)), 0.0)
                units.append(dict(q=q_ref[rs, sl], k=k_ref[rs, sl], v=v_ref[rs, sl], rows=rs,
                                  g_col=g_col, decay=decay, strict=strict, masks=masks,
                                  beta=beta_t[rs, d * HEADS + h:d * HEADS + h + 1],
                                  g_tot=gc_t[tot_row:tot_row + 1, j:j + 1]))

    k_b = [u["k"].astype(_bf16) for u in units]
    a = [jnp.where(u["strict"], _dot_nt(k_b[i], k_b[i]) * u["decay"] * u["beta"], 0.0) for i, u in enumerate(units)]
    qk = [_dot_nt(u["q"], k_b[i]) * u["decay"] for i, u in enumerate(units)]
    inv = [eye - jnp.where(u["masks"][0], a[i], 0.0) for i, u in enumerate(units)]
    for l in range(1, len(sizes)):
        inv_b = [m.astype(_bf16) for m in inv]
        t1 = [_dot(jnp.where(u["masks"][l], a[i], 0.0), inv_b[i]) for i, u in enumerate(units)]
        inv = [inv[i] - _dot(inv_b[i], t1[i]) for i in range(len(units))]
    inv_b = [m.astype(_bf16) for m in inv]
    e_g = [jnp.exp(u["g_col"]) for u in units]
    sol = [_dot(inv_b[i], jnp.concatenate([u["v"] * u["beta"], u["k"] * (u["beta"] * e_g[i])], axis=1))
           for i, u in enumerate(units)]
    uu = [m[:, :HEAD_DIM] for m in sol]
    ww = [m[:, HEAD_DIM:] for m in sol]
    qd = [u["q"] * e_g[i] for i, u in enumerate(units)]
    kd = [u["k"] * jnp.exp(u["g_tot"] - u["g_col"]) for u in units]
    qk_sol = [_dot(qk[i], sol[i]) for i in range(len(units))]
    out0 = [m[:, :HEAD_DIM] for m in qk_sol]
    q_eff = [(qd[i] - qk_sol[i][:, HEAD_DIM:]).astype(_bf16) for i in range(len(units))]
    kd_sol = [_dot_tn(kd[i], sol[i]) for i in range(len(units))]
    gain = [m[:, :HEAD_DIM] for m in kd_sol]
    trans = [m[:, HEAD_DIM:].astype(_bf16) for m in kd_sol]

    st = [st_ref[i] for i in range(2 * HEADS)]
    for slot in range(nsub):
        for d in range(2):
            for h in range(HEADS):
                i, j = (d * nsub + slot) * HEADS + h, d * HEADS + h
                st_b = st[j].astype(_bf16)
                out = out0[i] + _dot(q_eff[i], st_b)
                st[j] = st[j] * jnp.exp(units[i]["g_tot"]) - _dot(trans[i], st_b) + gain[i]
                (o_f, o_b)[d][units[i]["rows"], h * HEAD_DIM:(h + 1) * HEAD_DIM] = out.astype(o_f.dtype)
    for i in range(2 * HEADS):
        st_ref[i] = st[i]


def _gdn(qkv, h, consts, first_f, first_b, rows):
    t = qkv.shape[0]
    n = t // rows
    fwd = lambda w, colblk: pl.BlockSpec((rows, w), lambda i, a, b: (i, colblk))
    bwd = lambda w, colblk: pl.BlockSpec((rows, w), lambda i, a, b: (n - 1 - i, colblk))
    gcol = COL_GB // LANES
    out = jax.ShapeDtypeStruct((t, WIDTH), BRANCH_DTYPE)
    return pl.pallas_call(
        _gdn_body,
        grid_spec=pltpu.PrefetchScalarGridSpec(
            num_scalar_prefetch=2,
            grid=(n,),
            in_specs=[fwd(WIDTH, 0), fwd(WIDTH, 1), fwd(WIDTH, 2), fwd(LANES, gcol),
                      bwd(WIDTH, 0), bwd(WIDTH, 1), bwd(WIDTH, 2), bwd(LANES, gcol),
                      pl.BlockSpec((8, LANES), lambda i, a, b: (0, 0))],
            out_specs=[fwd(WIDTH, 0), bwd(WIDTH, 0)],
            scratch_shapes=[pltpu.VMEM((2 * HEADS, HEAD_DIM, HEAD_DIM), _f32)]),
        out_shape=[out, out],
        compiler_params=_params(("arbitrary",)),
        name="gdn",
    )(first_f, first_b, qkv, qkv, qkv, h, qkv, qkv, qkv, h, consts)


def _rope(x, cos, sin_signed):
    src = _iota2((LANES, LANES), 0)
    dst = _iota2((LANES, LANES), 1)
    half = B_HEAD_DIM // 2
    partner = jnp.where(dst % B_HEAD_DIM < half, dst + half, dst - half)
    rot = jnp.dot(x.astype(_bf16), (src == partner).astype(_bf16), preferred_element_type=_f32)
    return x * cos + rot * sin_signed


def _swa_body(first_ref, last_ref, pos_ref, q_ref, kp_ref, kc_ref, kn_ref, vp_ref, vc_ref, vn_ref,
              cp_ref, sp_ref, cc_ref, sc_ref, cn_ref, sn_ref, sink_ref, o_ref):
    i = pl.program_id(0)
    blk = B_BLOCK
    nb = q_ref.shape[0] // blk
    has_prev = first_ref[i] == 0
    has_next = last_ref[i] == 0
    k_all = jnp.concatenate([_rope(kp_ref[...], cp_ref[...], sp_ref[...]),
                             _rope(kc_ref[...], cc_ref[...], sc_ref[...]),
                             _rope(kn_ref[...], cn_ref[...], sn_ref[...])], axis=0)
    v_all = jnp.concatenate([vp_ref[...], vc_ref[...], vn_ref[...]], axis=0)
    nq = B_GROUP * blk
    kpos = _iota2((3 * blk, nq), 0)
    qpos = _iota2((3 * blk, nq), 1) % blk
    rel = kpos - qpos
    in_window = (rel >= 0) & (rel <= 2 * WINDOW)
    cos, sin_signed = cc_ref[...], sc_ref[...]
    scale = B_HEAD_DIM ** -0.5
    qr = [_rope(q_ref[:, p * LANES:(p + 1) * LANES], cos, sin_signed) * scale for p in range(B_Q_HEADS // 2)]
    sink_row = sink_ref[0:1, :]
    chains = [(b, kv) for b in range(nb) for kv in range(B_KV_HEADS)]
    q4, sink, kh, vh, ok = [], [], [], [], []
    for b, kv in chains:
        rows = slice(b * blk, (b + 1) * blk)
        heads = [qr[hq // 2][rows, (hq % 2) * B_HEAD_DIM:(hq % 2 + 1) * B_HEAD_DIM]
                 for hq in range(kv * B_GROUP, (kv + 1) * B_GROUP)]
        q4.append(jnp.concatenate(heads, axis=0))
        sink.append(jnp.concatenate([jnp.broadcast_to(sink_row[:, kv * B_GROUP + g:kv * B_GROUP + g + 1], (1, blk))
                                     for g in range(B_GROUP)], axis=1))
        kh.append(k_all[b * blk:(b + 3) * blk, kv * B_HEAD_DIM:(kv + 1) * B_HEAD_DIM])
        vh.append(v_all[b * blk:(b + 3) * blk, kv * B_HEAD_DIM:(kv + 1) * B_HEAD_DIM])
        m_ok = in_window
        if b == 0:
            m_ok = m_ok & (has_prev | (kpos >= blk))
        if b == nb - 1:
            m_ok = m_ok & (has_next | (kpos < 2 * blk))
        ok.append(m_ok)
    cs = range(len(chains))
    s = [jnp.where(ok[c], _dot_nt(kh[c], q4[c]), -jnp.inf) for c in cs]
    m = [jnp.maximum(jnp.max(s[c], axis=0, keepdims=True), sink[c]) for c in cs]
    p = [jnp.exp(s[c] - m[c]) for c in cs]
    denom = [jnp.sum(p[c], axis=0, keepdims=True) + jnp.exp(sink[c] - m[c]) for c in cs]
    o_t = [_dot_tn(vh[c], p[c]) / denom[c] for c in cs]
    for c, (b, kv) in enumerate(chains):
        for g in range(0, B_GROUP, 2):
            hq = kv * B_GROUP + g
            pair = jnp.concatenate([o_t[c][:, g * blk:(g + 1) * blk],
                                    o_t[c][:, (g + 1) * blk:(g + 2) * blk]], axis=0)
            o_ref[b * blk:(b + 1) * blk, hq * B_HEAD_DIM:(hq + 2) * B_HEAD_DIM] = (
                jnp.transpose(pair).astype(o_ref.dtype))


def _swa(h, cos_t, sin_t, sink, first, last, pos, nb):
    t = h.shape[0]
    rows = nb * B_BLOCK
    n = t // rows
    n128 = t // B_BLOCK
    npos = cos_t.shape[0] // B_BLOCK
    kcol, vcol = COL_B_K // LANES, COL_B_V // LANES
    before = lambda colblk: pl.BlockSpec((B_BLOCK, LANES), lambda i, a, b, p: (jnp.maximum(i * nb - 1, 0), colblk))
    own = lambda colblk: pl.BlockSpec((rows, LANES), lambda i, a, b, p: (i, colblk))
    after = lambda colblk: pl.BlockSpec((B_BLOCK, LANES),
                                        lambda i, a, b, p: (jnp.minimum((i + 1) * nb, n128 - 1), colblk))
    t_before = pl.BlockSpec((B_BLOCK, LANES), lambda i, a, b, p: (jnp.maximum(p[i] * nb - 1, 0), 0))
    t_own = pl.BlockSpec((rows, LANES), lambda i, a, b, p: (p[i], 0))
    t_after = pl.BlockSpec((B_BLOCK, LANES), lambda i, a, b, p: (jnp.minimum((p[i] + 1) * nb, npos - 1), 0))
    qw = B_Q_HEADS * B_HEAD_DIM
    return pl.pallas_call(
        _swa_body,
        grid_spec=pltpu.PrefetchScalarGridSpec(
            num_scalar_prefetch=3,
            grid=(n,),
            in_specs=[pl.BlockSpec((rows, qw), lambda i, a, b, p: (i, COL_B_Q // qw)),
                      before(kcol), own(kcol), after(kcol), before(vcol), own(vcol), after(vcol),
                      t_before, t_before, t_own, t_own, t_after, t_after,
                      pl.BlockSpec((8, LANES), lambda i, a, b, p: (0, 0))],
            out_specs=pl.BlockSpec((rows, qw), lambda i, a, b, p: (i, 0))),
        out_shape=jax.ShapeDtypeStruct((t, qw), BRANCH_DTYPE),
        compiler_params=_params(("parallel",)),
        name="window_attn",
    )(first, last, pos, h, h, h, h, h, h, h, cos_t, sin_t, cos_t, sin_t, cos_t, sin_t, sink)


def _residual_ln(x, y, g, b):
    z = DN_ALPHA * x + y
    mu = jnp.mean(z, axis=1, keepdims=True)
    zc = z - mu
    var = jnp.mean(zc * zc, axis=1, keepdims=True)
    return zc * lax.rsqrt(var + 1e-5) * g + b


def _gated_rms(o, gate, g):
    outs = []
    for h in range(HEADS):
        sl = slice(h * HEAD_DIM, (h + 1) * HEAD_DIM)
        x = o[:, sl]
        x = x * lax.rsqrt(jnp.mean(x * x, axis=1, keepdims=True) + 1e-6) * g
        outs.append(x * _silu(gate[:, sl]))
    return jnp.concatenate(outs, axis=1)


def _merge_body(x_ref, af_ref, ab_ref, ag_ref, ob_ref, cf_ref, cb_ref, cg_ref, ma_ref, mb_ref, mc_ref,
                wa_ref, wb_ref, wc_ref, wo_ref, nrm_ref, ln_ref, o_ref):
    up = lambda ref: ref[...].astype(_f32)
    oa = _gated_rms(up(af_ref) + up(ab_ref), ag_ref[...], nrm_ref[0:1, :])
    oc = _gated_rms(up(cf_ref) + up(cb_ref), cg_ref[...], nrm_ref[1:2, :])
    pa = jnp.dot(oa.astype(_bf16), wa_ref[...], preferred_element_type=_f32)
    pb = jnp.dot(ob_ref[...].astype(_bf16), wb_ref[...], preferred_element_type=_f32)
    pc = jnp.dot(oc.astype(_bf16), wc_ref[...], preferred_element_type=_f32)
    mix = (jax.nn.sigmoid(ma_ref[...]) * pa + jax.nn.sigmoid(mb_ref[...]) * pb
           + jax.nn.sigmoid(mc_ref[...]) * pc)
    y = jnp.dot(mix.astype(_bf16), wo_ref[...], preferred_element_type=_f32)
    o_ref[...] = _residual_ln(x_ref[...], y, ln_ref[0:1, :], ln_ref[1:2, :])


def _merge(x, ha, hb, oa_f, oa_b, ob, oc_f, oc_b, wa, wb, wc, wo, nrm, ln, tm):
    t = x.shape[0]
    row = lambda w, colblk=0: pl.BlockSpec((tm, w), lambda i: (i, colblk))
    full = lambda a: pl.BlockSpec(a.shape, lambda i: (0, 0))
    mcol = COL_MERGE // D_MODEL
    return pl.pallas_call(
        _merge_body,
        grid=(t // tm,),
        in_specs=[row(D_MODEL), row(WIDTH), row(WIDTH), row(WIDTH, COL_A_G // WIDTH), row(WIDTH),
                  row(WIDTH), row(WIDTH), row(WIDTH, COL_C_GATE // WIDTH),
                  row(D_MODEL, mcol), row(D_MODEL, mcol + 1), row(D_MODEL, mcol + 2),
                  full(wa), full(wb), full(wc), full(wo), full(nrm), full(ln)],
        out_specs=row(D_MODEL),
        out_shape=jax.ShapeDtypeStruct((t, D_MODEL), _f32),
        compiler_params=_params(("parallel",)),
        name="merge_mix",
    )(x, oa_f, oa_b, ha, ob, oc_f, oc_b, ha, hb, hb, hb, wa, wb, wc, wo, nrm, ln)


def _xattn_body(seq_ref, x_ref, kv_ref, wq_ref, wo_ref, ln_ref, o_ref):
    x = x_ref[...]
    q = jnp.dot(x.astype(_bf16), wq_ref[...], preferred_element_type=_f32)
    hs = range(MEM_HEADS)
    cols = [slice(h * MEM_HEAD_DIM, (h + 1) * MEM_HEAD_DIM) for h in hs]
    s = [_dot_nt(q[:, cols[h]], kv_ref[0, :, cols[h]]) * (MEM_HEAD_DIM ** -0.5) for h in hs]
    p = [jnp.exp(s[h] - jnp.max(s[h], axis=1, keepdims=True)) for h in hs]
    p = [p[h] / jnp.sum(p[h], axis=1, keepdims=True) for h in hs]
    o = jnp.concatenate([_dot(p[h], kv_ref[0, :, D_MODEL + h * MEM_HEAD_DIM:D_MODEL + (h + 1) * MEM_HEAD_DIM])
                         for h in hs], axis=1)
    y = jnp.dot(o.astype(_bf16), wo_ref[...], preferred_element_type=_f32)
    o_ref[...] = _residual_ln(x, y, ln_ref[0:1, :], ln_ref[1:2, :])


def _xattn(x, kv, wq, wo, ln, seq_of_tile, tm):
    t = x.shape[0]
    full = lambda a: pl.BlockSpec(a.shape, lambda i, s: (0, 0))
    return pl.pallas_call(
        _xattn_body,
        grid_spec=pltpu.PrefetchScalarGridSpec(
            num_scalar_prefetch=1,
            grid=(t // tm,),
            in_specs=[pl.BlockSpec((tm, D_MODEL), lambda i, s: (i, 0)),
                      pl.BlockSpec((1,) + kv.shape[1:], lambda i, s: (s[i], 0, 0)),
                      full(wq), full(wo), full(ln)],
            out_specs=pl.BlockSpec((tm, D_MODEL), lambda i, s: (i, 0))),
        out_shape=jax.ShapeDtypeStruct((t, D_MODEL), _f32),
        compiler_params=_params(("parallel",)),
        name="mem_xattn",
    )(seq_of_tile, x, kv, wq, wo, ln)


def _ffn_tile(x_ref, w1_ref, w2_ref, ln_ref):
    x = x_ref[...]
    xb = x.astype(_bf16)
    gate = jnp.dot(xb, w1_ref[0], preferred_element_type=_f32)
    up = jnp.dot(xb, w1_ref[1], preferred_element_type=_f32)
    y = jnp.dot((_silu(gate) * up).astype(_bf16), w2_ref[...], preferred_element_type=_f32)
    return _residual_ln(x, y, ln_ref[0:1, :], ln_ref[1:2, :])


def _ffn_body(x_ref, w1_ref, w2_ref, ln_ref, o_ref):
    o_ref[...] = _ffn_tile(x_ref, w1_ref, w2_ref, ln_ref)


def _ffn_split_body(x_ref, w1_ref, w2_ref, ln_ref, head_ref, tail_ref, *, n_head):
    i = pl.program_id(0)
    y = _ffn_tile(x_ref, w1_ref, w2_ref, ln_ref)

    @pl.when(i < n_head)
    def _():
        head_ref[...] = y

    @pl.when(i >= n_head)
    def _():
        tail_ref[...] = y


def _ffn(x, w1, w2, ln, tm, split_rows=None):
    t = x.shape[0]
    ins = [pl.BlockSpec((tm, D_MODEL), lambda i: (i, 0)),
           _resident(w1.shape), _resident(w2.shape), _resident(ln.shape)]
    if split_rows is None:
        return pl.pallas_call(
            _ffn_body,
            grid=(t // tm,),
            in_specs=ins,
            out_specs=pl.BlockSpec((tm, D_MODEL), lambda i: (i, 0)),
            out_shape=jax.ShapeDtypeStruct((t, D_MODEL), _f32),
            compiler_params=_params(("parallel",)),
            name="swiglu_ffn",
        )(x, w1, w2, ln)
    n_head = split_rows // tm
    return pl.pallas_call(
        functools.partial(_ffn_split_body, n_head=n_head),
        grid=(t // tm,),
        in_specs=ins,
        out_specs=[pl.BlockSpec((tm, D_MODEL), lambda i: (jnp.minimum(i, n_head - 1), 0)),
                   pl.BlockSpec((tm, D_MODEL), lambda i: (jnp.maximum(i - n_head, 0), 0))],
        out_shape=[jax.ShapeDtypeStruct((split_rows, D_MODEL), _f32),
                   jax.ShapeDtypeStruct((t - split_rows, D_MODEL), _f32)],
        compiler_params=_params(("arbitrary",)),
        name="swiglu_ffn_split",
    )(x, w1, w2, ln)


def _boundary_tables(seq_lens, tile):
    first, last, pos, seq = [], [], [], []
    for sid, length in enumerate(seq_lens):
        n = length // tile
        for b in range(n):
            first.append(int(b == 0))
            last.append(int(b == n - 1))
            pos.append(b)
            seq.append(sid)
    as_i32 = lambda v: jnp.asarray(np.asarray(v, np.int32))
    return as_i32(first), as_i32(last), as_i32(pos), as_i32(seq)


def _split_in_cols(w):
    pad = jnp.zeros(w.shape[:-1] + (HA_COLS - 3968,), w.dtype)
    wa = jnp.concatenate([w[..., 0:3072], w[..., 4864:5376], w[..., 3072:3200], w[..., 3200:3328],
                          w[..., 5376:5392], pad], axis=-1)
    wb = jnp.concatenate([w[..., 5392:8464], w[..., 3328:4864]], axis=-1)
    return wa, wb


def _rows8(*rows):
    n = rows[0].shape[-1]
    out = jnp.zeros((8, n), _f32)
    for r, v in enumerate(rows):
        out = out.at[r].set(v.astype(_f32))
    return out


def _lane_pad(v, offset):
    return jnp.zeros((LANES,), _f32).at[offset:offset + v.shape[0]].set(v.astype(_f32))


def kernel(x_prompt, x_sample, mem_prompt, mem_sample, w_in, hgrn_lb_logits, hgrn_norm_g, attn_sink,
           gdn_conv_w, gdn_a_log, gdn_dt_bias, gdn_norm_g, w_branch_a, w_branch_b, w_branch_c, w_mix_out,
           w_mem_q, w_mem_kv, w_mem_o, w_ffn_in, w_ffn_out, ln_g, ln_b):
    depth = w_in.shape[0]
    d = x_prompt.shape[-1]
    seq_lens = (x_prompt.shape[1],) * x_prompt.shape[0] + (x_sample.shape[1],) * x_sample.shape[0]
    n_prompt = x_prompt.shape[0] * x_prompt.shape[1]
    x = jnp.concatenate([x_prompt.reshape(-1, d), x_sample.reshape(-1, d)], axis=0)
    mem = jnp.concatenate([mem_prompt, mem_sample], axis=0)
    n_seq, n_mem, _ = mem.shape
    t = x.shape[0]

    tm = math.gcd(512, *seq_lens)
    tm_proj = math.gcd(512, t)
    rows_a = math.gcd(HGRN_CHUNKS_PER_STEP * CHUNK, *seq_lens)
    first_a, last_a, _, _ = _boundary_tables(seq_lens, rows_a)
    last_a_rev = last_a[::-1]
    rows_g = math.gcd(GDN_CHUNKS_PER_STEP * CHUNK, *seq_lens)
    first_g, last_g, _, _ = _boundary_tables(seq_lens, rows_g)
    last_g_rev = last_g[::-1]
    nb_swa = math.gcd(SWA_BLOCKS_PER_STEP * B_BLOCK, *seq_lens) // B_BLOCK
    first_w, last_w, pos_w, _ = _boundary_tables(seq_lens, nb_swa * B_BLOCK)
    first_t, last_t, _, seq_t = _boundary_tables(seq_lens, tm)

    s_max = max(seq_lens)
    inv = ROPE_THETA ** (-jnp.arange(0, B_HEAD_DIM, 2, dtype=_f32) / B_HEAD_DIM)
    ang = jnp.arange(s_max, dtype=_f32)[:, None] * inv[None, :]
    cos_t = jnp.tile(jnp.cos(ang), (1, 4))
    sin_t = jnp.tile(jnp.concatenate([-jnp.sin(ang), jnp.sin(ang)], axis=1), (1, 2))

    cum = jnp.cumsum(jax.nn.softmax(hgrn_lb_logits.astype(_f32), axis=1), axis=1)
    lb = cum - cum[:, :1]

    w_in_a, w_in_b = (w.astype(_bf16) for w in _split_in_cols(w_in))
    bf = lambda w: w.astype(_bf16)
    w_a, w_b, w_c, w_mix = bf(w_branch_a), bf(w_branch_b), bf(w_branch_c), bf(w_mix_out)
    w_q, w_kv, w_o = bf(w_mem_q), bf(w_mem_kv), bf(w_mem_o)
    w_f1, w_f2 = bf(w_ffn_in), bf(w_ffn_out)
    mem2 = mem.reshape(n_seq * n_mem, d)

    for l in range(depth):
        lb_rows = []
        for dirn in range(2):
            lbd = lb[dirn, l]
            lb_rows += [jnp.log(lbd), jnp.log1p(-lbd)]
        ha = _in_proj(_in_proj_a_body, "in_proj_a", x, w_in_a[l], [_rows8(*lb_rows)], tm_proj)
        hb = _in_proj(_in_proj_b_body, "in_proj_b", x, w_in_b[l], [], tm_proj)
        oa_f, oa_b = _hgrn(ha, first_a, last_a_rev, rows_a)
        o_b = _swa(ha, cos_t, sin_t, _rows8(_lane_pad(attn_sink[l], 0)), first_w, last_w, pos_w, nb_swa)
        qkv = _gdn_prep(hb, _rows8(*[gdn_conv_w[l, j] for j in range(CONV_K)]), first_t, last_t, tm)
        gconst = _rows8(_lane_pad(-jnp.exp(gdn_a_log[l].astype(_f32)).reshape(-1), 2 * HEADS),
                        _lane_pad(gdn_dt_bias[l].reshape(-1), 2 * HEADS))
        oc_f, oc_b = _gdn(qkv, ha, gconst, first_g, last_g_rev, rows_g)
        nrm = _rows8(hgrn_norm_g[l], gdn_norm_g[l])
        x = _merge(x, ha, hb, oa_f, oa_b, o_b, oc_f, oc_b, w_a[l], w_b[l], w_c[l], w_mix[l], nrm,
                   _rows8(ln_g[l, 0], ln_b[l, 0]), tm)
        kv = _matmul(mem2, _col_tiles(w_kv[l], 512), n_mem).reshape(n_seq, n_mem, 2 * d)
        x = _xattn(x, kv, w_q[l], w_o[l], _rows8(ln_g[l, 1], ln_b[l, 1]), seq_t, tm)
        x = _ffn(x, _col_tiles(w_f1[l], FFN_HIDDEN), w_f2[l], _rows8(ln_g[l, 2], ln_b[l, 2]), tm,
                 split_rows=n_prompt if l == depth - 1 else None)

    y_prompt, y_sample = x
    return (y_prompt.reshape(x_prompt.shape), y_sample.reshape(x_sample.shape))
```

```python
import functools
import math

import numpy as np
import jax
import jax.numpy as jnp
from jax import lax
from jax.experimental import pallas as pl
from jax.experimental.pallas import tpu as pltpu

D_MODEL = 1024
DEPTH = 4
HEADS = 4
HEAD_DIM = 128
WIDTH = HEADS * HEAD_DIM
B_Q_HEADS = 8
B_KV_HEADS = 2
B_GROUP = B_Q_HEADS // B_KV_HEADS
B_HEAD_DIM = 64
WINDOW = 128
B_BLOCK = 128
ROPE_THETA = 10000.0
CONV_K = 5
MEM_HEADS = 4
MEM_HEAD_DIM = D_MODEL // MEM_HEADS
FFN_HIDDEN = 2816
DN_ALPHA = (2 * DEPTH) ** 0.25

CHUNK = 64
GDN_CHUNKS_PER_STEP = 4
HGRN_CHUNKS_PER_STEP = 8
SWA_BLOCKS_PER_STEP = 8
LANES = 128
LOG2_E = math.log2(math.e)
VMEM_LIMIT = 56 * 1024 * 1024

COL_A_Q, COL_A_FF, COL_A_FB, COL_A_I, COL_A_G = 0, 512, 1024, 1536, 2048
COL_B_Q = 2560
COL_C_GATE = 3072
COL_B_K, COL_B_V = 3584, 3712
COL_GB = 3840
HA_COLS = 4096

_f32 = jnp.float32
_bf16 = jnp.bfloat16
BRANCH_DTYPE = _bf16
GATE_DTYPE = _bf16


def _dot(a, b):
    return jnp.dot(a.astype(_bf16), b.astype(_bf16), preferred_element_type=_f32)


def _dot_nt(a, b):
    return lax.dot_general(a.astype(_bf16), b.astype(_bf16), (((1,), (1,)), ((), ())),
                           preferred_element_type=_f32)


def _dot_tn(a, b):
    return lax.dot_general(a.astype(_bf16), b.astype(_bf16), (((0,), (0,)), ((), ())),
                           preferred_element_type=_f32)


def _split3(x):
    hi = x.astype(_bf16)
    r1 = x - hi.astype(_f32)
    mid = r1.astype(_bf16)
    lo = (r1 - mid.astype(_f32)).astype(_bf16)
    return hi, mid, lo


def _exact_dot(sel, parts):
    dot = lambda p: jnp.dot(sel, p, preferred_element_type=_f32)
    return dot(parts[0]) + dot(parts[1]) + dot(parts[2])


def _silu(x):
    return x * jax.nn.sigmoid(x)


def _iota2(shape, dim):
    return lax.broadcasted_iota(jnp.int32, shape, dim)


def _params(sem):
    return pltpu.CompilerParams(dimension_semantics=sem, vmem_limit_bytes=VMEM_LIMIT)


def _resident(shape):
    return pl.BlockSpec(shape, lambda *_: (0,) * len(shape), pipeline_mode=pl.Buffered(1))


def _col_tiles(w, tn):
    k, n = w.shape
    return w.reshape(k, n // tn, tn).transpose(1, 0, 2)


def _matmul_body(x_ref, w_ref, o_ref, xb_ref):
    j = pl.program_id(1)

    @pl.when(j == 0)
    def _():
        xb_ref[...] = x_ref[...].astype(_bf16)

    o_ref[...] = jnp.dot(xb_ref[...], w_ref[j], preferred_element_type=_f32)


def _matmul(x, w_tiles, tm):
    t, k = x.shape
    nt, _, tn = w_tiles.shape
    n = nt * tn
    return pl.pallas_call(
        _matmul_body,
        grid=(t // tm, nt),
        in_specs=[pl.BlockSpec((tm, k), lambda i, j: (i, 0)), _resident(w_tiles.shape)],
        out_specs=pl.BlockSpec((tm, tn), lambda i, j: (i, j)),
        out_shape=jax.ShapeDtypeStruct((t, n), _f32),
        scratch_shapes=[pltpu.VMEM((tm, k), _bf16)],
        compiler_params=_params(("parallel", "arbitrary")),
        name="dense_proj",
    )(x, w_tiles)


def _in_proj_a_body(x_ref, w_ref, lb_ref, o_ref):
    xb = x_ref[...].astype(_bf16)
    proj = lambda lo, hi: jnp.dot(xb, w_ref[:, lo:hi], preferred_element_type=_f32)

    def log_f(d):
        def fn(z):
            log_sig = jnp.minimum(z, 0.0) - jnp.log(1.0 + jnp.exp(-jnp.abs(z)))
            log_lb = lb_ref[2 * d:2 * d + 1, :]
            b = lb_ref[2 * d + 1:2 * d + 2, :] + log_sig
            return jnp.maximum(log_lb, b) + jnp.log(1.0 + jnp.exp(-jnp.abs(log_lb - b)))
        return fn

    groups = [(COL_A_Q, COL_A_Q + WIDTH, lambda z: _silu(z) * (HEAD_DIM ** -0.5)),
              (COL_A_FF, COL_A_FF + WIDTH, log_f(0)),
              (COL_A_FB, COL_A_FB + WIDTH, log_f(1)),
              (COL_A_I, o_ref.shape[1], lambda z: z)]
    z_next = proj(groups[0][0], groups[0][1])
    for g, (lo, hi, fn) in enumerate(groups):
        z = z_next
        if g + 1 < len(groups):
            z_next = proj(groups[g + 1][0], groups[g + 1][1])
        o_ref[:, lo:hi] = fn(z)


def _in_proj_b_body(x_ref, w_ref, gates_ref, qkv_ref):
    xb = x_ref[...].astype(_bf16)
    split = gates_ref.shape[1]
    gates_ref[...] = jnp.dot(xb, w_ref[:, :split], preferred_element_type=_f32).astype(gates_ref.dtype)
    qkv_ref[...] = jnp.dot(xb, w_ref[:, split:], preferred_element_type=_f32)


def _in_proj(body, name, x, w, consts, tm, outs):
    t, k = x.shape
    return pl.pallas_call(
        body,
        grid=(t // tm,),
        in_specs=[pl.BlockSpec((tm, k), lambda i: (i, 0)), _resident(w.shape)] + [_resident(c.shape) for c in consts],
        out_specs=[pl.BlockSpec((tm, width), lambda i: (i, 0)) for width, _ in outs],
        out_shape=[jax.ShapeDtypeStruct((t, width), dtype) for width, dtype in outs],
        compiler_params=_params(("parallel",)),
        name=name,
    )(x, w, *consts)


def _chunk_masks(c, reverse):
    row = _iota2((c, c), 0)
    col = _iota2((c, c), 1)
    if reverse:
        return col >= row, col > row
    return col <= row, col < row


def _level_mask(c, blk, reverse, reps=1):
    row = _iota2((c, reps * c), 0)
    col = _iota2((c, reps * c), 1) % c
    half = blk // 2
    same = (row // blk) == (col // blk)
    r_hi = (row % blk) >= half
    c_hi = (col % blk) >= half
    if reverse:
        return same & jnp.logical_not(r_hi) & c_hi
    return same & r_hi & jnp.logical_not(c_hi)


def _block_ref_rows(g, blk, reverse):
    c, n = g.shape
    idx = blk // 2 if reverse else blk // 2 - 1
    rows = max(blk, 8)
    g3 = g.reshape(c // rows, rows, n)
    pick = lambda r: jnp.broadcast_to(g3[:, r:r + 1, :], g3.shape)
    out = pick(idx)
    if blk < rows:
        sub = lax.broadcasted_iota(jnp.int32, g3.shape, 1)
        for b in range(1, rows // blk):
            out = jnp.where(sub >= b * blk, pick(b * blk + idx), out)
    return out.reshape(c, n)


def _levels(c):
    out, blk = [], c
    while blk >= 2:
        out.append(blk)
        blk //= 2
    return out


def _hgrn_body(first_f_ref, first_b_ref, q_f, lf_f, v_f, q_b, lf_b, v_b, o_f, o_b, st_ref):
    c = CHUNK
    step = pl.program_id(0)

    @pl.when(first_f_ref[step] == 1)
    def _():
        st_ref[0:HEADS] = jnp.zeros((HEADS, HEAD_DIM, HEAD_DIM), _f32)

    @pl.when(first_b_ref[step] == 1)
    def _():
        st_ref[HEADS:2 * HEADS] = jnp.zeros((HEADS, HEAD_DIM, HEAD_DIM), _f32)

    nsub = q_f.shape[0] // c
    row = _iota2((c, c), 0)
    col = _iota2((c, c), 1)
    eye = row == col
    blks = _levels(c)
    units = []
    for d, (q_ref, lf_ref, v_ref) in enumerate(((q_f, lf_f, v_f), (q_b, lf_b, v_b))):
        reverse = d == 1
        incl, _ = _chunk_masks(c, reverse)
        tri = incl.astype(_bf16)
        masks = [_level_mask(c, blk, reverse) for blk in blks]
        log_f = lf_ref[...]
        key_all = 1.0 - jnp.exp(log_f)
        rpos = _iota2((c, HEAD_DIM), 0)
        later = [((rpos % blk) < blk // 2) if reverse else ((rpos % blk) >= blk // 2) for blk in blks]
        sgn = [jnp.where(m, LOG2_E, -LOG2_E) for m in later]
        q_all = q_ref[...]
        v_all = v_ref[...]
        tot_row = 0 if reverse else c - 1
        for s in (range(nsub - 1, -1, -1) if reverse else range(nsub)):
            rs = slice(s * c, (s + 1) * c)
            gc_all = _exact_dot(tri, _split3(log_f[rs]))
            ref_all = [_block_ref_rows(gc_all, blk, reverse) for blk in blks]
            for h in range(HEADS):
                sl = slice(h * HEAD_DIM, (h + 1) * HEAD_DIM)
                units.append(dict(q=q_all[rs, sl], k=key_all[rs, sl], v=v_all[rs, sl], gc=gc_all[:, sl],
                                  refs=[r[:, sl] for r in ref_all], masks=masks, sgn=sgn, rows=rs,
                                  g_tot=gc_all[tot_row:tot_row + 1, sl]))

    scores = [jnp.where(eye, _dot_nt(u["q"], u["k"]), 0.0) for u in units]
    for l in range(len(blks)):
        e = [jnp.exp2((u["gc"] - u["refs"][l]) * u["sgn"][l]) for u in units]
        scores = [jnp.where(u["masks"][l], _dot_nt(u["q"] * e[i], u["k"] * e[i]), scores[i])
                  for i, u in enumerate(units)]
    intra = [_dot(scores[i], u["v"]) for i, u in enumerate(units)]
    qd = [u["q"] * jnp.exp(u["gc"]) for u in units]
    kd = [u["k"] * jnp.exp(u["g_tot"] - u["gc"]) for u in units]

    st = [st_ref[i] for i in range(2 * HEADS)]
    for slot in range(nsub):
        for d in range(2):
            for h in range(HEADS):
                i, j = (d * nsub + slot) * HEADS + h, d * HEADS + h
                out = intra[i] + _dot_nt(qd[i], st[j])
                st[j] = st[j] * jnp.exp(units[i]["g_tot"]) + _dot_tn(units[i]["v"], kd[i])
                (o_f, o_b)[d][units[i]["rows"], h * HEAD_DIM:(h + 1) * HEAD_DIM] = out.astype(o_f.dtype)
    for i in range(2 * HEADS):
        st_ref[i] = st[i]


def _hgrn(h, first_f, first_b, rows):
    t = h.shape[0]
    n = t // rows
    fwd = lambda col: pl.BlockSpec((rows, WIDTH), lambda i, a, b: (i, col // WIDTH))
    bwd = lambda col: pl.BlockSpec((rows, WIDTH), lambda i, a, b: (n - 1 - i, col // WIDTH))
    out = jax.ShapeDtypeStruct((t, WIDTH), BRANCH_DTYPE)
    return pl.pallas_call(
        _hgrn_body,
        grid_spec=pltpu.PrefetchScalarGridSpec(
            num_scalar_prefetch=2,
            grid=(n,),
            in_specs=[fwd(COL_A_Q), fwd(COL_A_FF), fwd(COL_A_I), bwd(COL_A_Q), bwd(COL_A_FB), bwd(COL_A_I)],
            out_specs=[fwd(0), bwd(0)],
            scratch_shapes=[pltpu.VMEM((2 * HEADS, HEAD_DIM, HEAD_DIM), _f32)]),
        out_shape=[out, out],
        compiler_params=_params(("arbitrary",)),
        name="hgrn",
    )(first_f, first_b, h, h, h, h, h, h)


def _gdn_prep_body(first_ref, last_ref, prev_ref, cur_ref, next_ref, w_ref, o_ref, ext_ref):
    i = pl.program_id(0)
    tm = cur_ref.shape[0]
    halo = CONV_K // 2
    ext_ref[0:8, :] = jnp.where(first_ref[i] == 1, 0.0, prev_ref[...])
    ext_ref[8:8 + tm, :] = cur_ref[...]
    ext_ref[8 + tm:16 + tm, :] = jnp.where(last_ref[i] == 1, 0.0, next_ref[...])
    acc = cur_ref[...] * w_ref[halo:halo + 1, :]
    for j in range(CONV_K):
        if j != halo:
            off = 8 - halo + j
            acc = acc + ext_ref[off:off + tm, :] * w_ref[j:j + 1, :]
    y = _silu(acc)
    for part in range(3):
        for h in range(HEADS):
            lo = part * WIDTH + h * HEAD_DIM
            x = y[:, lo:lo + HEAD_DIM]
            if part < 2:
                x = x * lax.rsqrt(jnp.sum(x * x, axis=1, keepdims=True) + 1e-6)
            if part == 0:
                x = x * (HEAD_DIM ** -0.5)
            o_ref[:, lo:lo + HEAD_DIM] = x


def _gdn_prep(h, conv_w, first, last, tm):
    t = h.shape[0]
    n = t // tm
    r8 = tm // 8
    cb = 0
    return pl.pallas_call(
        _gdn_prep_body,
        grid_spec=pltpu.PrefetchScalarGridSpec(
            num_scalar_prefetch=2,
            grid=(n,),
            in_specs=[pl.BlockSpec((8, 3 * WIDTH), lambda i, f, l: (jnp.maximum(i * r8 - 1, 0), cb)),
                      pl.BlockSpec((tm, 3 * WIDTH), lambda i, f, l: (i, cb)),
                      pl.BlockSpec((8, 3 * WIDTH), lambda i, f, l: (jnp.minimum((i + 1) * r8, n * r8 - 1), cb)),
                      pl.BlockSpec((8, 3 * WIDTH), lambda i, f, l: (0, 0))],
            out_specs=pl.BlockSpec((tm, 3 * WIDTH), lambda i, f, l: (i, 0)),
            scratch_shapes=[pltpu.VMEM((tm + 16, 3 * WIDTH), _f32)]),
        out_shape=jax.ShapeDtypeStruct((t, 3 * WIDTH), _f32),
        compiler_params=_params(("parallel",)),
        name="gdn_prep",
    )(first, last, h, h, h, conv_w)


def _gdn_body(first_f_ref, first_b_ref, q_f, k_f, v_f, gb_f, q_b, k_b, v_b, gb_b, cst_ref, o_f, o_b, st_ref):
    c = CHUNK
    step = pl.program_id(0)

    @pl.when(first_f_ref[step] == 1)
    def _():
        st_ref[0:HEADS] = jnp.zeros((HEADS, HEAD_DIM, HEAD_DIM), _f32)

    @pl.when(first_b_ref[step] == 1)
    def _():
        st_ref[HEADS:2 * HEADS] = jnp.zeros((HEADS, HEAD_DIM, HEAD_DIM), _f32)

    nsub = q_f.shape[0] // c
    row = _iota2((c, c), 0)
    col = _iota2((c, c), 1)
    eye = (row == col).astype(_f32)
    neg_a = cst_ref[0:1, :]
    dt_bias = cst_ref[1:2, :]
    sizes = _levels(c)[::-1]
    units = []
    for d, (q_ref, k_ref, v_ref, gb_ref) in enumerate(((q_f, k_f, v_f, gb_f), (q_b, k_b, v_b, gb_b))):
        reverse = d == 1
        incl, strict = _chunk_masks(c, reverse)
        tri = incl.astype(_bf16)
        masks = [_level_mask(c, blk, reverse) for blk in sizes]
        gb = gb_ref[...]
        beta_t = jax.nn.sigmoid(gb)
        zz = gb + dt_bias
        g_in = neg_a * (jnp.maximum(zz, 0.0) + jnp.log1p(jnp.exp(-jnp.abs(zz))))
        tot_row = 0 if reverse else c - 1
        for s in (range(nsub - 1, -1, -1) if reverse else range(nsub)):
            rs = slice(s * c, (s + 1) * c)
            gc_t = _exact_dot(tri, _split3(g_in[rs]))
            gc_tt = jnp.transpose(gc_t)
            for h in range(HEADS):
                sl = slice(h * HEAD_DIM, (h + 1) * HEAD_DIM)
                j = 2 * HEADS + d * HEADS + h
                g_col = gc_t[:, j:j + 1]
                decay = jnp.where(incl, jnp.exp(jnp.minimum(g_col - gc_tt[j:j + 1, :], 0.0)), 0.0)
                units.append(dict(q=q_ref[rs, sl], k=k_ref[rs, sl], v=v_ref[rs, sl], rows=rs,
                                  g_col=g_col, decay=decay, strict=strict, masks=masks,
                                  beta=beta_t[rs, d * HEADS + h:d * HEADS + h + 1],
                                  g_tot=gc_t[tot_row:tot_row + 1, j:j + 1]))

    k_b = [u["k"].astype(_bf16) for u in units]
    a = [jnp.where(u["strict"], _dot_nt(k_b[i], k_b[i]) * u["decay"] * u["beta"], 0.0) for i, u in enumerate(units)]
    qk = [_dot_nt(u["q"], k_b[i]) * u["decay"] for i, u in enumerate(units)]
    inv = [eye - jnp.where(u["masks"][0], a[i], 0.0) for i, u in enumerate(units)]
    for l in range(1, len(sizes)):
        inv_b = [m.astype(_bf16) for m in inv]
        t1 = [_dot(jnp.where(u["masks"][l], a[i], 0.0), inv_b[i]) for i, u in enumerate(units)]
        inv = [inv[i] - _dot(inv_b[i], t1[i]) for i in range(len(units))]
    inv_b = [m.astype(_bf16) for m in inv]
    e_g = [jnp.exp(u["g_col"]) for u in units]
    sol = [_dot(inv_b[i], jnp.concatenate([u["v"] * u["beta"], u["k"] * (u["beta"] * e_g[i])], axis=1))
           for i, u in enumerate(units)]
    uu = [m[:, :HEAD_DIM] for m in sol]
    ww = [m[:, HEAD_DIM:] for m in sol]
    qd = [u["q"] * e_g[i] for i, u in enumerate(units)]
    kd = [u["k"] * jnp.exp(u["g_tot"] - u["g_col"]) for u in units]
    qk_sol = [_dot(qk[i], sol[i]) for i in range(len(units))]
    out0 = [m[:, :HEAD_DIM] for m in qk_sol]
    q_eff = [(qd[i] - qk_sol[i][:, HEAD_DIM:]).astype(_bf16) for i in range(len(units))]
    kd_sol = [_dot_tn(kd[i], sol[i]) for i in range(len(units))]
    gain = [m[:, :HEAD_DIM] for m in kd_sol]
    trans = [m[:, HEAD_DIM:].astype(_bf16) for m in kd_sol]

    st = [st_ref[i] for i in range(2 * HEADS)]
    for slot in range(nsub):
        for d in range(2):
            for h in range(HEADS):
                i, j = (d * nsub + slot) * HEADS + h, d * HEADS + h
                st_b = st[j].astype(_bf16)
                out = out0[i] + _dot(q_eff[i], st_b)
                st[j] = st[j] * jnp.exp(units[i]["g_tot"]) - _dot(trans[i], st_b) + gain[i]
                (o_f, o_b)[d][units[i]["rows"], h * HEAD_DIM:(h + 1) * HEAD_DIM] = out.astype(o_f.dtype)
    for i in range(2 * HEADS):
        st_ref[i] = st[i]


def _gdn(qkv, h, consts, first_f, first_b, rows):
    t = qkv.shape[0]
    n = t // rows
    fwd = lambda w, colblk: pl.BlockSpec((rows, w), lambda i, a, b: (i, colblk))
    bwd = lambda w, colblk: pl.BlockSpec((rows, w), lambda i, a, b: (n - 1 - i, colblk))
    gcol = COL_GB // LANES
    out = jax.ShapeDtypeStruct((t, WIDTH), BRANCH_DTYPE)
    return pl.pallas_call(
        _gdn_body,
        grid_spec=pltpu.PrefetchScalarGridSpec(
            num_scalar_prefetch=2,
            grid=(n,),
            in_specs=[fwd(WIDTH, 0), fwd(WIDTH, 1), fwd(WIDTH, 2), fwd(LANES, gcol),
                      bwd(WIDTH, 0), bwd(WIDTH, 1), bwd(WIDTH, 2), bwd(LANES, gcol),
                      pl.BlockSpec((8, LANES), lambda i, a, b: (0, 0))],
            out_specs=[fwd(WIDTH, 0), bwd(WIDTH, 0)],
            scratch_shapes=[pltpu.VMEM((2 * HEADS, HEAD_DIM, HEAD_DIM), _f32)]),
        out_shape=[out, out],
        compiler_params=_params(("arbitrary",)),
        name="gdn",
    )(first_f, first_b, qkv, qkv, qkv, h, qkv, qkv, qkv, h, consts)


def _rope(x, cos, sin_signed):
    src = _iota2((LANES, LANES), 0)
    dst = _iota2((LANES, LANES), 1)
    half = B_HEAD_DIM // 2
    partner = jnp.where(dst % B_HEAD_DIM < half, dst + half, dst - half)
    rot = jnp.dot(x.astype(_bf16), (src == partner).astype(_bf16), preferred_element_type=_f32)
    return x * cos + rot * sin_signed


def _swa_body(first_ref, last_ref, pos_ref, q_ref, kp_ref, kc_ref, kn_ref, vp_ref, vc_ref, vn_ref,
              cp_ref, sp_ref, cc_ref, sc_ref, cn_ref, sn_ref, sink_ref, o_ref):
    i = pl.program_id(0)
    blk = B_BLOCK
    nb = q_ref.shape[0] // blk
    has_prev = first_ref[i] == 0
    has_next = last_ref[i] == 0
    k_all = jnp.concatenate([_rope(kp_ref[...], cp_ref[...], sp_ref[...]),
                             _rope(kc_ref[...], cc_ref[...], sc_ref[...]),
                             _rope(kn_ref[...], cn_ref[...], sn_ref[...])], axis=0)
    v_all = jnp.concatenate([vp_ref[...], vc_ref[...], vn_ref[...]], axis=0)
    nq = B_GROUP * blk
    kpos = _iota2((3 * blk, nq), 0)
    qpos = _iota2((3 * blk, nq), 1) % blk
    rel = kpos - qpos
    in_window = (rel >= 0) & (rel <= 2 * WINDOW)
    cos, sin_signed = cc_ref[...], sc_ref[...]
    scale = B_HEAD_DIM ** -0.5
    qr = [_rope(q_ref[:, p * LANES:(p + 1) * LANES], cos, sin_signed) * scale for p in range(B_Q_HEADS // 2)]
    sink_row = sink_ref[0:1, :]
    chains = [(b, kv) for b in range(nb) for kv in range(B_KV_HEADS)]
    q4, sink, kh, vh, ok = [], [], [], [], []
    for b, kv in chains:
        rows = slice(b * blk, (b + 1) * blk)
        heads = [qr[hq // 2][rows, (hq % 2) * B_HEAD_DIM:(hq % 2 + 1) * B_HEAD_DIM]
                 for hq in range(kv * B_GROUP, (kv + 1) * B_GROUP)]
        q4.append(jnp.concatenate(heads, axis=0))
        sink.append(jnp.concatenate([jnp.broadcast_to(sink_row[:, kv * B_GROUP + g:kv * B_GROUP + g + 1], (1, blk))
                                     for g in range(B_GROUP)], axis=1))
        kh.append(k_all[b * blk:(b + 3) * blk, kv * B_HEAD_DIM:(kv + 1) * B_HEAD_DIM])
        vh.append(v_all[b * blk:(b + 3) * blk, kv * B_HEAD_DIM:(kv + 1) * B_HEAD_DIM])
        m_ok = in_window
        if b == 0:
            m_ok = m_ok & (has_prev | (kpos >= blk))
        if b == nb - 1:
            m_ok = m_ok & (has_next | (kpos < 2 * blk))
        ok.append(m_ok)
    cs = range(len(chains))
    s = [jnp.where(ok[c], _dot_nt(kh[c], q4[c]), -jnp.inf) for c in cs]
    m = [jnp.maximum(jnp.max(s[c], axis=0, keepdims=True), sink[c]) for c in cs]
    p = [jnp.exp(s[c] - m[c]) for c in cs]
    denom = [jnp.sum(p[c], axis=0, keepdims=True) + jnp.exp(sink[c] - m[c]) for c in cs]
    o_t = [_dot_tn(vh[c], p[c]) / denom[c] for c in cs]
    for c, (b, kv) in enumerate(chains):
        for g in range(0, B_GROUP, 2):
            hq = kv * B_GROUP + g
            pair = jnp.concatenate([o_t[c][:, g * blk:(g + 1) * blk],
                                    o_t[c][:, (g + 1) * blk:(g + 2) * blk]], axis=0)
            o_ref[b * blk:(b + 1) * blk, hq * B_HEAD_DIM:(hq + 2) * B_HEAD_DIM] = (
                jnp.transpose(pair).astype(o_ref.dtype))


def _swa(h, cos_t, sin_t, sink, first, last, pos, nb):
    t = h.shape[0]
    rows = nb * B_BLOCK
    n = t // rows
    n128 = t // B_BLOCK
    npos = cos_t.shape[0] // B_BLOCK
    kcol, vcol = COL_B_K // LANES, COL_B_V // LANES
    before = lambda colblk: pl.BlockSpec((B_BLOCK, LANES), lambda i, a, b, p: (jnp.maximum(i * nb - 1, 0), colblk))
    own = lambda colblk: pl.BlockSpec((rows, LANES), lambda i, a, b, p: (i, colblk))
    after = lambda colblk: pl.BlockSpec((B_BLOCK, LANES),
                                        lambda i, a, b, p: (jnp.minimum((i + 1) * nb, n128 - 1), colblk))
    t_before = pl.BlockSpec((B_BLOCK, LANES), lambda i, a, b, p: (jnp.maximum(p[i] * nb - 1, 0), 0))
    t_own = pl.BlockSpec((rows, LANES), lambda i, a, b, p: (p[i], 0))
    t_after = pl.BlockSpec((B_BLOCK, LANES), lambda i, a, b, p: (jnp.minimum((p[i] + 1) * nb, npos - 1), 0))
    qw = B_Q_HEADS * B_HEAD_DIM
    return pl.pallas_call(
        _swa_body,
        grid_spec=pltpu.PrefetchScalarGridSpec(
            num_scalar_prefetch=3,
            grid=(n,),
            in_specs=[pl.BlockSpec((rows, qw), lambda i, a, b, p: (i, COL_B_Q // qw)),
                      before(kcol), own(kcol), after(kcol), before(vcol), own(vcol), after(vcol),
                      t_before, t_before, t_own, t_own, t_after, t_after,
                      pl.BlockSpec((8, LANES), lambda i, a, b, p: (0, 0))],
            out_specs=pl.BlockSpec((rows, qw), lambda i, a, b, p: (i, 0))),
        out_shape=jax.ShapeDtypeStruct((t, qw), BRANCH_DTYPE),
        compiler_params=_params(("parallel",)),
        name="window_attn",
    )(first, last, pos, h, h, h, h, h, h, h, cos_t, sin_t, cos_t, sin_t, cos_t, sin_t, sink)


def _residual_ln(x, y, g, b):
    z = DN_ALPHA * x + y
    mu = jnp.mean(z, axis=1, keepdims=True)
    zc = z - mu
    var = jnp.mean(zc * zc, axis=1, keepdims=True)
    return zc * lax.rsqrt(var + 1e-5) * g + b


def _gated_rms(o, gate, g):
    outs = []
    for h in range(HEADS):
        sl = slice(h * HEAD_DIM, (h + 1) * HEAD_DIM)
        x = o[:, sl]
        x = x * lax.rsqrt(jnp.mean(x * x, axis=1, keepdims=True) + 1e-6) * g
        outs.append(x * _silu(gate[:, sl]))
    return jnp.concatenate(outs, axis=1)


def _merge_body(x_ref, af_ref, ab_ref, ag_ref, ob_ref, cf_ref, cb_ref, cg_ref, ma_ref, mb_ref, mc_ref,
                wa_ref, wb_ref, wc_ref, wo_ref, nrm_ref, ln_ref, o_ref):
    up = lambda ref: ref[...].astype(_f32)
    oa = _gated_rms(up(af_ref) + up(ab_ref), ag_ref[...], nrm_ref[0:1, :])
    oc = _gated_rms(up(cf_ref) + up(cb_ref), cg_ref[...], nrm_ref[1:2, :])
    pa = jnp.dot(oa.astype(_bf16), wa_ref[...], preferred_element_type=_f32)
    pb = jnp.dot(ob_ref[...].astype(_bf16), wb_ref[...], preferred_element_type=_f32)
    pc = jnp.dot(oc.astype(_bf16), wc_ref[...], preferred_element_type=_f32)
    mix = jax.nn.sigmoid(up(ma_ref)) * pa + jax.nn.sigmoid(up(mb_ref)) * pb + jax.nn.sigmoid(up(mc_ref)) * pc
    y = jnp.dot(mix.astype(_bf16), wo_ref[...], preferred_element_type=_f32)
    o_ref[...] = _residual_ln(x_ref[...], y, ln_ref[0:1, :], ln_ref[1:2, :])


def _merge(x, ha, gates, oa_f, oa_b, ob, oc_f, oc_b, wa, wb, wc, wo, nrm, ln, tm):
    t = x.shape[0]
    row = lambda w, colblk=0: pl.BlockSpec((tm, w), lambda i: (i, colblk))
    full = lambda a: pl.BlockSpec(a.shape, lambda i: (0, 0))
    mcol = 0
    return pl.pallas_call(
        _merge_body,
        grid=(t // tm,),
        in_specs=[row(D_MODEL), row(WIDTH), row(WIDTH), row(WIDTH, COL_A_G // WIDTH), row(WIDTH),
                  row(WIDTH), row(WIDTH), row(WIDTH, COL_C_GATE // WIDTH),
                  row(D_MODEL, mcol), row(D_MODEL, mcol + 1), row(D_MODEL, mcol + 2),
                  full(wa), full(wb), full(wc), full(wo), full(nrm), full(ln)],
        out_specs=row(D_MODEL),
        out_shape=jax.ShapeDtypeStruct((t, D_MODEL), _f32),
        compiler_params=_params(("parallel",)),
        name="merge_mix",
    )(x, oa_f, oa_b, ha, ob, oc_f, oc_b, ha, gates, gates, gates, wa, wb, wc, wo, nrm, ln)


def _xattn_body(seq_ref, x_ref, kv_ref, wq_ref, wo_ref, ln_ref, o_ref):
    x = x_ref[...]
    q = jnp.dot(x.astype(_bf16), wq_ref[...], preferred_element_type=_f32)
    hs = range(MEM_HEADS)
    cols = [slice(h * MEM_HEAD_DIM, (h + 1) * MEM_HEAD_DIM) for h in hs]
    s = [_dot_nt(q[:, cols[h]], kv_ref[0, :, cols[h]]) * (MEM_HEAD_DIM ** -0.5) for h in hs]
    p = [jnp.exp(s[h] - jnp.max(s[h], axis=1, keepdims=True)) for h in hs]
    p = [p[h] / jnp.sum(p[h], axis=1, keepdims=True) for h in hs]
    o = jnp.concatenate([_dot(p[h], kv_ref[0, :, D_MODEL + h * MEM_HEAD_DIM:D_MODEL + (h + 1) * MEM_HEAD_DIM])
                         for h in hs], axis=1)
    y = jnp.dot(o.astype(_bf16), wo_ref[...], preferred_element_type=_f32)
    o_ref[...] = _residual_ln(x, y, ln_ref[0:1, :], ln_ref[1:2, :])


def _xattn(x, kv, wq, wo, ln, seq_of_tile, tm):
    t = x.shape[0]
    full = lambda a: pl.BlockSpec(a.shape, lambda i, s: (0, 0))
    return pl.pallas_call(
        _xattn_body,
        grid_spec=pltpu.PrefetchScalarGridSpec(
            num_scalar_prefetch=1,
            grid=(t // tm,),
            in_specs=[pl.BlockSpec((tm, D_MODEL), lambda i, s: (i, 0)),
                      pl.BlockSpec((1,) + kv.shape[1:], lambda i, s: (s[i], 0, 0)),
                      full(wq), full(wo), full(ln)],
            out_specs=pl.BlockSpec((tm, D_MODEL), lambda i, s: (i, 0))),
        out_shape=jax.ShapeDtypeStruct((t, D_MODEL), _f32),
        compiler_params=_params(("parallel",)),
        name="mem_xattn",
    )(seq_of_tile, x, kv, wq, wo, ln)


def _ffn_tile(x_ref, w1_ref, w2_ref, ln_ref):
    x = x_ref[...]
    xb = x.astype(_bf16)
    gate = jnp.dot(xb, w1_ref[0], preferred_element_type=_f32)
    up = jnp.dot(xb, w1_ref[1], preferred_element_type=_f32)
    y = jnp.dot((_silu(gate) * up).astype(_bf16), w2_ref[...], preferred_element_type=_f32)
    return _residual_ln(x, y, ln_ref[0:1, :], ln_ref[1:2, :])


def _ffn_body(x_ref, w1_ref, w2_ref, ln_ref, o_ref):
    o_ref[...] = _ffn_tile(x_ref, w1_ref, w2_ref, ln_ref)


def _ffn_split_body(x_ref, w1_ref, w2_ref, ln_ref, head_ref, tail_ref, *, n_head):
    i = pl.program_id(0)
    y = _ffn_tile(x_ref, w1_ref, w2_ref, ln_ref)

    @pl.when(i < n_head)
    def _():
        head_ref[...] = y

    @pl.when(i >= n_head)
    def _():
        tail_ref[...] = y


def _ffn(x, w1, w2, ln, tm, split_rows=None):
    t = x.shape[0]
    ins = [pl.BlockSpec((tm, D_MODEL), lambda i: (i, 0)),
           _resident(w1.shape), _resident(w2.shape), _resident(ln.shape)]
    if split_rows is None:
        return pl.pallas_call(
            _ffn_body,
            grid=(t // tm,),
            in_specs=ins,
            out_specs=pl.BlockSpec((tm, D_MODEL), lambda i: (i, 0)),
            out_shape=jax.ShapeDtypeStruct((t, D_MODEL), _f32),
            compiler_params=_params(("parallel",)),
            name="swiglu_ffn",
        )(x, w1, w2, ln)
    n_head = split_rows // tm
    return pl.pallas_call(
        functools.partial(_ffn_split_body, n_head=n_head),
        grid=(t // tm,),
        in_specs=ins,
        out_specs=[pl.BlockSpec((tm, D_MODEL), lambda i: (jnp.minimum(i, n_head - 1), 0)),
                   pl.BlockSpec((tm, D_MODEL), lambda i: (jnp.maximum(i - n_head, 0), 0))],
        out_shape=[jax.ShapeDtypeStruct((split_rows, D_MODEL), _f32),
                   jax.ShapeDtypeStruct((t - split_rows, D_MODEL), _f32)],
        compiler_params=_params(("arbitrary",)),
        name="swiglu_ffn_split",
    )(x, w1, w2, ln)


def _boundary_tables(seq_lens, tile):
    first, last, pos, seq = [], [], [], []
    for sid, length in enumerate(seq_lens):
        n = length // tile
        for b in range(n):
            first.append(int(b == 0))
            last.append(int(b == n - 1))
            pos.append(b)
            seq.append(sid)
    as_i32 = lambda v: jnp.asarray(np.asarray(v, np.int32))
    return as_i32(first), as_i32(last), as_i32(pos), as_i32(seq)


def _split_in_cols(w):
    parts = [w[..., 0:3072], w[..., 4864:5376], w[..., 3072:3200], w[..., 3200:3328], w[..., 5376:5392]]
    pad = jnp.zeros(w.shape[:-1] + (HA_COLS - sum(p.shape[-1] for p in parts),), w.dtype)
    wa = jnp.concatenate(parts + [pad], axis=-1)
    wb = jnp.concatenate([w[..., 5392:8464], w[..., 3328:4864]], axis=-1)
    return wa, wb


def _rows8(*rows):
    n = rows[0].shape[-1]
    out = jnp.zeros((8, n), _f32)
    for r, v in enumerate(rows):
        out = out.at[r].set(v.astype(_f32))
    return out


def _lane_pad(v, offset):
    return jnp.zeros((LANES,), _f32).at[offset:offset + v.shape[0]].set(v.astype(_f32))


def kernel(x_prompt, x_sample, mem_prompt, mem_sample, w_in, hgrn_lb_logits, hgrn_norm_g, attn_sink,
           gdn_conv_w, gdn_a_log, gdn_dt_bias, gdn_norm_g, w_branch_a, w_branch_b, w_branch_c, w_mix_out,
           w_mem_q, w_mem_kv, w_mem_o, w_ffn_in, w_ffn_out, ln_g, ln_b):
    depth = w_in.shape[0]
    d = x_prompt.shape[-1]
    seq_lens = (x_prompt.shape[1],) * x_prompt.shape[0] + (x_sample.shape[1],) * x_sample.shape[0]
    n_prompt = x_prompt.shape[0] * x_prompt.shape[1]
    x = jnp.concatenate([x_prompt.reshape(-1, d), x_sample.reshape(-1, d)], axis=0)
    mem = jnp.concatenate([mem_prompt, mem_sample], axis=0)
    n_seq, n_mem, _ = mem.shape
    t = x.shape[0]

    tm = math.gcd(512, *seq_lens)
    tm_proj = math.gcd(512, t)
    rows_a = math.gcd(HGRN_CHUNKS_PER_STEP * CHUNK, *seq_lens)
    first_a, last_a, _, _ = _boundary_tables(seq_lens, rows_a)
    last_a_rev = last_a[::-1]
    rows_g = math.gcd(GDN_CHUNKS_PER_STEP * CHUNK, *seq_lens)
    first_g, last_g, _, _ = _boundary_tables(seq_lens, rows_g)
    last_g_rev = last_g[::-1]
    nb_swa = math.gcd(SWA_BLOCKS_PER_STEP * B_BLOCK, *seq_lens) // B_BLOCK
    first_w, last_w, pos_w, _ = _boundary_tables(seq_lens, nb_swa * B_BLOCK)
    first_t, last_t, _, seq_t = _boundary_tables(seq_lens, tm)

    s_max = max(seq_lens)
    inv = ROPE_THETA ** (-jnp.arange(0, B_HEAD_DIM, 2, dtype=_f32) / B_HEAD_DIM)
    ang = jnp.arange(s_max, dtype=_f32)[:, None] * inv[None, :]
    cos_t = jnp.tile(jnp.cos(ang), (1, 4))
    sin_t = jnp.tile(jnp.concatenate([-jnp.sin(ang), jnp.sin(ang)], axis=1), (1, 2))

    cum = jnp.cumsum(jax.nn.softmax(hgrn_lb_logits.astype(_f32), axis=1), axis=1)
    lb = cum - cum[:, :1]

    w_in_a, w_in_b = (w.astype(_bf16) for w in _split_in_cols(w_in))
    bf = lambda w: w.astype(_bf16)
    w_a, w_b, w_c, w_mix = bf(w_branch_a), bf(w_branch_b), bf(w_branch_c), bf(w_mix_out)
    w_q, w_kv, w_o = bf(w_mem_q), bf(w_mem_kv), bf(w_mem_o)
    w_f1, w_f2 = bf(w_ffn_in), bf(w_ffn_out)
    mem2 = mem.reshape(n_seq * n_mem, d)

    for l in range(depth):
        lb_rows = []
        for dirn in range(2):
            lbd = lb[dirn, l]
            lb_rows += [jnp.log(lbd), jnp.log1p(-lbd)]
        (ha,) = _in_proj(_in_proj_a_body, "in_proj_a", x, w_in_a[l], [_rows8(*lb_rows)], tm_proj,
                         [(HA_COLS, _f32)])
        gates, qkv_raw = _in_proj(_in_proj_b_body, "in_proj_b", x, w_in_b[l], [], tm_proj,
                                  [(3 * d, GATE_DTYPE), (3 * WIDTH, _f32)])
        oa_f, oa_b = _hgrn(ha, first_a, last_a_rev, rows_a)
        o_b = _swa(ha, cos_t, sin_t, _rows8(_lane_pad(attn_sink[l], 0)), first_w, last_w, pos_w, nb_swa)
        qkv = _gdn_prep(qkv_raw, _rows8(*[gdn_conv_w[l, j] for j in range(CONV_K)]), first_t, last_t, tm)
        gconst = _rows8(_lane_pad(-jnp.exp(gdn_a_log[l].astype(_f32)).reshape(-1), 2 * HEADS),
                        _lane_pad(gdn_dt_bias[l].reshape(-1), 2 * HEADS))
        oc_f, oc_b = _gdn(qkv, ha, gconst, first_g, last_g_rev, rows_g)
        nrm = _rows8(hgrn_norm_g[l], gdn_norm_g[l])
        x = _merge(x, ha, gates, oa_f, oa_b, o_b, oc_f, oc_b, w_a[l], w_b[l], w_c[l], w_mix[l], nrm,
                   _rows8(ln_g[l, 0], ln_b[l, 0]), tm)
        kv = _matmul(mem2, _col_tiles(w_kv[l], 512), n_mem).reshape(n_seq, n_mem, 2 * d)
        x = _xattn(x, kv, w_q[l], w_o[l], _rows8(ln_g[l, 1], ln_b[l, 1]), seq_t, tm)
        x = _ffn(x, _col_tiles(w_f1[l], FFN_HIDDEN), w_f2[l], _rows8(ln_g[l, 2], ln_b[l, 2]), tm,
                 split_rows=n_prompt if l == depth - 1 else None)

    y_prompt, y_sample = x
    return (y_prompt.reshape(x_prompt.shape), y_sample.reshape(x_sample.shape))
```

```python
import functools
import math

import numpy as np
import jax
import jax.numpy as jnp
from jax import lax
from jax.experimental import pallas as pl
from jax.experimental.pallas import tpu as pltpu

D_MODEL = 1024
DEPTH = 4
HEADS = 4
HEAD_DIM = 128
WIDTH = HEADS * HEAD_DIM
B_Q_HEADS = 8
B_KV_HEADS = 2
B_GROUP = B_Q_HEADS // B_KV_HEADS
B_HEAD_DIM = 64
WINDOW = 128
B_BLOCK = 128
ROPE_THETA = 10000.0
CONV_K = 5
MEM_HEADS = 4
MEM_HEAD_DIM = D_MODEL // MEM_HEADS
FFN_HIDDEN = 2816
DN_ALPHA = (2 * DEPTH) ** 0.25

CHUNK = 64
GDN_CHUNKS_PER_STEP = 4
HGRN_CHUNKS_PER_STEP = 8
SWA_BLOCKS_PER_STEP = 8
LANES = 128
LOG2_E = math.log2(math.e)
VMEM_LIMIT = 56 * 1024 * 1024

COL_A_Q, COL_A_FF, COL_A_FB, COL_A_I, COL_A_G = 0, 512, 1024, 1536, 2048
COL_B_Q = 2560
COL_C_GATE = 3072
COL_B_K, COL_B_V = 3584, 3712
COL_GB = 3840
HA_COLS = 4096

_f32 = jnp.float32
_bf16 = jnp.bfloat16
BRANCH_DTYPE = _bf16
GATE_DTYPE = _bf16


def _dot(a, b):
    return jnp.dot(a.astype(_bf16), b.astype(_bf16), preferred_element_type=_f32)


def _dot_nt(a, b):
    return lax.dot_general(a.astype(_bf16), b.astype(_bf16), (((1,), (1,)), ((), ())),
                           preferred_element_type=_f32)


def _dot_tn(a, b):
    return lax.dot_general(a.astype(_bf16), b.astype(_bf16), (((0,), (0,)), ((), ())),
                           preferred_element_type=_f32)


def _split3(x):
    hi = x.astype(_bf16)
    r1 = x - hi.astype(_f32)
    mid = r1.astype(_bf16)
    lo = (r1 - mid.astype(_f32)).astype(_bf16)
    return hi, mid, lo


def _exact_dot(sel, parts):
    dot = lambda p: jnp.dot(sel, p, preferred_element_type=_f32)
    return dot(parts[0]) + dot(parts[1]) + dot(parts[2])


def _silu(x):
    return x * jax.nn.sigmoid(x)


def _iota2(shape, dim):
    return lax.broadcasted_iota(jnp.int32, shape, dim)


def _params(sem):
    return pltpu.CompilerParams(dimension_semantics=sem, vmem_limit_bytes=VMEM_LIMIT)


def _resident(shape):
    return pl.BlockSpec(shape, lambda *_: (0,) * len(shape), pipeline_mode=pl.Buffered(1))


class _TwoPartRows:
    def __init__(self, head_ref, tail_ref, n_head):
        self.head_ref, self.tail_ref, self.n_head = head_ref, tail_ref, n_head
        self.shape, self.dtype = head_ref.shape, head_ref.dtype

    def __getitem__(self, idx):
        return jnp.where(pl.program_id(0) < self.n_head, self.head_ref[idx], self.tail_ref[idx])


def _token_rows(x, tm):
    if not isinstance(x, tuple):
        return [pl.BlockSpec((tm, x.shape[1]), lambda i: (i, 0))], [x], lambda body: body
    head, tail = x
    n_head = head.shape[0] // tm
    specs = [pl.BlockSpec((tm, head.shape[1]), lambda i: (jnp.minimum(i, n_head - 1), 0)),
             pl.BlockSpec((tm, tail.shape[1]), lambda i: (jnp.maximum(i - n_head, 0), 0))]

    def wrap(body):
        return lambda head_ref, tail_ref, *rest: body(_TwoPartRows(head_ref, tail_ref, n_head), *rest)
    return specs, [head, tail], wrap


def _num_rows(x):
    return sum(part.shape[0] for part in x) if isinstance(x, tuple) else x.shape[0]


def _col_tiles(w, tn):
    k, n = w.shape
    return w.reshape(k, n // tn, tn).transpose(1, 0, 2)


def _matmul_body(x_ref, w_ref, o_ref, xb_ref):
    j = pl.program_id(1)

    @pl.when(j == 0)
    def _():
        xb_ref[...] = x_ref[...].astype(_bf16)

    o_ref[...] = jnp.dot(xb_ref[...], w_ref[j], preferred_element_type=_f32)


def _matmul(x, w_tiles, tm):
    t, k = x.shape
    nt, _, tn = w_tiles.shape
    n = nt * tn
    return pl.pallas_call(
        _matmul_body,
        grid=(t // tm, nt),
        in_specs=[pl.BlockSpec((tm, k), lambda i, j: (i, 0)), _resident(w_tiles.shape)],
        out_specs=pl.BlockSpec((tm, tn), lambda i, j: (i, j)),
        out_shape=jax.ShapeDtypeStruct((t, n), _f32),
        scratch_shapes=[pltpu.VMEM((tm, k), _bf16)],
        compiler_params=_params(("parallel", "arbitrary")),
        name="dense_proj",
    )(x, w_tiles)


def _in_proj_a_body(x_ref, w_ref, lb_ref, o_ref):
    xb = x_ref[...].astype(_bf16)
    proj = lambda lo, hi: jnp.dot(xb, w_ref[:, lo:hi], preferred_element_type=_f32)

    def log_f(d):
        def fn(z):
            log_sig = jnp.minimum(z, 0.0) - jnp.log(1.0 + jnp.exp(-jnp.abs(z)))
            log_lb = lb_ref[2 * d:2 * d + 1, :]
            b = lb_ref[2 * d + 1:2 * d + 2, :] + log_sig
            return jnp.maximum(log_lb, b) + jnp.log(1.0 + jnp.exp(-jnp.abs(log_lb - b)))
        return fn

    groups = [(COL_A_Q, COL_A_Q + WIDTH, lambda z: _silu(z) * (HEAD_DIM ** -0.5)),
              (COL_A_FF, COL_A_FF + WIDTH, log_f(0)),
              (COL_A_FB, COL_A_FB + WIDTH, log_f(1)),
              (COL_A_I, o_ref.shape[1], lambda z: z)]
    z_next = proj(groups[0][0], groups[0][1])
    for g, (lo, hi, fn) in enumerate(groups):
        z = z_next
        if g + 1 < len(groups):
            z_next = proj(groups[g + 1][0], groups[g + 1][1])
        o_ref[:, lo:hi] = fn(z)


def _in_proj_b_body(x_ref, w_ref, gates_ref, qkv_ref):
    xb = x_ref[...].astype(_bf16)
    split = gates_ref.shape[1]
    gates_ref[...] = jnp.dot(xb, w_ref[:, :split], preferred_element_type=_f32).astype(gates_ref.dtype)
    qkv_ref[...] = jnp.dot(xb, w_ref[:, split:], preferred_element_type=_f32)


def _in_proj(body, name, x, w, consts, tm, outs):
    t = _num_rows(x)
    x_specs, x_ops, wrap = _token_rows(x, tm)
    return pl.pallas_call(
        wrap(body),
        grid=(t // tm,),
        in_specs=x_specs + [_resident(w.shape)] + [_resident(c.shape) for c in consts],
        out_specs=[pl.BlockSpec((tm, width), lambda i: (i, 0)) for width, _ in outs],
        out_shape=[jax.ShapeDtypeStruct((t, width), dtype) for width, dtype in outs],
        compiler_params=_params(("parallel",)),
        name=name,
    )(*x_ops, w, *consts)


def _chunk_masks(c, reverse):
    row = _iota2((c, c), 0)
    col = _iota2((c, c), 1)
    if reverse:
        return col >= row, col > row
    return col <= row, col < row


def _level_mask(c, blk, reverse, reps=1):
    row = _iota2((c, reps * c), 0)
    col = _iota2((c, reps * c), 1) % c
    half = blk // 2
    same = (row // blk) == (col // blk)
    r_hi = (row % blk) >= half
    c_hi = (col % blk) >= half
    if reverse:
        return same & jnp.logical_not(r_hi) & c_hi
    return same & r_hi & jnp.logical_not(c_hi)


def _block_ref_rows(g, blk, reverse):
    c, n = g.shape
    idx = blk // 2 if reverse else blk // 2 - 1
    rows = max(blk, 8)
    g3 = g.reshape(c // rows, rows, n)
    pick = lambda r: jnp.broadcast_to(g3[:, r:r + 1, :], g3.shape)
    out = pick(idx)
    if blk < rows:
        sub = lax.broadcasted_iota(jnp.int32, g3.shape, 1)
        for b in range(1, rows // blk):
            out = jnp.where(sub >= b * blk, pick(b * blk + idx), out)
    return out.reshape(c, n)


def _levels(c):
    out, blk = [], c
    while blk >= 2:
        out.append(blk)
        blk //= 2
    return out


def _hgrn_body(first_f_ref, first_b_ref, q_f, lf_f, v_f, q_b, lf_b, v_b, o_f, o_b, st_ref):
    c = CHUNK
    step = pl.program_id(0)

    @pl.when(first_f_ref[step] == 1)
    def _():
        st_ref[0:HEADS] = jnp.zeros((HEADS, HEAD_DIM, HEAD_DIM), _f32)

    @pl.when(first_b_ref[step] == 1)
    def _():
        st_ref[HEADS:2 * HEADS] = jnp.zeros((HEADS, HEAD_DIM, HEAD_DIM), _f32)

    nsub = q_f.shape[0] // c
    row = _iota2((c, c), 0)
    col = _iota2((c, c), 1)
    eye = row == col
    blks = _levels(c)
    units = []
    for d, (q_ref, lf_ref, v_ref) in enumerate(((q_f, lf_f, v_f), (q_b, lf_b, v_b))):
        reverse = d == 1
        incl, _ = _chunk_masks(c, reverse)
        tri = incl.astype(_bf16)
        masks = [_level_mask(c, blk, reverse) for blk in blks]
        log_f = lf_ref[...]
        key_all = 1.0 - jnp.exp(log_f)
        rpos = _iota2((c, HEAD_DIM), 0)
        later = [((rpos % blk) < blk // 2) if reverse else ((rpos % blk) >= blk // 2) for blk in blks]
        sgn = [jnp.where(m, LOG2_E, -LOG2_E) for m in later]
        q_all = q_ref[...]
        v_all = v_ref[...]
        tot_row = 0 if reverse else c - 1
        for s in (range(nsub - 1, -1, -1) if reverse else range(nsub)):
            rs = slice(s * c, (s + 1) * c)
            gc_all = _exact_dot(tri, _split3(log_f[rs]))
            ref_all = [_block_ref_rows(gc_all, blk, reverse) for blk in blks]
            for h in range(HEADS):
                sl = slice(h * HEAD_DIM, (h + 1) * HEAD_DIM)
                units.append(dict(q=q_all[rs, sl], k=key_all[rs, sl], v=v_all[rs, sl], gc=gc_all[:, sl],
                                  refs=[r[:, sl] for r in ref_all], masks=masks, sgn=sgn, rows=rs,
                                  g_tot=gc_all[tot_row:tot_row + 1, sl]))

    scores = [jnp.where(eye, _dot_nt(u["q"], u["k"]), 0.0) for u in units]
    for l in range(len(blks)):
        e = [jnp.exp2((u["gc"] - u["refs"][l]) * u["sgn"][l]) for u in units]
        scores = [jnp.where(u["masks"][l], _dot_nt(u["q"] * e[i], u["k"] * e[i]), scores[i])
                  for i, u in enumerate(units)]
    intra = [_dot(scores[i], u["v"]) for i, u in enumerate(units)]
    qd = [u["q"] * jnp.exp(u["gc"]) for u in units]
    kd = [u["k"] * jnp.exp(u["g_tot"] - u["gc"]) for u in units]

    st = [st_ref[i] for i in range(2 * HEADS)]
    for slot in range(nsub):
        for d in range(2):
            for h in range(HEADS):
                i, j = (d * nsub + slot) * HEADS + h, d * HEADS + h
                out = intra[i] + _dot_nt(qd[i], st[j])
                st[j] = st[j] * jnp.exp(units[i]["g_tot"]) + _dot_tn(units[i]["v"], kd[i])
                (o_f, o_b)[d][units[i]["rows"], h * HEAD_DIM:(h + 1) * HEAD_DIM] = out.astype(o_f.dtype)
    for i in range(2 * HEADS):
        st_ref[i] = st[i]


def _hgrn(h, first_f, first_b, rows):
    t = h.shape[0]
    n = t // rows
    fwd = lambda col: pl.BlockSpec((rows, WIDTH), lambda i, a, b: (i, col // WIDTH))
    bwd = lambda col: pl.BlockSpec((rows, WIDTH), lambda i, a, b: (n - 1 - i, col // WIDTH))
    out = jax.ShapeDtypeStruct((t, WIDTH), BRANCH_DTYPE)
    return pl.pallas_call(
        _hgrn_body,
        grid_spec=pltpu.PrefetchScalarGridSpec(
            num_scalar_prefetch=2,
            grid=(n,),
            in_specs=[fwd(COL_A_Q), fwd(COL_A_FF), fwd(COL_A_I), bwd(COL_A_Q), bwd(COL_A_FB), bwd(COL_A_I)],
            out_specs=[fwd(0), bwd(0)],
            scratch_shapes=[pltpu.VMEM((2 * HEADS, HEAD_DIM, HEAD_DIM), _f32)]),
        out_shape=[out, out],
        compiler_params=_params(("arbitrary",)),
        name="hgrn",
    )(first_f, first_b, h, h, h, h, h, h)


def _gdn_prep_body(first_ref, last_ref, prev_ref, cur_ref, next_ref, w_ref, o_ref, ext_ref):
    i = pl.program_id(0)
    tm = cur_ref.shape[0]
    halo = CONV_K // 2
    ext_ref[0:8, :] = jnp.where(first_ref[i] == 1, 0.0, prev_ref[...])
    ext_ref[8:8 + tm, :] = cur_ref[...]
    ext_ref[8 + tm:16 + tm, :] = jnp.where(last_ref[i] == 1, 0.0, next_ref[...])
    acc = cur_ref[...] * w_ref[halo:halo + 1, :]
    for j in range(CONV_K):
        if j != halo:
            off = 8 - halo + j
            acc = acc + ext_ref[off:off + tm, :] * w_ref[j:j + 1, :]
    y = _silu(acc)
    for part in range(3):
        for h in range(HEADS):
            lo = part * WIDTH + h * HEAD_DIM
            x = y[:, lo:lo + HEAD_DIM]
            if part < 2:
                x = x * lax.rsqrt(jnp.sum(x * x, axis=1, keepdims=True) + 1e-6)
            if part == 0:
                x = x * (HEAD_DIM ** -0.5)
            o_ref[:, lo:lo + HEAD_DIM] = x


def _gdn_prep(h, conv_w, first, last, tm):
    t = h.shape[0]
    n = t // tm
    r8 = tm // 8
    cb = 0
    return pl.pallas_call(
        _gdn_prep_body,
        grid_spec=pltpu.PrefetchScalarGridSpec(
            num_scalar_prefetch=2,
            grid=(n,),
            in_specs=[pl.BlockSpec((8, 3 * WIDTH), lambda i, f, l: (jnp.maximum(i * r8 - 1, 0), cb)),
                      pl.BlockSpec((tm, 3 * WIDTH), lambda i, f, l: (i, cb)),
                      pl.BlockSpec((8, 3 * WIDTH), lambda i, f, l: (jnp.minimum((i + 1) * r8, n * r8 - 1), cb)),
                      pl.BlockSpec((8, 3 * WIDTH), lambda i, f, l: (0, 0))],
            out_specs=pl.BlockSpec((tm, 3 * WIDTH), lambda i, f, l: (i, 0)),
            scratch_shapes=[pltpu.VMEM((tm + 16, 3 * WIDTH), _f32)]),
        out_shape=jax.ShapeDtypeStruct((t, 3 * WIDTH), _f32),
        compiler_params=_params(("parallel",)),
        name="gdn_prep",
    )(first, last, h, h, h, conv_w)


def _gdn_body(first_f_ref, first_b_ref, q_f, k_f, v_f, gb_f, q_b, k_b, v_b, gb_b, cst_ref, o_f, o_b, st_ref):
    c = CHUNK
    step = pl.program_id(0)

    @pl.when(first_f_ref[step] == 1)
    def _():
        st_ref[0:HEADS] = jnp.zeros((HEADS, HEAD_DIM, HEAD_DIM), _f32)

    @pl.when(first_b_ref[step] == 1)
    def _():
        st_ref[HEADS:2 * HEADS] = jnp.zeros((HEADS, HEAD_DIM, HEAD_DIM), _f32)

    nsub = q_f.shape[0] // c
    row = _iota2((c, c), 0)
    col = _iota2((c, c), 1)
    eye = (row == col).astype(_f32)
    neg_a = cst_ref[0:1, :]
    dt_bias = cst_ref[1:2, :]
    sizes = _levels(c)[::-1]
    units = []
    for d, (q_ref, k_ref, v_ref, gb_ref) in enumerate(((q_f, k_f, v_f, gb_f), (q_b, k_b, v_b, gb_b))):
        reverse = d == 1
        incl, strict = _chunk_masks(c, reverse)
        tri = incl.astype(_bf16)
        masks = [_level_mask(c, blk, reverse) for blk in sizes]
        gb = gb_ref[...]
        beta_t = jax.nn.sigmoid(gb)
        zz = gb + dt_bias
        g_in = neg_a * (jnp.maximum(zz, 0.0) + jnp.log1p(jnp.exp(-jnp.abs(zz))))
        tot_row = 0 if reverse else c - 1
        for s in (range(nsub - 1, -1, -1) if reverse else range(nsub)):
            rs = slice(s * c, (s + 1) * c)
            gc_t = _exact_dot(tri, _split3(g_in[rs]))
            gc_tt = jnp.transpose(gc_t)
            for h in range(HEADS):
                sl = slice(h * HEAD_DIM, (h + 1) * HEAD_DIM)
                j = 2 * HEADS + d * HEADS + h
                g_col = gc_t[:, j:j + 1]
                decay = jnp.where(incl, jnp.exp(jnp.minimum(g_col - gc_tt[j:j + 1, :], 0.0)), 0.0)
                units.append(dict(q=q_ref[rs, sl], k=k_ref[rs, sl], v=v_ref[rs, sl], rows=rs,
                                  g_col=g_col, decay=decay, strict=strict, masks=masks,
                                  beta=beta_t[rs, d * HEADS + h:d * HEADS + h + 1],
                                  g_tot=gc_t[tot_row:tot_row + 1, j:j + 1]))

    k_b = [u["k"].astype(_bf16) for u in units]
    a = [jnp.where(u["strict"], _dot_nt(k_b[i], k_b[i]) * u["decay"] * u["beta"], 0.0) for i, u in enumerate(units)]
    qk = [_dot_nt(u["q"], k_b[i]) * u["decay"] for i, u in enumerate(units)]
    inv = [eye - jnp.where(u["masks"][0], a[i], 0.0) for i, u in enumerate(units)]
    for l in range(1, len(sizes)):
        inv_b = [m.astype(_bf16) for m in inv]
        t1 = [_dot(jnp.where(u["masks"][l], a[i], 0.0), inv_b[i]) for i, u in enumerate(units)]
        inv = [inv[i] - _dot(inv_b[i], t1[i]) for i in range(len(units))]
    inv_b = [m.astype(_bf16) for m in inv]
    e_g = [jnp.exp(u["g_col"]) for u in units]
    sol = [_dot(inv_b[i], jnp.concatenate([u["v"] * u["beta"], u["k"] * (u["beta"] * e_g[i])], axis=1))
           for i, u in enumerate(units)]
    uu = [m[:, :HEAD_DIM] for m in sol]
    ww = [m[:, HEAD_DIM:] for m in sol]
    qd = [u["q"] * e_g[i] for i, u in enumerate(units)]
    kd = [u["k"] * jnp.exp(u["g_tot"] - u["g_col"]) for u in units]
    qk_sol = [_dot(qk[i], sol[i]) for i in range(len(units))]
    out0 = [m[:, :HEAD_DIM] for m in qk_sol]
    q_eff = [(qd[i] - qk_sol[i][:, HEAD_DIM:]).astype(_bf16) for i in range(len(units))]
    kd_sol = [_dot_tn(kd[i], sol[i]) for i in range(len(units))]
    gain = [m[:, :HEAD_DIM] for m in kd_sol]
    trans = [m[:, HEAD_DIM:].astype(_bf16) for m in kd_sol]

    st = [st_ref[i] for i in range(2 * HEADS)]
    for slot in range(nsub):
        for d in range(2):
            for h in range(HEADS):
                i, j = (d * nsub + slot) * HEADS + h, d * HEADS + h
                st_b = st[j].astype(_bf16)
                out = out0[i] + _dot(q_eff[i], st_b)
                st[j] = st[j] * jnp.exp(units[i]["g_tot"]) - _dot(trans[i], st_b) + gain[i]
                (o_f, o_b)[d][units[i]["rows"], h * HEAD_DIM:(h + 1) * HEAD_DIM] = out.astype(o_f.dtype)
    for i in range(2 * HEADS):
        st_ref[i] = st[i]


def _gdn(qkv, h, consts, first_f, first_b, rows):
    t = qkv.shape[0]
    n = t // rows
    fwd = lambda w, colblk: pl.BlockSpec((rows, w), lambda i, a, b: (i, colblk))
    bwd = lambda w, colblk: pl.BlockSpec((rows, w), lambda i, a, b: (n - 1 - i, colblk))
    gcol = COL_GB // LANES
    out = jax.ShapeDtypeStruct((t, WIDTH), BRANCH_DTYPE)
    return pl.pallas_call(
        _gdn_body,
        grid_spec=pltpu.PrefetchScalarGridSpec(
            num_scalar_prefetch=2,
            grid=(n,),
            in_specs=[fwd(WIDTH, 0), fwd(WIDTH, 1), fwd(WIDTH, 2), fwd(LANES, gcol),
                      bwd(WIDTH, 0), bwd(WIDTH, 1), bwd(WIDTH, 2), bwd(LANES, gcol),
                      pl.BlockSpec((8, LANES), lambda i, a, b: (0, 0))],
            out_specs=[fwd(WIDTH, 0), bwd(WIDTH, 0)],
            scratch_shapes=[pltpu.VMEM((2 * HEADS, HEAD_DIM, HEAD_DIM), _f32)]),
        out_shape=[out, out],
        compiler_params=_params(("arbitrary",)),
        name="gdn",
    )(first_f, first_b, qkv, qkv, qkv, h, qkv, qkv, qkv, h, consts)


def _rope(x, cos, sin_signed):
    src = _iota2((LANES, LANES), 0)
    dst = _iota2((LANES, LANES), 1)
    half = B_HEAD_DIM // 2
    partner = jnp.where(dst % B_HEAD_DIM < half, dst + half, dst - half)
    rot = jnp.dot(x.astype(_bf16), (src == partner).astype(_bf16), preferred_element_type=_f32)
    return x * cos + rot * sin_signed


def _swa_body(first_ref, last_ref, pos_ref, q_ref, kp_ref, kc_ref, kn_ref, vp_ref, vc_ref, vn_ref,
              cp_ref, sp_ref, cc_ref, sc_ref, cn_ref, sn_ref, sink_ref, o_ref):
    i = pl.program_id(0)
    blk = B_BLOCK
    nb = q_ref.shape[0] // blk
    has_prev = first_ref[i] == 0
    has_next = last_ref[i] == 0
    k_all = jnp.concatenate([_rope(kp_ref[...], cp_ref[...], sp_ref[...]),
                             _rope(kc_ref[...], cc_ref[...], sc_ref[...]),
                             _rope(kn_ref[...], cn_ref[...], sn_ref[...])], axis=0)
    v_all = jnp.concatenate([vp_ref[...], vc_ref[...], vn_ref[...]], axis=0)
    nq = B_GROUP * blk
    kpos = _iota2((3 * blk, nq), 0)
    qpos = _iota2((3 * blk, nq), 1) % blk
    rel = kpos - qpos
    in_window = (rel >= 0) & (rel <= 2 * WINDOW)
    cos, sin_signed = cc_ref[...], sc_ref[...]
    scale = B_HEAD_DIM ** -0.5
    qr = [_rope(q_ref[:, p * LANES:(p + 1) * LANES], cos, sin_signed) * scale for p in range(B_Q_HEADS // 2)]
    sink_row = sink_ref[0:1, :]
    chains = [(b, kv) for b in range(nb) for kv in range(B_KV_HEADS)]
    q4, sink, kh, vh, ok = [], [], [], [], []
    for b, kv in chains:
        rows = slice(b * blk, (b + 1) * blk)
        heads = [qr[hq // 2][rows, (hq % 2) * B_HEAD_DIM:(hq % 2 + 1) * B_HEAD_DIM]
                 for hq in range(kv * B_GROUP, (kv + 1) * B_GROUP)]
        q4.append(jnp.concatenate(heads, axis=0))
        sink.append(jnp.concatenate([jnp.broadcast_to(sink_row[:, kv * B_GROUP + g:kv * B_GROUP + g + 1], (1, blk))
                                     for g in range(B_GROUP)], axis=1))
        kh.append(k_all[b * blk:(b + 3) * blk, kv * B_HEAD_DIM:(kv + 1) * B_HEAD_DIM])
        vh.append(v_all[b * blk:(b + 3) * blk, kv * B_HEAD_DIM:(kv + 1) * B_HEAD_DIM])
        m_ok = in_window
        if b == 0:
            m_ok = m_ok & (has_prev | (kpos >= blk))
        if b == nb - 1:
            m_ok = m_ok & (has_next | (kpos < 2 * blk))
        ok.append(m_ok)
    cs = range(len(chains))
    s = [jnp.where(ok[c], _dot_nt(kh[c], q4[c]), -jnp.inf) for c in cs]
    m = [jnp.maximum(jnp.max(s[c], axis=0, keepdims=True), sink[c]) for c in cs]
    p = [jnp.exp(s[c] - m[c]) for c in cs]
    denom = [jnp.sum(p[c], axis=0, keepdims=True) + jnp.exp(sink[c] - m[c]) for c in cs]
    o_t = [_dot_tn(vh[c], p[c]) / denom[c] for c in cs]
    for c, (b, kv) in enumerate(chains):
        for g in range(0, B_GROUP, 2):
            hq = kv * B_GROUP + g
            pair = jnp.concatenate([o_t[c][:, g * blk:(g + 1) * blk],
                                    o_t[c][:, (g + 1) * blk:(g + 2) * blk]], axis=0)
            o_ref[b * blk:(b + 1) * blk, hq * B_HEAD_DIM:(hq + 2) * B_HEAD_DIM] = (
                jnp.transpose(pair).astype(o_ref.dtype))


def _swa(h, cos_t, sin_t, sink, first, last, pos, nb):
    t = h.shape[0]
    rows = nb * B_BLOCK
    n = t // rows
    n128 = t // B_BLOCK
    npos = cos_t.shape[0] // B_BLOCK
    kcol, vcol = COL_B_K // LANES, COL_B_V // LANES
    before = lambda colblk: pl.BlockSpec((B_BLOCK, LANES), lambda i, a, b, p: (jnp.maximum(i * nb - 1, 0), colblk))
    own = lambda colblk: pl.BlockSpec((rows, LANES), lambda i, a, b, p: (i, colblk))
    after = lambda colblk: pl.BlockSpec((B_BLOCK, LANES),
                                        lambda i, a, b, p: (jnp.minimum((i + 1) * nb, n128 - 1), colblk))
    t_before = pl.BlockSpec((B_BLOCK, LANES), lambda i, a, b, p: (jnp.maximum(p[i] * nb - 1, 0), 0))
    t_own = pl.BlockSpec((rows, LANES), lambda i, a, b, p: (p[i], 0))
    t_after = pl.BlockSpec((B_BLOCK, LANES), lambda i, a, b, p: (jnp.minimum((p[i] + 1) * nb, npos - 1), 0))
    qw = B_Q_HEADS * B_HEAD_DIM
    return pl.pallas_call(
        _swa_body,
        grid_spec=pltpu.PrefetchScalarGridSpec(
            num_scalar_prefetch=3,
            grid=(n,),
            in_specs=[pl.BlockSpec((rows, qw), lambda i, a, b, p: (i, COL_B_Q // qw)),
                      before(kcol), own(kcol), after(kcol), before(vcol), own(vcol), after(vcol),
                      t_before, t_before, t_own, t_own, t_after, t_after,
                      pl.BlockSpec((8, LANES), lambda i, a, b, p: (0, 0))],
            out_specs=pl.BlockSpec((rows, qw), lambda i, a, b, p: (i, 0))),
        out_shape=jax.ShapeDtypeStruct((t, qw), BRANCH_DTYPE),
        compiler_params=_params(("parallel",)),
        name="window_attn",
    )(first, last, pos, h, h, h, h, h, h, h, cos_t, sin_t, cos_t, sin_t, cos_t, sin_t, sink)


def _residual_ln(x, y, g, b):
    z = DN_ALPHA * x + y
    mu = jnp.mean(z, axis=1, keepdims=True)
    zc = z - mu
    var = jnp.mean(zc * zc, axis=1, keepdims=True)
    return zc * lax.rsqrt(var + 1e-5) * g + b


def _gated_rms(o, gate, g):
    outs = []
    for h in range(HEADS):
        sl = slice(h * HEAD_DIM, (h + 1) * HEAD_DIM)
        x = o[:, sl]
        x = x * lax.rsqrt(jnp.mean(x * x, axis=1, keepdims=True) + 1e-6) * g
        outs.append(x * _silu(gate[:, sl]))
    return jnp.concatenate(outs, axis=1)


def _merge_body(x_ref, af_ref, ab_ref, ag_ref, ob_ref, cf_ref, cb_ref, cg_ref, ma_ref, mb_ref, mc_ref,
                wa_ref, wb_ref, wc_ref, wo_ref, nrm_ref, ln_ref, o_ref):
    up = lambda ref: ref[...].astype(_f32)
    oa = _gated_rms(up(af_ref) + up(ab_ref), ag_ref[...], nrm_ref[0:1, :])
    oc = _gated_rms(up(cf_ref) + up(cb_ref), cg_ref[...], nrm_ref[1:2, :])
    pa = jnp.dot(oa.astype(_bf16), wa_ref[...], preferred_element_type=_f32)
    pb = jnp.dot(ob_ref[...].astype(_bf16), wb_ref[...], preferred_element_type=_f32)
    pc = jnp.dot(oc.astype(_bf16), wc_ref[...], preferred_element_type=_f32)
    mix = jax.nn.sigmoid(up(ma_ref)) * pa + jax.nn.sigmoid(up(mb_ref)) * pb + jax.nn.sigmoid(up(mc_ref)) * pc
    y = jnp.dot(mix.astype(_bf16), wo_ref[...], preferred_element_type=_f32)
    o_ref[...] = _residual_ln(x_ref[...], y, ln_ref[0:1, :], ln_ref[1:2, :])


def _merge(x, ha, gates, oa_f, oa_b, ob, oc_f, oc_b, wa, wb, wc, wo, nrm, ln, tm):
    t = _num_rows(x)
    x_specs, x_ops, wrap = _token_rows(x, tm)
    row = lambda w, colblk=0: pl.BlockSpec((tm, w), lambda i: (i, colblk))
    full = lambda a: pl.BlockSpec(a.shape, lambda i: (0, 0))
    return pl.pallas_call(
        wrap(_merge_body),
        grid=(t // tm,),
        in_specs=x_specs + [row(WIDTH), row(WIDTH), row(WIDTH, COL_A_G // WIDTH), row(WIDTH),
                            row(WIDTH), row(WIDTH), row(WIDTH, COL_C_GATE // WIDTH),
                            row(D_MODEL, 0), row(D_MODEL, 1), row(D_MODEL, 2),
                            full(wa), full(wb), full(wc), full(wo), full(nrm), full(ln)],
        out_specs=row(D_MODEL),
        out_shape=jax.ShapeDtypeStruct((t, D_MODEL), _f32),
        compiler_params=_params(("parallel",)),
        name="merge_mix",
    )(*x_ops, oa_f, oa_b, ha, ob, oc_f, oc_b, ha, gates, gates, gates, wa, wb, wc, wo, nrm, ln)


def _xattn_body(seq_ref, x_ref, kv_ref, wq_ref, wo_ref, ln_ref, o_ref):
    x = x_ref[...]
    q = jnp.dot(x.astype(_bf16), wq_ref[...], preferred_element_type=_f32)
    hs = range(MEM_HEADS)
    cols = [slice(h * MEM_HEAD_DIM, (h + 1) * MEM_HEAD_DIM) for h in hs]
    s = [_dot_nt(q[:, cols[h]], kv_ref[0, :, cols[h]]) * (MEM_HEAD_DIM ** -0.5) for h in hs]
    p = [jnp.exp(s[h] - jnp.max(s[h], axis=1, keepdims=True)) for h in hs]
    p = [p[h] / jnp.sum(p[h], axis=1, keepdims=True) for h in hs]
    o = jnp.concatenate([_dot(p[h], kv_ref[0, :, D_MODEL + h * MEM_HEAD_DIM:D_MODEL + (h + 1) * MEM_HEAD_DIM])
                         for h in hs], axis=1)
    y = jnp.dot(o.astype(_bf16), wo_ref[...], preferred_element_type=_f32)
    o_ref[...] = _residual_ln(x, y, ln_ref[0:1, :], ln_ref[1:2, :])


def _xattn(x, kv, wq, wo, ln, seq_of_tile, tm):
    t = x.shape[0]
    full = lambda a: pl.BlockSpec(a.shape, lambda i, s: (0, 0))
    return pl.pallas_call(
        _xattn_body,
        grid_spec=pltpu.PrefetchScalarGridSpec(
            num_scalar_prefetch=1,
            grid=(t // tm,),
            in_specs=[pl.BlockSpec((tm, D_MODEL), lambda i, s: (i, 0)),
                      pl.BlockSpec((1,) + kv.shape[1:], lambda i, s: (s[i], 0, 0)),
                      full(wq), full(wo), full(ln)],
            out_specs=pl.BlockSpec((tm, D_MODEL), lambda i, s: (i, 0))),
        out_shape=jax.ShapeDtypeStruct((t, D_MODEL), _f32),
        compiler_params=_params(("parallel",)),
        name="mem_xattn",
    )(seq_of_tile, x, kv, wq, wo, ln)


def _ffn_tile(x_ref, w1_ref, w2_ref, ln_ref):
    x = x_ref[...]
    xb = x.astype(_bf16)
    gate = jnp.dot(xb, w1_ref[0], preferred_element_type=_f32)
    up = jnp.dot(xb, w1_ref[1], preferred_element_type=_f32)
    y = jnp.dot((_silu(gate) * up).astype(_bf16), w2_ref[...], preferred_element_type=_f32)
    return _residual_ln(x, y, ln_ref[0:1, :], ln_ref[1:2, :])


def _ffn_body(x_ref, w1_ref, w2_ref, ln_ref, o_ref):
    o_ref[...] = _ffn_tile(x_ref, w1_ref, w2_ref, ln_ref)


def _ffn_split_body(x_ref, w1_ref, w2_ref, ln_ref, head_ref, tail_ref, *, n_head):
    i = pl.program_id(0)
    y = _ffn_tile(x_ref, w1_ref, w2_ref, ln_ref)

    @pl.when(i < n_head)
    def _():
        head_ref[...] = y

    @pl.when(i >= n_head)
    def _():
        tail_ref[...] = y


def _ffn(x, w1, w2, ln, tm, split_rows=None):
    t = x.shape[0]
    ins = [pl.BlockSpec((tm, D_MODEL), lambda i: (i, 0)),
           _resident(w1.shape), _resident(w2.shape), _resident(ln.shape)]
    if split_rows is None:
        return pl.pallas_call(
            _ffn_body,
            grid=(t // tm,),
            in_specs=ins,
            out_specs=pl.BlockSpec((tm, D_MODEL), lambda i: (i, 0)),
            out_shape=jax.ShapeDtypeStruct((t, D_MODEL), _f32),
            compiler_params=_params(("parallel",)),
            name="swiglu_ffn",
        )(x, w1, w2, ln)
    n_head = split_rows // tm
    return pl.pallas_call(
        functools.partial(_ffn_split_body, n_head=n_head),
        grid=(t // tm,),
        in_specs=ins,
        out_specs=[pl.BlockSpec((tm, D_MODEL), lambda i: (jnp.minimum(i, n_head - 1), 0)),
                   pl.BlockSpec((tm, D_MODEL), lambda i: (jnp.maximum(i - n_head, 0), 0))],
        out_shape=[jax.ShapeDtypeStruct((split_rows, D_MODEL), _f32),
                   jax.ShapeDtypeStruct((t - split_rows, D_MODEL), _f32)],
        compiler_params=_params(("arbitrary",)),
        name="swiglu_ffn_split",
    )(x, w1, w2, ln)


def _boundary_tables(seq_lens, tile):
    first, last, pos, seq = [], [], [], []
    for sid, length in enumerate(seq_lens):
        n = length // tile
        for b in range(n):
            first.append(int(b == 0))
            last.append(int(b == n - 1))
            pos.append(b)
            seq.append(sid)
    as_i32 = lambda v: jnp.asarray(np.asarray(v, np.int32))
    return as_i32(first), as_i32(last), as_i32(pos), as_i32(seq)


def _split_in_cols(w):
    parts = [w[..., 0:3072], w[..., 4864:5376], w[..., 3072:3200], w[..., 3200:3328], w[..., 5376:5392]]
    pad = jnp.zeros(w.shape[:-1] + (HA_COLS - sum(p.shape[-1] for p in parts),), w.dtype)
    wa = jnp.concatenate(parts + [pad], axis=-1)
    wb = jnp.concatenate([w[..., 5392:8464], w[..., 3328:4864]], axis=-1)
    return wa, wb


def _rows8(*rows):
    n = rows[0].shape[-1]
    out = jnp.zeros((8, n), _f32)
    for r, v in enumerate(rows):
        out = out.at[r].set(v.astype(_f32))
    return out


def _lane_pad(v, offset):
    return jnp.zeros((LANES,), _f32).at[offset:offset + v.shape[0]].set(v.astype(_f32))


def kernel(x_prompt, x_sample, mem_prompt, mem_sample, w_in, hgrn_lb_logits, hgrn_norm_g, attn_sink,
           gdn_conv_w, gdn_a_log, gdn_dt_bias, gdn_norm_g, w_branch_a, w_branch_b, w_branch_c, w_mix_out,
           w_mem_q, w_mem_kv, w_mem_o, w_ffn_in, w_ffn_out, ln_g, ln_b):
    depth = w_in.shape[0]
    d = x_prompt.shape[-1]
    seq_lens = (x_prompt.shape[1],) * x_prompt.shape[0] + (x_sample.shape[1],) * x_sample.shape[0]
    n_prompt = x_prompt.shape[0] * x_prompt.shape[1]
    x = (x_prompt.reshape(-1, d), x_sample.reshape(-1, d))
    mem = jnp.concatenate([mem_prompt, mem_sample], axis=0)
    n_seq, n_mem, _ = mem.shape
    t = _num_rows(x)

    tm = math.gcd(512, *seq_lens)
    tm_proj = tm
    rows_a = math.gcd(HGRN_CHUNKS_PER_STEP * CHUNK, *seq_lens)
    first_a, last_a, _, _ = _boundary_tables(seq_lens, rows_a)
    last_a_rev = last_a[::-1]
    rows_g = math.gcd(GDN_CHUNKS_PER_STEP * CHUNK, *seq_lens)
    first_g, last_g, _, _ = _boundary_tables(seq_lens, rows_g)
    last_g_rev = last_g[::-1]
    nb_swa = math.gcd(SWA_BLOCKS_PER_STEP * B_BLOCK, *seq_lens) // B_BLOCK
    first_w, last_w, pos_w, _ = _boundary_tables(seq_lens, nb_swa * B_BLOCK)
    first_t, last_t, _, seq_t = _boundary_tables(seq_lens, tm)

    s_max = max(seq_lens)
    inv = ROPE_THETA ** (-jnp.arange(0, B_HEAD_DIM, 2, dtype=_f32) / B_HEAD_DIM)
    ang = jnp.arange(s_max, dtype=_f32)[:, None] * inv[None, :]
    cos_t = jnp.tile(jnp.cos(ang), (1, 4))
    sin_t = jnp.tile(jnp.concatenate([-jnp.sin(ang), jnp.sin(ang)], axis=1), (1, 2))

    cum = jnp.cumsum(jax.nn.softmax(hgrn_lb_logits.astype(_f32), axis=1), axis=1)
    lb = cum - cum[:, :1]

    w_in_a, w_in_b = (w.astype(_bf16) for w in _split_in_cols(w_in))
    bf = lambda w: w.astype(_bf16)
    w_a, w_b, w_c, w_mix = bf(w_branch_a), bf(w_branch_b), bf(w_branch_c), bf(w_mix_out)
    w_q, w_kv, w_o = bf(w_mem_q), bf(w_mem_kv), bf(w_mem_o)
    w_f1, w_f2 = bf(w_ffn_in), bf(w_ffn_out)
    mem2 = mem.reshape(n_seq * n_mem, d)

    for l in range(depth):
        lb_rows = []
        for dirn in range(2):
            lbd = lb[dirn, l]
            lb_rows += [jnp.log(lbd), jnp.log1p(-lbd)]
        (ha,) = _in_proj(_in_proj_a_body, "in_proj_a", x, w_in_a[l], [_rows8(*lb_rows)], tm_proj,
                         [(HA_COLS, _f32)])
        gates, qkv_raw = _in_proj(_in_proj_b_body, "in_proj_b", x, w_in_b[l], [], tm_proj,
                                  [(3 * d, GATE_DTYPE), (3 * WIDTH, _f32)])
        oa_f, oa_b = _hgrn(ha, first_a, last_a_rev, rows_a)
        o_b = _swa(ha, cos_t, sin_t, _rows8(_lane_pad(attn_sink[l], 0)), first_w, last_w, pos_w, nb_swa)
        qkv = _gdn_prep(qkv_raw, _rows8(*[gdn_conv_w[l, j] for j in range(CONV_K)]), first_t, last_t, tm)
        gconst = _rows8(_lane_pad(-jnp.exp(gdn_a_log[l].astype(_f32)).reshape(-1), 2 * HEADS),
                        _lane_pad(gdn_dt_bias[l].reshape(-1), 2 * HEADS))
        oc_f, oc_b = _gdn(qkv, ha, gconst, first_g, last_g_rev, rows_g)
        nrm = _rows8(hgrn_norm_g[l], gdn_norm_g[l])
        x = _merge(x, ha, gates, oa_f, oa_b, o_b, oc_f, oc_b, w_a[l], w_b[l], w_c[l], w_mix[l], nrm,
                   _rows8(ln_g[l, 0], ln_b[l, 0]), tm)
        kv = _matmul(mem2, _col_tiles(w_kv[l], 512), n_mem).reshape(n_seq, n_mem, 2 * d)
        x = _xattn(x, kv, w_q[l], w_o[l], _rows8(ln_g[l, 1], ln_b[l, 1]), seq_t, tm)
        x = _ffn(x, _col_tiles(w_f1[l], FFN_HIDDEN), w_f2[l], _rows8(ln_g[l, 2], ln_b[l, 2]), tm,
                 split_rows=n_prompt if l == depth - 1 else None)

    y_prompt, y_sample = x
    return (y_prompt.reshape(x_prompt.shape), y_sample.reshape(x_sample.shape))
```

```python
import functools
import math

import numpy as np
import jax
import jax.numpy as jnp
from jax import lax
from jax.experimental import pallas as pl
from jax.experimental.pallas import tpu as pltpu

D_MODEL = 1024
DEPTH = 4
HEADS = 4
HEAD_DIM = 128
WIDTH = HEADS * HEAD_DIM
B_Q_HEADS = 8
B_KV_HEADS = 2
B_GROUP = B_Q_HEADS // B_KV_HEADS
B_HEAD_DIM = 64
WINDOW = 128
B_BLOCK = 128
ROPE_THETA = 10000.0
CONV_K = 5
MEM_HEADS = 4
MEM_HEAD_DIM = D_MODEL // MEM_HEADS
FFN_HIDDEN = 2816
DN_ALPHA = (2 * DEPTH) ** 0.25

CHUNK = 64
GDN_CHUNKS_PER_STEP = 4
HGRN_CHUNKS_PER_STEP = 8
SWA_BLOCKS_PER_STEP = 8
LANES = 128
LOG2_E = math.log2(math.e)
VMEM_LIMIT = 56 * 1024 * 1024

COL_A_Q, COL_A_FF, COL_A_FB, COL_A_I, COL_A_G = 0, 512, 1024, 1536, 2048
COL_B_Q = 2560
COL_C_GATE = 3072
COL_B_K, COL_B_V = 3584, 3712
COL_GB = 3840
HA_COLS = 4096

_f32 = jnp.float32
_bf16 = jnp.bfloat16
BRANCH_DTYPE = _bf16
GATE_DTYPE = _bf16


def _dot(a, b):
    return jnp.dot(a.astype(_bf16), b.astype(_bf16), preferred_element_type=_f32)


def _dot_nt(a, b):
    return lax.dot_general(a.astype(_bf16), b.astype(_bf16), (((1,), (1,)), ((), ())),
                           preferred_element_type=_f32)


def _dot_tn(a, b):
    return lax.dot_general(a.astype(_bf16), b.astype(_bf16), (((0,), (0,)), ((), ())),
                           preferred_element_type=_f32)


def _split3(x):
    hi = x.astype(_bf16)
    r1 = x - hi.astype(_f32)
    mid = r1.astype(_bf16)
    lo = (r1 - mid.astype(_f32)).astype(_bf16)
    return hi, mid, lo


def _exact_dot(sel, parts):
    dot = lambda p: jnp.dot(sel, p, preferred_element_type=_f32)
    return dot(parts[0]) + dot(parts[1]) + dot(parts[2])


def _silu(x):
    return x * jax.nn.sigmoid(x)


def _iota2(shape, dim):
    return lax.broadcasted_iota(jnp.int32, shape, dim)


def _params(sem):
    return pltpu.CompilerParams(dimension_semantics=sem, vmem_limit_bytes=VMEM_LIMIT)


def _resident(shape):
    return pl.BlockSpec(shape, lambda *_: (0,) * len(shape), pipeline_mode=pl.Buffered(1))


class _TwoPartRows:
    def __init__(self, head_ref, tail_ref, n_head):
        self.head_ref, self.tail_ref, self.n_head = head_ref, tail_ref, n_head
        self.shape, self.dtype = head_ref.shape, head_ref.dtype

    def __getitem__(self, idx):
        return jnp.where(pl.program_id(0) < self.n_head, self.head_ref[idx], self.tail_ref[idx])


def _token_rows(x, tm):
    if not isinstance(x, tuple):
        return [pl.BlockSpec((tm, x.shape[1]), lambda i: (i, 0))], [x], lambda body: body
    head, tail = x
    n_head = head.shape[0] // tm
    specs = [pl.BlockSpec((tm, head.shape[1]), lambda i: (jnp.minimum(i, n_head - 1), 0)),
             pl.BlockSpec((tm, tail.shape[1]), lambda i: (jnp.maximum(i - n_head, 0), 0))]

    def wrap(body):
        return lambda head_ref, tail_ref, *rest: body(_TwoPartRows(head_ref, tail_ref, n_head), *rest)
    return specs, [head, tail], wrap


def _num_rows(x):
    return sum(part.shape[0] for part in x) if isinstance(x, tuple) else x.shape[0]


def _col_tiles(w, tn):
    k, n = w.shape
    return w.reshape(k, n // tn, tn).transpose(1, 0, 2)


def _matmul_body(x_ref, w_ref, o_ref, xb_ref):
    j = pl.program_id(1)

    @pl.when(j == 0)
    def _():
        xb_ref[...] = x_ref[...].astype(_bf16)

    o_ref[...] = jnp.dot(xb_ref[...], w_ref[j], preferred_element_type=_f32)


def _matmul(x, w_tiles, tm):
    t, k = x.shape
    nt, _, tn = w_tiles.shape
    n = nt * tn
    return pl.pallas_call(
        _matmul_body,
        grid=(t // tm, nt),
        in_specs=[pl.BlockSpec((tm, k), lambda i, j: (i, 0)), _resident(w_tiles.shape)],
        out_specs=pl.BlockSpec((tm, tn), lambda i, j: (i, j)),
        out_shape=jax.ShapeDtypeStruct((t, n), _f32),
        scratch_shapes=[pltpu.VMEM((tm, k), _bf16)],
        compiler_params=_params(("parallel", "arbitrary")),
        name="dense_proj",
    )(x, w_tiles)


def _in_proj_a_body(x_ref, w_ref, lb_ref, o_ref):
    xb = x_ref[...].astype(_bf16)
    proj = lambda lo, hi: jnp.dot(xb, w_ref[:, lo:hi], preferred_element_type=_f32)

    def log_f(d):
        def fn(z):
            log_sig = jnp.minimum(z, 0.0) - jnp.log(1.0 + jnp.exp(-jnp.abs(z)))
            log_lb = lb_ref[2 * d:2 * d + 1, :]
            b = lb_ref[2 * d + 1:2 * d + 2, :] + log_sig
            return jnp.maximum(log_lb, b) + jnp.log(1.0 + jnp.exp(-jnp.abs(log_lb - b)))
        return fn

    groups = [(COL_A_Q, COL_A_Q + WIDTH, lambda z: _silu(z) * (HEAD_DIM ** -0.5)),
              (COL_A_FF, COL_A_FF + WIDTH, log_f(0)),
              (COL_A_FB, COL_A_FB + WIDTH, log_f(1)),
              (COL_A_I, o_ref.shape[1], lambda z: z)]
    z_next = proj(groups[0][0], groups[0][1])
    for g, (lo, hi, fn) in enumerate(groups):
        z = z_next
        if g + 1 < len(groups):
            z_next = proj(groups[g + 1][0], groups[g + 1][1])
        o_ref[:, lo:hi] = fn(z)


def _in_proj_b_body(x_ref, w_ref, gates_ref, qkv_ref):
    xb = x_ref[...].astype(_bf16)
    split = gates_ref.shape[1]
    gates_ref[...] = jnp.dot(xb, w_ref[:, :split], preferred_element_type=_f32).astype(gates_ref.dtype)
    qkv_ref[...] = jnp.dot(xb, w_ref[:, split:], preferred_element_type=_f32)


def _in_proj(body, name, x, w, consts, tm, outs):
    t = _num_rows(x)
    x_specs, x_ops, wrap = _token_rows(x, tm)
    return pl.pallas_call(
        wrap(body),
        grid=(t // tm,),
        in_specs=x_specs + [_resident(w.shape)] + [_resident(c.shape) for c in consts],
        out_specs=[pl.BlockSpec((tm, width), lambda i: (i, 0)) for width, _ in outs],
        out_shape=[jax.ShapeDtypeStruct((t, width), dtype) for width, dtype in outs],
        compiler_params=_params(("parallel",)),
        name=name,
    )(*x_ops, w, *consts)


def _chunk_masks(c, reverse):
    row = _iota2((c, c), 0)
    col = _iota2((c, c), 1)
    if reverse:
        return col >= row, col > row
    return col <= row, col < row


def _level_mask(c, blk, reverse, reps=1):
    row = _iota2((c, reps * c), 0)
    col = _iota2((c, reps * c), 1) % c
    half = blk // 2
    same = (row // blk) == (col // blk)
    r_hi = (row % blk) >= half
    c_hi = (col % blk) >= half
    if reverse:
        return same & jnp.logical_not(r_hi) & c_hi
    return same & r_hi & jnp.logical_not(c_hi)


def _block_ref_rows(g, blk, reverse):
    c, n = g.shape
    idx = blk // 2 if reverse else blk // 2 - 1
    rows = max(blk, 8)
    g3 = g.reshape(c // rows, rows, n)
    pick = lambda r: jnp.broadcast_to(g3[:, r:r + 1, :], g3.shape)
    out = pick(idx)
    if blk < rows:
        sub = lax.broadcasted_iota(jnp.int32, g3.shape, 1)
        for b in range(1, rows // blk):
            out = jnp.where(sub >= b * blk, pick(b * blk + idx), out)
    return out.reshape(c, n)


def _levels(c):
    out, blk = [], c
    while blk >= 2:
        out.append(blk)
        blk //= 2
    return out


def _hgrn_body(first_f_ref, first_b_ref, q_f, lf_f, v_f, q_b, lf_b, v_b, o_f, o_b, st_ref):
    c = CHUNK
    step = pl.program_id(0)

    @pl.when(first_f_ref[step] == 1)
    def _():
        st_ref[0:HEADS] = jnp.zeros((HEADS, HEAD_DIM, HEAD_DIM), _f32)

    @pl.when(first_b_ref[step] == 1)
    def _():
        st_ref[HEADS:2 * HEADS] = jnp.zeros((HEADS, HEAD_DIM, HEAD_DIM), _f32)

    nsub = q_f.shape[0] // c
    row = _iota2((c, c), 0)
    col = _iota2((c, c), 1)
    eye = row == col
    blks = _levels(c)
    units = []
    for d, (q_ref, lf_ref, v_ref) in enumerate(((q_f, lf_f, v_f), (q_b, lf_b, v_b))):
        reverse = d == 1
        incl, _ = _chunk_masks(c, reverse)
        tri = incl.astype(_bf16)
        masks = [_level_mask(c, blk, reverse) for blk in blks]
        log_f = lf_ref[...]
        key_all = 1.0 - jnp.exp(log_f)
        rpos = _iota2((c, HEAD_DIM), 0)
        later = [((rpos % blk) < blk // 2) if reverse else ((rpos % blk) >= blk // 2) for blk in blks]
        sgn = [jnp.where(m, LOG2_E, -LOG2_E) for m in later]
        q_all = q_ref[...]
        v_all = v_ref[...]
        tot_row = 0 if reverse else c - 1
        for s in (range(nsub - 1, -1, -1) if reverse else range(nsub)):
            rs = slice(s * c, (s + 1) * c)
            gc_all = _exact_dot(tri, _split3(log_f[rs]))
            ref_all = [_block_ref_rows(gc_all, blk, reverse) for blk in blks]
            for h in range(HEADS):
                sl = slice(h * HEAD_DIM, (h + 1) * HEAD_DIM)
                units.append(dict(q=q_all[rs, sl], k=key_all[rs, sl], v=v_all[rs, sl], gc=gc_all[:, sl],
                                  refs=[r[:, sl] for r in ref_all], masks=masks, sgn=sgn, rows=rs,
                                  g_tot=gc_all[tot_row:tot_row + 1, sl]))

    scores = [jnp.where(eye, _dot_nt(u["q"], u["k"]), 0.0) for u in units]
    for l in range(len(blks)):
        e = [jnp.exp2((u["gc"] - u["refs"][l]) * u["sgn"][l]) for u in units]
        scores = [jnp.where(u["masks"][l], _dot_nt(u["q"] * e[i], u["k"] * e[i]), scores[i])
                  for i, u in enumerate(units)]
    intra = [_dot(scores[i], u["v"]) for i, u in enumerate(units)]
    qd = [u["q"] * jnp.exp(u["gc"]) for u in units]
    kd = [u["k"] * jnp.exp(u["g_tot"] - u["gc"]) for u in units]

    st = [st_ref[i] for i in range(2 * HEADS)]
    for slot in range(nsub):
        for d in range(2):
            for h in range(HEADS):
                i, j = (d * nsub + slot) * HEADS + h, d * HEADS + h
                out = intra[i] + _dot_nt(qd[i], st[j])
                st[j] = st[j] * jnp.exp(units[i]["g_tot"]) + _dot_tn(units[i]["v"], kd[i])
                (o_f, o_b)[d][units[i]["rows"], h * HEAD_DIM:(h + 1) * HEAD_DIM] = out.astype(o_f.dtype)
    for i in range(2 * HEADS):
        st_ref[i] = st[i]


def _hgrn(h, first_f, first_b, rows):
    t = h.shape[0]
    n = t // rows
    fwd = lambda col: pl.BlockSpec((rows, WIDTH), lambda i, a, b: (i, col // WIDTH))
    bwd = lambda col: pl.BlockSpec((rows, WIDTH), lambda i, a, b: (n - 1 - i, col // WIDTH))
    out = jax.ShapeDtypeStruct((t, WIDTH), BRANCH_DTYPE)
    return pl.pallas_call(
        _hgrn_body,
        grid_spec=pltpu.PrefetchScalarGridSpec(
            num_scalar_prefetch=2,
            grid=(n,),
            in_specs=[fwd(COL_A_Q), fwd(COL_A_FF), fwd(COL_A_I), bwd(COL_A_Q), bwd(COL_A_FB), bwd(COL_A_I)],
            out_specs=[fwd(0), bwd(0)],
            scratch_shapes=[pltpu.VMEM((2 * HEADS, HEAD_DIM, HEAD_DIM), _f32)]),
        out_shape=[out, out],
        compiler_params=_params(("arbitrary",)),
        name="hgrn",
    )(first_f, first_b, h, h, h, h, h, h)


def _gdn_prep_body(first_ref, last_ref, prev_ref, cur_ref, next_ref, w_ref, o_ref, ext_ref):
    i = pl.program_id(0)
    tm = cur_ref.shape[0]
    halo = CONV_K // 2
    ext_ref[0:8, :] = jnp.where(first_ref[i] == 1, 0.0, prev_ref[...])
    ext_ref[8:8 + tm, :] = cur_ref[...]
    ext_ref[8 + tm:16 + tm, :] = jnp.where(last_ref[i] == 1, 0.0, next_ref[...])
    acc = cur_ref[...] * w_ref[halo:halo + 1, :]
    for j in range(CONV_K):
        if j != halo:
            off = 8 - halo + j
            acc = acc + ext_ref[off:off + tm, :] * w_ref[j:j + 1, :]
    y = _silu(acc)
    for part in range(3):
        for h in range(HEADS):
            lo = part * WIDTH + h * HEAD_DIM
            x = y[:, lo:lo + HEAD_DIM]
            if part < 2:
                x = x * lax.rsqrt(jnp.sum(x * x, axis=1, keepdims=True) + 1e-6)
            if part == 0:
                x = x * (HEAD_DIM ** -0.5)
            o_ref[:, lo:lo + HEAD_DIM] = x


def _gdn_prep(h, conv_w, first, last, tm):
    t = h.shape[0]
    n = t // tm
    r8 = tm // 8
    cb = 0
    return pl.pallas_call(
        _gdn_prep_body,
        grid_spec=pltpu.PrefetchScalarGridSpec(
            num_scalar_prefetch=2,
            grid=(n,),
            in_specs=[pl.BlockSpec((8, 3 * WIDTH), lambda i, f, l: (jnp.maximum(i * r8 - 1, 0), cb)),
                      pl.BlockSpec((tm, 3 * WIDTH), lambda i, f, l: (i, cb)),
                      pl.BlockSpec((8, 3 * WIDTH), lambda i, f, l: (jnp.minimum((i + 1) * r8, n * r8 - 1), cb)),
                      pl.BlockSpec((8, 3 * WIDTH), lambda i, f, l: (0, 0))],
            out_specs=pl.BlockSpec((tm, 3 * WIDTH), lambda i, f, l: (i, 0)),
            scratch_shapes=[pltpu.VMEM((tm + 16, 3 * WIDTH), _f32)]),
        out_shape=jax.ShapeDtypeStruct((t, 3 * WIDTH), _f32),
        compiler_params=_params(("parallel",)),
        name="gdn_prep",
    )(first, last, h, h, h, conv_w)


def _gdn_body(first_f_ref, first_b_ref, q_f, k_f, v_f, gb_f, q_b, k_b, v_b, gb_b, cst_ref, o_f, o_b, st_ref):
    c = CHUNK
    step = pl.program_id(0)

    @pl.when(first_f_ref[step] == 1)
    def _():
        st_ref[0:HEADS] = jnp.zeros((HEADS, HEAD_DIM, HEAD_DIM), _f32)

    @pl.when(first_b_ref[step] == 1)
    def _():
        st_ref[HEADS:2 * HEADS] = jnp.zeros((HEADS, HEAD_DIM, HEAD_DIM), _f32)

    nsub = q_f.shape[0] // c
    row = _iota2((c, c), 0)
    col = _iota2((c, c), 1)
    eye = (row == col).astype(_f32)
    neg_a = cst_ref[0:1, :]
    dt_bias = cst_ref[1:2, :]
    sizes = _levels(c)[::-1]
    units = []
    for d, (q_ref, k_ref, v_ref, gb_ref) in enumerate(((q_f, k_f, v_f, gb_f), (q_b, k_b, v_b, gb_b))):
        reverse = d == 1
        incl, strict = _chunk_masks(c, reverse)
        tri = incl.astype(_bf16)
        masks = [_level_mask(c, blk, reverse) for blk in sizes]
        gb = gb_ref[...]
        beta_t = jax.nn.sigmoid(gb)
        zz = gb + dt_bias
        g_in = neg_a * (jnp.maximum(zz, 0.0) + jnp.log1p(jnp.exp(-jnp.abs(zz))))
        tot_row = 0 if reverse else c - 1
        for s in (range(nsub - 1, -1, -1) if reverse else range(nsub)):
            rs = slice(s * c, (s + 1) * c)
            gc_t = _exact_dot(tri, _split3(g_in[rs]))
            gc_tt = jnp.transpose(gc_t)
            for h in range(HEADS):
                sl = slice(h * HEAD_DIM, (h + 1) * HEAD_DIM)
                j = 2 * HEADS + d * HEADS + h
                g_col = gc_t[:, j:j + 1]
                decay = jnp.where(incl, jnp.exp(jnp.minimum(g_col - gc_tt[j:j + 1, :], 0.0)), 0.0)
                units.append(dict(q=q_ref[rs, sl], k=k_ref[rs, sl], v=v_ref[rs, sl], rows=rs,
                                  g_col=g_col, decay=decay, strict=strict, masks=masks,
                                  beta=beta_t[rs, d * HEADS + h:d * HEADS + h + 1],
                                  g_tot=gc_t[tot_row:tot_row + 1, j:j + 1]))

    k_b = [u["k"].astype(_bf16) for u in units]
    a = [jnp.where(u["strict"], _dot_nt(k_b[i], k_b[i]) * u["decay"] * u["beta"], 0.0) for i, u in enumerate(units)]
    qk = [_dot_nt(u["q"], k_b[i]) * u["decay"] for i, u in enumerate(units)]
    inv = [eye - jnp.where(u["masks"][0], a[i], 0.0) for i, u in enumerate(units)]
    for l in range(1, len(sizes)):
        inv_b = [m.astype(_bf16) for m in inv]
        t1 = [_dot(jnp.where(u["masks"][l], a[i], 0.0), inv_b[i]) for i, u in enumerate(units)]
        inv = [inv[i] - _dot(inv_b[i], t1[i]) for i in range(len(units))]
    inv_b = [m.astype(_bf16) for m in inv]
    e_g = [jnp.exp(u["g_col"]) for u in units]
    sol = [_dot(inv_b[i], jnp.concatenate([u["v"] * u["beta"], u["k"] * (u["beta"] * e_g[i])], axis=1))
           for i, u in enumerate(units)]
    uu = [m[:, :HEAD_DIM] for m in sol]
    ww = [m[:, HEAD_DIM:] for m in sol]
    qd = [u["q"] * e_g[i] for i, u in enumerate(units)]
    kd = [u["k"] * jnp.exp(u["g_tot"] - u["g_col"]) for u in units]
    qk_sol = [_dot(qk[i], sol[i]) for i in range(len(units))]
    out0 = [m[:, :HEAD_DIM] for m in qk_sol]
    q_eff = [(qd[i] - qk_sol[i][:, HEAD_DIM:]).astype(_bf16) for i in range(len(units))]
    kd_sol = [_dot_tn(kd[i], sol[i]) for i in range(len(units))]
    gain = [m[:, :HEAD_DIM] for m in kd_sol]
    trans = [m[:, HEAD_DIM:].astype(_bf16) for m in kd_sol]

    st = [st_ref[i] for i in range(2 * HEADS)]
    for slot in range(nsub):
        for d in range(2):
            for h in range(HEADS):
                i, j = (d * nsub + slot) * HEADS + h, d * HEADS + h
                st_b = st[j].astype(_bf16)
                out = out0[i] + _dot(q_eff[i], st_b)
                st[j] = st[j] * jnp.exp(units[i]["g_tot"]) - _dot(trans[i], st_b) + gain[i]
                (o_f, o_b)[d][units[i]["rows"], h * HEAD_DIM:(h + 1) * HEAD_DIM] = out.astype(o_f.dtype)
    for i in range(2 * HEADS):
        st_ref[i] = st[i]


def _gdn(qkv, h, consts, first_f, first_b, rows):
    t = qkv.shape[0]
    n = t // rows
    fwd = lambda w, colblk: pl.BlockSpec((rows, w), lambda i, a, b: (i, colblk))
    bwd = lambda w, colblk: pl.BlockSpec((rows, w), lambda i, a, b: (n - 1 - i, colblk))
    gcol = COL_GB // LANES
    out = jax.ShapeDtypeStruct((t, WIDTH), BRANCH_DTYPE)
    return pl.pallas_call(
        _gdn_body,
        grid_spec=pltpu.PrefetchScalarGridSpec(
            num_scalar_prefetch=2,
            grid=(n,),
            in_specs=[fwd(WIDTH, 0), fwd(WIDTH, 1), fwd(WIDTH, 2), fwd(LANES, gcol),
                      bwd(WIDTH, 0), bwd(WIDTH, 1), bwd(WIDTH, 2), bwd(LANES, gcol),
                      pl.BlockSpec((8, LANES), lambda i, a, b: (0, 0))],
            out_specs=[fwd(WIDTH, 0), bwd(WIDTH, 0)],
            scratch_shapes=[pltpu.VMEM((2 * HEADS, HEAD_DIM, HEAD_DIM), _f32)]),
        out_shape=[out, out],
        compiler_params=_params(("arbitrary",)),
        name="gdn",
    )(first_f, first_b, qkv, qkv, qkv, h, qkv, qkv, qkv, h, consts)


def _rope(x, cos, sin_signed):
    src = _iota2((LANES, LANES), 0)
    dst = _iota2((LANES, LANES), 1)
    half = B_HEAD_DIM // 2
    partner = jnp.where(dst % B_HEAD_DIM < half, dst + half, dst - half)
    rot = jnp.dot(x.astype(_bf16), (src == partner).astype(_bf16), preferred_element_type=_f32)
    return x * cos + rot * sin_signed


def _swa_body(first_ref, last_ref, pos_ref, q_ref, kp_ref, kc_ref, kn_ref, vp_ref, vc_ref, vn_ref,
              cp_ref, sp_ref, cc_ref, sc_ref, cn_ref, sn_ref, sink_ref, o_ref):
    i = pl.program_id(0)
    blk = B_BLOCK
    nb = q_ref.shape[0] // blk
    has_prev = first_ref[i] == 0
    has_next = last_ref[i] == 0
    k_all = jnp.concatenate([_rope(kp_ref[...], cp_ref[...], sp_ref[...]),
                             _rope(kc_ref[...], cc_ref[...], sc_ref[...]),
                             _rope(kn_ref[...], cn_ref[...], sn_ref[...])], axis=0)
    v_all = jnp.concatenate([vp_ref[...], vc_ref[...], vn_ref[...]], axis=0)
    nq = B_GROUP * blk
    kpos = _iota2((3 * blk, nq), 0)
    qpos = _iota2((3 * blk, nq), 1) % blk
    rel = kpos - qpos
    in_window = (rel >= 0) & (rel <= 2 * WINDOW)
    cos, sin_signed = cc_ref[...], sc_ref[...]
    scale = B_HEAD_DIM ** -0.5
    qr = [_rope(q_ref[:, p * LANES:(p + 1) * LANES], cos, sin_signed) * scale for p in range(B_Q_HEADS // 2)]
    sink_row = sink_ref[0:1, :]
    chains = [(b, kv) for b in range(nb) for kv in range(B_KV_HEADS)]
    q4, sink, kh, vh, ok = [], [], [], [], []
    for b, kv in chains:
        rows = slice(b * blk, (b + 1) * blk)
        heads = [qr[hq // 2][rows, (hq % 2) * B_HEAD_DIM:(hq % 2 + 1) * B_HEAD_DIM]
                 for hq in range(kv * B_GROUP, (kv + 1) * B_GROUP)]
        q4.append(jnp.concatenate(heads, axis=0))
        sink.append(jnp.concatenate([jnp.broadcast_to(sink_row[:, kv * B_GROUP + g:kv * B_GROUP + g + 1], (1, blk))
                                     for g in range(B_GROUP)], axis=1))
        kh.append(k_all[b * blk:(b + 3) * blk, kv * B_HEAD_DIM:(kv + 1) * B_HEAD_DIM])
        vh.append(v_all[b * blk:(b + 3) * blk, kv * B_HEAD_DIM:(kv + 1) * B_HEAD_DIM])
        m_ok = in_window
        if b == 0:
            m_ok = m_ok & (has_prev | (kpos >= blk))
        if b == nb - 1:
            m_ok = m_ok & (has_next | (kpos < 2 * blk))
        ok.append(m_ok)
    cs = range(len(chains))
    s = [jnp.where(ok[c], _dot_nt(kh[c], q4[c]), -jnp.inf) for c in cs]
    m = [jnp.maximum(jnp.max(s[c], axis=0, keepdims=True), sink[c]) for c in cs]
    p = [jnp.exp(s[c] - m[c]) for c in cs]
    denom = [jnp.sum(p[c], axis=0, keepdims=True) + jnp.exp(sink[c] - m[c]) for c in cs]
    o_t = [_dot_tn(vh[c], p[c]) / denom[c] for c in cs]
    for c, (b, kv) in enumerate(chains):
        for g in range(0, B_GROUP, 2):
            hq = kv * B_GROUP + g
            pair = jnp.concatenate([o_t[c][:, g * blk:(g + 1) * blk],
                                    o_t[c][:, (g + 1) * blk:(g + 2) * blk]], axis=0)
            o_ref[b * blk:(b + 1) * blk, hq * B_HEAD_DIM:(hq + 2) * B_HEAD_DIM] = (
                jnp.transpose(pair).astype(o_ref.dtype))


def _swa(h, cos_t, sin_t, sink, first, last, pos, nb):
    t = h.shape[0]
    rows = nb * B_BLOCK
    n = t // rows
    n128 = t // B_BLOCK
    npos = cos_t.shape[0] // B_BLOCK
    kcol, vcol = COL_B_K // LANES, COL_B_V // LANES
    before = lambda colblk: pl.BlockSpec((B_BLOCK, LANES), lambda i, a, b, p: (jnp.maximum(i * nb - 1, 0), colblk))
    own = lambda colblk: pl.BlockSpec((rows, LANES), lambda i, a, b, p: (i, colblk))
    after = lambda colblk: pl.BlockSpec((B_BLOCK, LANES),
                                        lambda i, a, b, p: (jnp.minimum((i + 1) * nb, n128 - 1), colblk))
    t_before = pl.BlockSpec((B_BLOCK, LANES), lambda i, a, b, p: (jnp.maximum(p[i] * nb - 1, 0), 0))
    t_own = pl.BlockSpec((rows, LANES), lambda i, a, b, p: (p[i], 0))
    t_after = pl.BlockSpec((B_BLOCK, LANES), lambda i, a, b, p: (jnp.minimum((p[i] + 1) * nb, npos - 1), 0))
    qw = B_Q_HEADS * B_HEAD_DIM
    return pl.pallas_call(
        _swa_body,
        grid_spec=pltpu.PrefetchScalarGridSpec(
            num_scalar_prefetch=3,
            grid=(n,),
            in_specs=[pl.BlockSpec((rows, qw), lambda i, a, b, p: (i, COL_B_Q // qw)),
                      before(kcol), own(kcol), after(kcol), before(vcol), own(vcol), after(vcol),
                      t_before, t_before, t_own, t_own, t_after, t_after,
                      pl.BlockSpec((8, LANES), lambda i, a, b, p: (0, 0))],
            out_specs=pl.BlockSpec((rows, qw), lambda i, a, b, p: (i, 0))),
        out_shape=jax.ShapeDtypeStruct((t, qw), BRANCH_DTYPE),
        compiler_params=_params(("parallel",)),
        name="window_attn",
    )(first, last, pos, h, h, h, h, h, h, h, cos_t, sin_t, cos_t, sin_t, cos_t, sin_t, sink)


def _residual_ln(x, y, g, b):
    z = DN_ALPHA * x + y
    mu = jnp.mean(z, axis=1, keepdims=True)
    zc = z - mu
    var = jnp.mean(zc * zc, axis=1, keepdims=True)
    return zc * lax.rsqrt(var + 1e-5) * g + b


def _gated_rms(o, gate, g):
    outs = []
    for h in range(HEADS):
        sl = slice(h * HEAD_DIM, (h + 1) * HEAD_DIM)
        x = o[:, sl]
        x = x * lax.rsqrt(jnp.mean(x * x, axis=1, keepdims=True) + 1e-6) * g
        outs.append(x * _silu(gate[:, sl]))
    return jnp.concatenate(outs, axis=1)


def _merge_body(x_ref, af_ref, ab_ref, ag_ref, ob_ref, cf_ref, cb_ref, cg_ref, ma_ref, mb_ref, mc_ref,
                wa_ref, wb_ref, wc_ref, wo_ref, nrm_ref, ln_ref, o_ref):
    up = lambda ref: ref[...].astype(_f32)
    oa = _gated_rms(up(af_ref) + up(ab_ref), ag_ref[...], nrm_ref[0:1, :])
    oc = _gated_rms(up(cf_ref) + up(cb_ref), cg_ref[...], nrm_ref[1:2, :])
    pa = jnp.dot(oa.astype(_bf16), wa_ref[...], preferred_element_type=_f32)
    pb = jnp.dot(ob_ref[...].astype(_bf16), wb_ref[...], preferred_element_type=_f32)
    pc = jnp.dot(oc.astype(_bf16), wc_ref[...], preferred_element_type=_f32)
    mix = jax.nn.sigmoid(up(ma_ref)) * pa + jax.nn.sigmoid(up(mb_ref)) * pb + jax.nn.sigmoid(up(mc_ref)) * pc
    y = jnp.dot(mix.astype(_bf16), wo_ref[...], preferred_element_type=_f32)
    o_ref[...] = _residual_ln(x_ref[...], y, ln_ref[0:1, :], ln_ref[1:2, :])


def _merge(x, ha, gates, oa_f, oa_b, ob, oc_f, oc_b, wa, wb, wc, wo, nrm, ln, tm):
    t = _num_rows(x)
    x_specs, x_ops, wrap = _token_rows(x, tm)
    row = lambda w, colblk=0: pl.BlockSpec((tm, w), lambda i: (i, colblk))
    full = lambda a: pl.BlockSpec(a.shape, lambda i: (0, 0))
    return pl.pallas_call(
        wrap(_merge_body),
        grid=(t // tm,),
        in_specs=x_specs + [row(WIDTH), row(WIDTH), row(WIDTH, COL_A_G // WIDTH), row(WIDTH),
                            row(WIDTH), row(WIDTH), row(WIDTH, COL_C_GATE // WIDTH),
                            row(D_MODEL, 0), row(D_MODEL, 1), row(D_MODEL, 2),
                            full(wa), full(wb), full(wc), full(wo), full(nrm), full(ln)],
        out_specs=row(D_MODEL),
        out_shape=jax.ShapeDtypeStruct((t, D_MODEL), _f32),
        compiler_params=_params(("parallel",)),
        name="merge_mix",
    )(*x_ops, oa_f, oa_b, ha, ob, oc_f, oc_b, ha, gates, gates, gates, wa, wb, wc, wo, nrm, ln)


def _xattn_body(seq_ref, x_ref, kv_ref, wq_ref, wo_ref, ln_ref, o_ref):
    x = x_ref[...]
    q = jnp.dot(x.astype(_bf16), wq_ref[...], preferred_element_type=_f32)
    hs = range(MEM_HEADS)
    cols = [slice(h * MEM_HEAD_DIM, (h + 1) * MEM_HEAD_DIM) for h in hs]
    s = [_dot_nt(q[:, cols[h]], kv_ref[0, :, cols[h]]) * (MEM_HEAD_DIM ** -0.5) for h in hs]
    p = [jnp.exp(s[h] - jnp.max(s[h], axis=1, keepdims=True)) for h in hs]
    p = [p[h] / jnp.sum(p[h], axis=1, keepdims=True) for h in hs]
    o = jnp.concatenate([_dot(p[h], kv_ref[0, :, D_MODEL + h * MEM_HEAD_DIM:D_MODEL + (h + 1) * MEM_HEAD_DIM])
                         for h in hs], axis=1)
    y = jnp.dot(o.astype(_bf16), wo_ref[...], preferred_element_type=_f32)
    o_ref[...] = _residual_ln(x, y, ln_ref[0:1, :], ln_ref[1:2, :])


def _xattn(x, kv, wq, wo, ln, seq_of_tile, tm):
    t = x.shape[0]
    full = lambda a: pl.BlockSpec(a.shape, lambda i, s: (0, 0))
    return pl.pallas_call(
        _xattn_body,
        grid_spec=pltpu.PrefetchScalarGridSpec(
            num_scalar_prefetch=1,
            grid=(t // tm,),
            in_specs=[pl.BlockSpec((tm, D_MODEL), lambda i, s: (i, 0)),
                      pl.BlockSpec((1,) + kv.shape[1:], lambda i, s: (s[i], 0, 0)),
                      full(wq), full(wo), full(ln)],
            out_specs=pl.BlockSpec((tm, D_MODEL), lambda i, s: (i, 0))),
        out_shape=jax.ShapeDtypeStruct((t, D_MODEL), _f32),
        compiler_params=_params(("parallel",)),
        name="mem_xattn",
    )(seq_of_tile, x, kv, wq, wo, ln)


def _ffn_tile(x_ref, w1_ref, w2_ref, ln_ref):
    x = x_ref[...]
    xb = x.astype(_bf16)
    hidden = w2_ref.shape[0]
    gate = jnp.dot(xb, w1_ref[:, :hidden], preferred_element_type=_f32)
    up = jnp.dot(xb, w1_ref[:, hidden:], preferred_element_type=_f32)
    y = jnp.dot((_silu(gate) * up).astype(_bf16), w2_ref[...], preferred_element_type=_f32)
    return _residual_ln(x, y, ln_ref[0:1, :], ln_ref[1:2, :])


def _ffn_body(x_ref, w1_ref, w2_ref, ln_ref, o_ref):
    o_ref[...] = _ffn_tile(x_ref, w1_ref, w2_ref, ln_ref)


def _ffn_split_body(x_ref, w1_ref, w2_ref, ln_ref, head_ref, tail_ref, *, n_head):
    i = pl.program_id(0)
    y = _ffn_tile(x_ref, w1_ref, w2_ref, ln_ref)

    @pl.when(i < n_head)
    def _():
        head_ref[...] = y

    @pl.when(i >= n_head)
    def _():
        tail_ref[...] = y


def _ffn(x, w1, w2, ln, tm, split_rows=None):
    t = x.shape[0]
    ins = [pl.BlockSpec((tm, D_MODEL), lambda i: (i, 0)),
           _resident(w1.shape), _resident(w2.shape), _resident(ln.shape)]
    if split_rows is None:
        return pl.pallas_call(
            _ffn_body,
            grid=(t // tm,),
            in_specs=ins,
            out_specs=pl.BlockSpec((tm, D_MODEL), lambda i: (i, 0)),
            out_shape=jax.ShapeDtypeStruct((t, D_MODEL), _f32),
            compiler_params=_params(("parallel",)),
            name="swiglu_ffn",
        )(x, w1, w2, ln)
    n_head = split_rows // tm
    return pl.pallas_call(
        functools.partial(_ffn_split_body, n_head=n_head),
        grid=(t // tm,),
        in_specs=ins,
        out_specs=[pl.BlockSpec((tm, D_MODEL), lambda i: (jnp.minimum(i, n_head - 1), 0)),
                   pl.BlockSpec((tm, D_MODEL), lambda i: (jnp.maximum(i - n_head, 0), 0))],
        out_shape=[jax.ShapeDtypeStruct((split_rows, D_MODEL), _f32),
                   jax.ShapeDtypeStruct((t - split_rows, D_MODEL), _f32)],
        compiler_params=_params(("arbitrary",)),
        name="swiglu_ffn_split",
    )(x, w1, w2, ln)


def _boundary_tables(seq_lens, tile):
    first, last, pos, seq = [], [], [], []
    for sid, length in enumerate(seq_lens):
        n = length // tile
        for b in range(n):
            first.append(int(b == 0))
            last.append(int(b == n - 1))
            pos.append(b)
            seq.append(sid)
    as_i32 = lambda v: jnp.asarray(np.asarray(v, np.int32))
    return as_i32(first), as_i32(last), as_i32(pos), as_i32(seq)


def _split_in_cols(w):
    parts = [w[..., 0:3072], w[..., 4864:5376], w[..., 3072:3200], w[..., 3200:3328], w[..., 5376:5392]]
    pad = jnp.zeros(w.shape[:-1] + (HA_COLS - sum(p.shape[-1] for p in parts),), w.dtype)
    wa = jnp.concatenate(parts + [pad], axis=-1)
    wb = jnp.concatenate([w[..., 5392:8464], w[..., 3328:4864]], axis=-1)
    return wa, wb


def _rows8(*rows):
    n = rows[0].shape[-1]
    out = jnp.zeros((8, n), _f32)
    for r, v in enumerate(rows):
        out = out.at[r].set(v.astype(_f32))
    return out


def _lane_pad(v, offset):
    return jnp.zeros((LANES,), _f32).at[offset:offset + v.shape[0]].set(v.astype(_f32))


def kernel(x_prompt, x_sample, mem_prompt, mem_sample, w_in, hgrn_lb_logits, hgrn_norm_g, attn_sink,
           gdn_conv_w, gdn_a_log, gdn_dt_bias, gdn_norm_g, w_branch_a, w_branch_b, w_branch_c, w_mix_out,
           w_mem_q, w_mem_kv, w_mem_o, w_ffn_in, w_ffn_out, ln_g, ln_b):
    depth = w_in.shape[0]
    d = x_prompt.shape[-1]
    seq_lens = (x_prompt.shape[1],) * x_prompt.shape[0] + (x_sample.shape[1],) * x_sample.shape[0]
    n_prompt = x_prompt.shape[0] * x_prompt.shape[1]
    x = (x_prompt.reshape(-1, d), x_sample.reshape(-1, d))
    mem = jnp.concatenate([mem_prompt, mem_sample], axis=0)
    n_seq, n_mem, _ = mem.shape
    t = _num_rows(x)

    tm = math.gcd(512, *seq_lens)
    tm_proj = tm
    rows_a = math.gcd(HGRN_CHUNKS_PER_STEP * CHUNK, *seq_lens)
    first_a, last_a, _, _ = _boundary_tables(seq_lens, rows_a)
    last_a_rev = last_a[::-1]
    rows_g = math.gcd(GDN_CHUNKS_PER_STEP * CHUNK, *seq_lens)
    first_g, last_g, _, _ = _boundary_tables(seq_lens, rows_g)
    last_g_rev = last_g[::-1]
    nb_swa = math.gcd(SWA_BLOCKS_PER_STEP * B_BLOCK, *seq_lens) // B_BLOCK
    first_w, last_w, pos_w, _ = _boundary_tables(seq_lens, nb_swa * B_BLOCK)
    first_t, last_t, _, seq_t = _boundary_tables(seq_lens, tm)

    s_max = max(seq_lens)
    inv = ROPE_THETA ** (-jnp.arange(0, B_HEAD_DIM, 2, dtype=_f32) / B_HEAD_DIM)
    ang = jnp.arange(s_max, dtype=_f32)[:, None] * inv[None, :]
    cos_t = jnp.tile(jnp.cos(ang), (1, 4))
    sin_t = jnp.tile(jnp.concatenate([-jnp.sin(ang), jnp.sin(ang)], axis=1), (1, 2))

    cum = jnp.cumsum(jax.nn.softmax(hgrn_lb_logits.astype(_f32), axis=1), axis=1)
    lb = cum - cum[:, :1]

    w_in_a, w_in_b = _split_in_cols(w_in.astype(_bf16))
    bf = lambda w: w.astype(_bf16)
    w_a, w_b, w_c, w_mix = bf(w_branch_a), bf(w_branch_b), bf(w_branch_c), bf(w_mix_out)
    w_q, w_kv, w_o = bf(w_mem_q), bf(w_mem_kv), bf(w_mem_o)
    w_f1, w_f2 = bf(w_ffn_in), bf(w_ffn_out)
    mem2 = mem.reshape(n_seq * n_mem, d)

    for l in range(depth):
        lb_rows = []
        for dirn in range(2):
            lbd = lb[dirn, l]
            lb_rows += [jnp.log(lbd), jnp.log1p(-lbd)]
        (ha,) = _in_proj(_in_proj_a_body, "in_proj_a", x, w_in_a[l], [_rows8(*lb_rows)], tm_proj,
                         [(HA_COLS, _f32)])
        gates, qkv_raw = _in_proj(_in_proj_b_body, "in_proj_b", x, w_in_b[l], [], tm_proj,
                                  [(3 * d, GATE_DTYPE), (3 * WIDTH, _f32)])
        oa_f, oa_b = _hgrn(ha, first_a, last_a_rev, rows_a)
        o_b = _swa(ha, cos_t, sin_t, _rows8(_lane_pad(attn_sink[l], 0)), first_w, last_w, pos_w, nb_swa)
        qkv = _gdn_prep(qkv_raw, _rows8(*[gdn_conv_w[l, j] for j in range(CONV_K)]), first_t, last_t, tm)
        gconst = _rows8(_lane_pad(-jnp.exp(gdn_a_log[l].astype(_f32)).reshape(-1), 2 * HEADS),
                        _lane_pad(gdn_dt_bias[l].reshape(-1), 2 * HEADS))
        oc_f, oc_b = _gdn(qkv, ha, gconst, first_g, last_g_rev, rows_g)
        nrm = _rows8(hgrn_norm_g[l], gdn_norm_g[l])
        x = _merge(x, ha, gates, oa_f, oa_b, o_b, oc_f, oc_b, w_a[l], w_b[l], w_c[l], w_mix[l], nrm,
                   _rows8(ln_g[l, 0], ln_b[l, 0]), tm)
        kv = _matmul(mem2, _col_tiles(w_kv[l], 512), n_mem).reshape(n_seq, n_mem, 2 * d)
        x = _xattn(x, kv, w_q[l], w_o[l], _rows8(ln_g[l, 1], ln_b[l, 1]), seq_t, tm)
        x = _ffn(x, w_f1[l], w_f2[l], _rows8(ln_g[l, 2], ln_b[l, 2]), tm,
                 split_rows=n_prompt if l == depth - 1 else None)

    y_prompt, y_sample = x
    return (y_prompt.reshape(x_prompt.shape), y_sample.reshape(x_sample.shape))
```

```python
import functools
import math

import numpy as np
import jax
import jax.numpy as jnp
from jax import lax
from jax.experimental import pallas as pl
from jax.experimental.pallas import tpu as pltpu

D_MODEL = 1024
DEPTH = 4
HEADS = 4
HEAD_DIM = 128
WIDTH = HEADS * HEAD_DIM
B_Q_HEADS = 8
B_KV_HEADS = 2
B_GROUP = B_Q_HEADS // B_KV_HEADS
B_HEAD_DIM = 64
WINDOW = 128
B_BLOCK = 128
ROPE_THETA = 10000.0
CONV_K = 5
MEM_HEADS = 4
MEM_HEAD_DIM = D_MODEL // MEM_HEADS
FFN_HIDDEN = 2816
DN_ALPHA = (2 * DEPTH) ** 0.25

CHUNK = 64
GDN_CHUNKS_PER_STEP = 4
HGRN_CHUNKS_PER_STEP = 8
SWA_BLOCKS_PER_STEP = 8
LANES = 128
LOG2_E = math.log2(math.e)
VMEM_LIMIT = 56 * 1024 * 1024

COL_A_Q, COL_A_FF, COL_A_FB, COL_A_I, COL_A_G = 0, 512, 1024, 1536, 2048
COL_B_Q = 2560
COL_C_GATE = 3072
COL_B_K, COL_B_V = 3584, 3712
COL_GB = 3840
HA_COLS = 4096

_f32 = jnp.float32
_bf16 = jnp.bfloat16
BRANCH_DTYPE = _bf16
GATE_DTYPE = _bf16


def _dot(a, b):
    return jnp.dot(a.astype(_bf16), b.astype(_bf16), preferred_element_type=_f32)


def _dot_nt(a, b):
    return lax.dot_general(a.astype(_bf16), b.astype(_bf16), (((1,), (1,)), ((), ())),
                           preferred_element_type=_f32)


def _dot_tn(a, b):
    return lax.dot_general(a.astype(_bf16), b.astype(_bf16), (((0,), (0,)), ((), ())),
                           preferred_element_type=_f32)


def _split3(x):
    hi = x.astype(_bf16)
    r1 = x - hi.astype(_f32)
    mid = r1.astype(_bf16)
    lo = (r1 - mid.astype(_f32)).astype(_bf16)
    return hi, mid, lo


def _exact_dot(sel, parts):
    dot = lambda p: jnp.dot(sel, p, preferred_element_type=_f32)
    return dot(parts[0]) + dot(parts[1]) + dot(parts[2])


def _silu(x):
    return x * jax.nn.sigmoid(x)


def _iota2(shape, dim):
    return lax.broadcasted_iota(jnp.int32, shape, dim)


def _params(sem):
    return pltpu.CompilerParams(dimension_semantics=sem, vmem_limit_bytes=VMEM_LIMIT)


def _resident(shape):
    return pl.BlockSpec(shape, lambda *_: (0,) * len(shape), pipeline_mode=pl.Buffered(1))


class _TwoPartRows:
    def __init__(self, head_ref, tail_ref, n_head):
        self.head_ref, self.tail_ref, self.n_head = head_ref, tail_ref, n_head
        self.shape, self.dtype = head_ref.shape, head_ref.dtype

    def __getitem__(self, idx):
        return jnp.where(pl.program_id(0) < self.n_head, self.head_ref[idx], self.tail_ref[idx])


def _token_rows(x, tm):
    if not isinstance(x, tuple):
        return [pl.BlockSpec((tm, x.shape[1]), lambda i: (i, 0))], [x], lambda body: body
    head, tail = x
    n_head = head.shape[0] // tm
    specs = [pl.BlockSpec((tm, head.shape[1]), lambda i: (jnp.minimum(i, n_head - 1), 0)),
             pl.BlockSpec((tm, tail.shape[1]), lambda i: (jnp.maximum(i - n_head, 0), 0))]

    def wrap(body):
        return lambda head_ref, tail_ref, *rest: body(_TwoPartRows(head_ref, tail_ref, n_head), *rest)
    return specs, [head, tail], wrap


def _num_rows(x):
    return sum(part.shape[0] for part in x) if isinstance(x, tuple) else x.shape[0]


def _col_tiles(w, tn):
    k, n = w.shape
    return w.reshape(k, n // tn, tn).transpose(1, 0, 2)


def _matmul_body(x_ref, w_ref, o_ref, xb_ref):
    j = pl.program_id(1)

    @pl.when(j == 0)
    def _():
        xb_ref[...] = x_ref[...].astype(_bf16)

    o_ref[...] = jnp.dot(xb_ref[...], w_ref[j], preferred_element_type=_f32)


def _matmul(x, w_tiles, tm):
    t, k = x.shape
    nt, _, tn = w_tiles.shape
    n = nt * tn
    return pl.pallas_call(
        _matmul_body,
        grid=(t // tm, nt),
        in_specs=[pl.BlockSpec((tm, k), lambda i, j: (i, 0)), _resident(w_tiles.shape)],
        out_specs=pl.BlockSpec((tm, tn), lambda i, j: (i, j)),
        out_shape=jax.ShapeDtypeStruct((t, n), _f32),
        scratch_shapes=[pltpu.VMEM((tm, k), _bf16)],
        compiler_params=_params(("parallel", "arbitrary")),
        name="dense_proj",
    )(x, w_tiles)


def _in_proj_a_body(x_ref, w_ref, lb_ref, o_ref):
    xb = x_ref[...].astype(_bf16)
    proj = lambda lo, hi: jnp.dot(xb, w_ref[:, lo:hi], preferred_element_type=_f32)

    def log_f(d):
        def fn(z):
            log_sig = jnp.minimum(z, 0.0) - jnp.log(1.0 + jnp.exp(-jnp.abs(z)))
            log_lb = lb_ref[2 * d:2 * d + 1, :]
            b = lb_ref[2 * d + 1:2 * d + 2, :] + log_sig
            return jnp.maximum(log_lb, b) + jnp.log(1.0 + jnp.exp(-jnp.abs(log_lb - b)))
        return fn

    groups = [(COL_A_Q, COL_A_Q + WIDTH, lambda z: _silu(z) * (HEAD_DIM ** -0.5)),
              (COL_A_FF, COL_A_FF + WIDTH, log_f(0)),
              (COL_A_FB, COL_A_FB + WIDTH, log_f(1)),
              (COL_A_I, o_ref.shape[1], lambda z: z)]
    z_next = proj(groups[0][0], groups[0][1])
    for g, (lo, hi, fn) in enumerate(groups):
        z = z_next
        if g + 1 < len(groups):
            z_next = proj(groups[g + 1][0], groups[g + 1][1])
        o_ref[:, lo:hi] = fn(z)


def _in_proj_b_body(x_ref, w_ref, gates_ref, qkv_ref):
    xb = x_ref[...].astype(_bf16)
    split = gates_ref.shape[1]
    gates_ref[...] = jnp.dot(xb, w_ref[:, :split], preferred_element_type=_f32).astype(gates_ref.dtype)
    qkv_ref[...] = jnp.dot(xb, w_ref[:, split:], preferred_element_type=_f32)


def _in_proj(body, name, x, w, consts, tm, outs):
    t = _num_rows(x)
    x_specs, x_ops, wrap = _token_rows(x, tm)
    return pl.pallas_call(
        wrap(body),
        grid=(t // tm,),
        in_specs=x_specs + [_resident(w.shape)] + [_resident(c.shape) for c in consts],
        out_specs=[pl.BlockSpec((tm, width), lambda i: (i, 0)) for width, _ in outs],
        out_shape=[jax.ShapeDtypeStruct((t, width), dtype) for width, dtype in outs],
        compiler_params=_params(("parallel",)),
        name=name,
    )(*x_ops, w, *consts)


def _chunk_masks(c, reverse):
    row = _iota2((c, c), 0)
    col = _iota2((c, c), 1)
    if reverse:
        return col >= row, col > row
    return col <= row, col < row


def _level_mask(c, blk, reverse, reps=1):
    row = _iota2((c, reps * c), 0)
    col = _iota2((c, reps * c), 1) % c
    half = blk // 2
    same = (row // blk) == (col // blk)
    r_hi = (row % blk) >= half
    c_hi = (col % blk) >= half
    if reverse:
        return same & jnp.logical_not(r_hi) & c_hi
    return same & r_hi & jnp.logical_not(c_hi)


def _block_ref_rows(g, blk, reverse):
    c, n = g.shape
    idx = blk // 2 if reverse else blk // 2 - 1
    rows = max(blk, 8)
    g3 = g.reshape(c // rows, rows, n)
    pick = lambda r: jnp.broadcast_to(g3[:, r:r + 1, :], g3.shape)
    out = pick(idx)
    if blk < rows:
        sub = lax.broadcasted_iota(jnp.int32, g3.shape, 1)
        for b in range(1, rows // blk):
            out = jnp.where(sub >= b * blk, pick(b * blk + idx), out)
    return out.reshape(c, n)


def _levels(c):
    out, blk = [], c
    while blk >= 2:
        out.append(blk)
        blk //= 2
    return out


def _hgrn_body(first_f_ref, first_b_ref, q_f, lf_f, v_f, q_b, lf_b, v_b, o_f, o_b, st_ref):
    c = CHUNK
    step = pl.program_id(0)

    @pl.when(first_f_ref[step] == 1)
    def _():
        st_ref[0:HEADS] = jnp.zeros((HEADS, HEAD_DIM, HEAD_DIM), _f32)

    @pl.when(first_b_ref[step] == 1)
    def _():
        st_ref[HEADS:2 * HEADS] = jnp.zeros((HEADS, HEAD_DIM, HEAD_DIM), _f32)

    nsub = q_f.shape[0] // c
    row = _iota2((c, c), 0)
    col = _iota2((c, c), 1)
    eye = row == col
    blks = _levels(c)
    units = []
    for d, (q_ref, lf_ref, v_ref) in enumerate(((q_f, lf_f, v_f), (q_b, lf_b, v_b))):
        reverse = d == 1
        incl, _ = _chunk_masks(c, reverse)
        tri = incl.astype(_bf16)
        masks = [_level_mask(c, blk, reverse) for blk in blks]
        log_f = lf_ref[...]
        key_all = 1.0 - jnp.exp(log_f)
        rpos = _iota2((c, HEAD_DIM), 0)
        later = [((rpos % blk) < blk // 2) if reverse else ((rpos % blk) >= blk // 2) for blk in blks]
        sgn = [jnp.where(m, LOG2_E, -LOG2_E) for m in later]
        q_all = q_ref[...]
        v_all = v_ref[...]
        tot_row = 0 if reverse else c - 1
        for s in (range(nsub - 1, -1, -1) if reverse else range(nsub)):
            rs = slice(s * c, (s + 1) * c)
            gc_all = _exact_dot(tri, _split3(log_f[rs]))
            ref_all = [_block_ref_rows(gc_all, blk, reverse) for blk in blks]
            for h in range(HEADS):
                sl = slice(h * HEAD_DIM, (h + 1) * HEAD_DIM)
                units.append(dict(q=q_all[rs, sl], k=key_all[rs, sl], v=v_all[rs, sl], gc=gc_all[:, sl],
                                  refs=[r[:, sl] for r in ref_all], masks=masks, sgn=sgn, rows=rs,
                                  g_tot=gc_all[tot_row:tot_row + 1, sl]))

    scores = [jnp.where(eye, _dot_nt(u["q"], u["k"]), 0.0) for u in units]
    for l in range(len(blks)):
        e = [jnp.exp2((u["gc"] - u["refs"][l]) * u["sgn"][l]) for u in units]
        scores = [jnp.where(u["masks"][l], _dot_nt(u["q"] * e[i], u["k"] * e[i]), scores[i])
                  for i, u in enumerate(units)]
    intra = [_dot(scores[i], u["v"]) for i, u in enumerate(units)]
    qd = [u["q"] * jnp.exp(u["gc"]) for u in units]
    kd = [u["k"] * jnp.exp(u["g_tot"] - u["gc"]) for u in units]

    st = [st_ref[i] for i in range(2 * HEADS)]
    for slot in range(nsub):
        for d in range(2):
            for h in range(HEADS):
                i, j = (d * nsub + slot) * HEADS + h, d * HEADS + h
                out = intra[i] + _dot_nt(qd[i], st[j])
                st[j] = st[j] * jnp.exp(units[i]["g_tot"]) + _dot_tn(units[i]["v"], kd[i])
                (o_f, o_b)[d][units[i]["rows"], h * HEAD_DIM:(h + 1) * HEAD_DIM] = out.astype(o_f.dtype)
    for i in range(2 * HEADS):
        st_ref[i] = st[i]


def _hgrn(h, first_f, first_b, rows):
    t = h.shape[0]
    n = t // rows
    fwd = lambda col: pl.BlockSpec((rows, WIDTH), lambda i, a, b: (i, col // WIDTH))
    bwd = lambda col: pl.BlockSpec((rows, WIDTH), lambda i, a, b: (n - 1 - i, col // WIDTH))
    out = jax.ShapeDtypeStruct((t, WIDTH), BRANCH_DTYPE)
    return pl.pallas_call(
        _hgrn_body,
        grid_spec=pltpu.PrefetchScalarGridSpec(
            num_scalar_prefetch=2,
            grid=(n,),
            in_specs=[fwd(COL_A_Q), fwd(COL_A_FF), fwd(COL_A_I), bwd(COL_A_Q), bwd(COL_A_FB), bwd(COL_A_I)],
            out_specs=[fwd(0), bwd(0)],
            scratch_shapes=[pltpu.VMEM((2 * HEADS, HEAD_DIM, HEAD_DIM), _f32)]),
        out_shape=[out, out],
        compiler_params=_params(("arbitrary",)),
        name="hgrn",
    )(first_f, first_b, h, h, h, h, h, h)


def _gdn_prep_body(first_ref, last_ref, prev_ref, cur_ref, next_ref, w_ref, o_ref, ext_ref):
    i = pl.program_id(0)
    tm = cur_ref.shape[0]
    halo = CONV_K // 2
    ext_ref[0:8, :] = jnp.where(first_ref[i] == 1, 0.0, prev_ref[...])
    ext_ref[8:8 + tm, :] = cur_ref[...]
    ext_ref[8 + tm:16 + tm, :] = jnp.where(last_ref[i] == 1, 0.0, next_ref[...])
    acc = cur_ref[...] * w_ref[halo:halo + 1, :]
    for j in range(CONV_K):
        if j != halo:
            off = 8 - halo + j
            acc = acc + ext_ref[off:off + tm, :] * w_ref[j:j + 1, :]
    y = _silu(acc)
    for part in range(3):
        for h in range(HEADS):
            lo = part * WIDTH + h * HEAD_DIM
            x = y[:, lo:lo + HEAD_DIM]
            if part < 2:
                x = x * lax.rsqrt(jnp.sum(x * x, axis=1, keepdims=True) + 1e-6)
            if part == 0:
                x = x * (HEAD_DIM ** -0.5)
            o_ref[:, lo:lo + HEAD_DIM] = x


def _gdn_prep(h, conv_w, first, last, tm):
    t = h.shape[0]
    n = t // tm
    r8 = tm // 8
    cb = 0
    return pl.pallas_call(
        _gdn_prep_body,
        grid_spec=pltpu.PrefetchScalarGridSpec(
            num_scalar_prefetch=2,
            grid=(n,),
            in_specs=[pl.BlockSpec((8, 3 * WIDTH), lambda i, f, l: (jnp.maximum(i * r8 - 1, 0), cb)),
                      pl.BlockSpec((tm, 3 * WIDTH), lambda i, f, l: (i, cb)),
                      pl.BlockSpec((8, 3 * WIDTH), lambda i, f, l: (jnp.minimum((i + 1) * r8, n * r8 - 1), cb)),
                      pl.BlockSpec((8, 3 * WIDTH), lambda i, f, l: (0, 0))],
            out_specs=pl.BlockSpec((tm, 3 * WIDTH), lambda i, f, l: (i, 0)),
            scratch_shapes=[pltpu.VMEM((tm + 16, 3 * WIDTH), _f32)]),
        out_shape=jax.ShapeDtypeStruct((t, 3 * WIDTH), _f32),
        compiler_params=_params(("parallel",)),
        name="gdn_prep",
    )(first, last, h, h, h, conv_w)


def _gdn_body(first_f_ref, first_b_ref, q_f, k_f, v_f, gb_f, q_b, k_b, v_b, gb_b, cst_ref, o_f, o_b, st_ref):
    c = CHUNK
    step = pl.program_id(0)

    @pl.when(first_f_ref[step] == 1)
    def _():
        st_ref[0:HEADS] = jnp.zeros((HEADS, HEAD_DIM, HEAD_DIM), _f32)

    @pl.when(first_b_ref[step] == 1)
    def _():
        st_ref[HEADS:2 * HEADS] = jnp.zeros((HEADS, HEAD_DIM, HEAD_DIM), _f32)

    nsub = q_f.shape[0] // c
    row = _iota2((c, c), 0)
    col = _iota2((c, c), 1)
    eye = (row == col).astype(_f32)
    neg_a = cst_ref[0:1, :]
    dt_bias = cst_ref[1:2, :]
    sizes = _levels(c)[::-1]
    units = []
    for d, (q_ref, k_ref, v_ref, gb_ref) in enumerate(((q_f, k_f, v_f, gb_f), (q_b, k_b, v_b, gb_b))):
        reverse = d == 1
        incl, strict = _chunk_masks(c, reverse)
        tri = incl.astype(_bf16)
        masks = [_level_mask(c, blk, reverse) for blk in sizes]
        gb = gb_ref[...]
        beta_t = jax.nn.sigmoid(gb)
        zz = gb + dt_bias
        g_in = neg_a * (jnp.maximum(zz, 0.0) + jnp.log1p(jnp.exp(-jnp.abs(zz))))
        tot_row = 0 if reverse else c - 1
        for s in (range(nsub - 1, -1, -1) if reverse else range(nsub)):
            rs = slice(s * c, (s + 1) * c)
            gc_t = _exact_dot(tri, _split3(g_in[rs]))
            gc_tt = jnp.transpose(gc_t)
            for h in range(HEADS):
                sl = slice(h * HEAD_DIM, (h + 1) * HEAD_DIM)
                j = 2 * HEADS + d * HEADS + h
                g_col = gc_t[:, j:j + 1]
                decay = jnp.where(incl, jnp.exp(jnp.minimum(g_col - gc_tt[j:j + 1, :], 0.0)), 0.0)
                units.append(dict(q=q_ref[rs, sl], k=k_ref[rs, sl], v=v_ref[rs, sl], rows=rs,
                                  g_col=g_col, decay=decay, strict=strict, masks=masks,
                                  beta=beta_t[rs, d * HEADS + h:d * HEADS + h + 1],
                                  g_tot=gc_t[tot_row:tot_row + 1, j:j + 1]))

    k_b = [u["k"].astype(_bf16) for u in units]
    a = [jnp.where(u["strict"], _dot_nt(k_b[i], k_b[i]) * u["decay"] * u["beta"], 0.0) for i, u in enumerate(units)]
    qk = [_dot_nt(u["q"], k_b[i]) * u["decay"] for i, u in enumerate(units)]
    inv = [eye - jnp.where(u["masks"][0], a[i], 0.0) for i, u in enumerate(units)]
    for l in range(1, len(sizes)):
        inv_b = [m.astype(_bf16) for m in inv]
        t1 = [_dot(jnp.where(u["masks"][l], a[i], 0.0), inv_b[i]) for i, u in enumerate(units)]
        inv = [inv[i] - _dot(inv_b[i], t1[i]) for i in range(len(units))]
    inv_b = [m.astype(_bf16) for m in inv]
    e_g = [jnp.exp(u["g_col"]) for u in units]
    sol = [_dot(inv_b[i], jnp.concatenate([u["v"] * u["beta"], u["k"] * (u["beta"] * e_g[i])], axis=1))
           for i, u in enumerate(units)]
    uu = [m[:, :HEAD_DIM] for m in sol]
    ww = [m[:, HEAD_DIM:] for m in sol]
    qd = [u["q"] * e_g[i] for i, u in enumerate(units)]
    kd = [u["k"] * jnp.exp(u["g_tot"] - u["g_col"]) for u in units]
    qk_sol = [_dot(qk[i], sol[i]) for i in range(len(units))]
    out0 = [m[:, :HEAD_DIM] for m in qk_sol]
    q_eff = [(qd[i] - qk_sol[i][:, HEAD_DIM:]).astype(_bf16) for i in range(len(units))]
    kd_sol = [_dot_tn(kd[i], sol[i]) for i in range(len(units))]
    gain = [m[:, :HEAD_DIM] for m in kd_sol]
    trans = [m[:, HEAD_DIM:].astype(_bf16) for m in kd_sol]

    st = [st_ref[i] for i in range(2 * HEADS)]
    for slot in range(nsub):
        for d in range(2):
            for h in range(HEADS):
                i, j = (d * nsub + slot) * HEADS + h, d * HEADS + h
                st_b = st[j].astype(_bf16)
                out = out0[i] + _dot(q_eff[i], st_b)
                st[j] = st[j] * jnp.exp(units[i]["g_tot"]) - _dot(trans[i], st_b) + gain[i]
                (o_f, o_b)[d][units[i]["rows"], h * HEAD_DIM:(h + 1) * HEAD_DIM] = out.astype(o_f.dtype)
    for i in range(2 * HEADS):
        st_ref[i] = st[i]


def _gdn(qkv, h, consts, first_f, first_b, rows):
    t = qkv.shape[0]
    n = t // rows
    fwd = lambda w, colblk: pl.BlockSpec((rows, w), lambda i, a, b: (i, colblk))
    bwd = lambda w, colblk: pl.BlockSpec((rows, w), lambda i, a, b: (n - 1 - i, colblk))
    gcol = COL_GB // LANES
    out = jax.ShapeDtypeStruct((t, WIDTH), BRANCH_DTYPE)
    return pl.pallas_call(
        _gdn_body,
        grid_spec=pltpu.PrefetchScalarGridSpec(
            num_scalar_prefetch=2,
            grid=(n,),
            in_specs=[fwd(WIDTH, 0), fwd(WIDTH, 1), fwd(WIDTH, 2), fwd(LANES, gcol),
                      bwd(WIDTH, 0), bwd(WIDTH, 1), bwd(WIDTH, 2), bwd(LANES, gcol),
                      pl.BlockSpec((8, LANES), lambda i, a, b: (0, 0))],
            out_specs=[fwd(WIDTH, 0), bwd(WIDTH, 0)],
            scratch_shapes=[pltpu.VMEM((2 * HEADS, HEAD_DIM, HEAD_DIM), _f32)]),
        out_shape=[out, out],
        compiler_params=_params(("arbitrary",)),
        name="gdn",
    )(first_f, first_b, qkv, qkv, qkv, h, qkv, qkv, qkv, h, consts)


def _rope(x, cos, sin_signed):
    src = _iota2((LANES, LANES), 0)
    dst = _iota2((LANES, LANES), 1)
    half = B_HEAD_DIM // 2
    partner = jnp.where(dst % B_HEAD_DIM < half, dst + half, dst - half)
    rot = jnp.dot(x.astype(_bf16), (src == partner).astype(_bf16), preferred_element_type=_f32)
    return x * cos + rot * sin_signed


def _swa_body(first_ref, last_ref, pos_ref, q_ref, kp_ref, kc_ref, kn_ref, vp_ref, vc_ref, vn_ref,
              cp_ref, sp_ref, cc_ref, sc_ref, cn_ref, sn_ref, sink_ref, o_ref):
    i = pl.program_id(0)
    blk = B_BLOCK
    nb = q_ref.shape[0] // blk
    has_prev = first_ref[i] == 0
    has_next = last_ref[i] == 0
    k_all = jnp.concatenate([_rope(kp_ref[...], cp_ref[...], sp_ref[...]),
                             _rope(kc_ref[...], cc_ref[...], sc_ref[...]),
                             _rope(kn_ref[...], cn_ref[...], sn_ref[...])], axis=0)
    v_all = jnp.concatenate([vp_ref[...], vc_ref[...], vn_ref[...]], axis=0)
    nq = B_GROUP * blk
    kpos = _iota2((3 * blk, nq), 0)
    qpos = _iota2((3 * blk, nq), 1) % blk
    rel = kpos - qpos
    in_window = (rel >= 0) & (rel <= 2 * WINDOW)
    cos, sin_signed = cc_ref[...], sc_ref[...]
    scale = B_HEAD_DIM ** -0.5
    qr = [_rope(q_ref[:, p * LANES:(p + 1) * LANES], cos, sin_signed) * scale for p in range(B_Q_HEADS // 2)]
    sink_row = sink_ref[0:1, :]
    chains = [(b, kv) for b in range(nb) for kv in range(B_KV_HEADS)]
    q4, sink, kh, vh, ok = [], [], [], [], []
    for b, kv in chains:
        rows = slice(b * blk, (b + 1) * blk)
        heads = [qr[hq // 2][rows, (hq % 2) * B_HEAD_DIM:(hq % 2 + 1) * B_HEAD_DIM]
                 for hq in range(kv * B_GROUP, (kv + 1) * B_GROUP)]
        q4.append(jnp.concatenate(heads, axis=0))
        sink.append(jnp.concatenate([jnp.broadcast_to(sink_row[:, kv * B_GROUP + g:kv * B_GROUP + g + 1], (1, blk))
                                     for g in range(B_GROUP)], axis=1))
        kh.append(k_all[b * blk:(b + 3) * blk, kv * B_HEAD_DIM:(kv + 1) * B_HEAD_DIM])
        vh.append(v_all[b * blk:(b + 3) * blk, kv * B_HEAD_DIM:(kv + 1) * B_HEAD_DIM])
        m_ok = in_window
        if b == 0:
            m_ok = m_ok & (has_prev | (kpos >= blk))
        if b == nb - 1:
            m_ok = m_ok & (has_next | (kpos < 2 * blk))
        ok.append(m_ok)
    cs = range(len(chains))
    s = [jnp.where(ok[c], _dot_nt(kh[c], q4[c]), -jnp.inf) for c in cs]
    m = [jnp.maximum(jnp.max(s[c], axis=0, keepdims=True), sink[c]) for c in cs]
    p = [jnp.exp(s[c] - m[c]) for c in cs]
    denom = [jnp.sum(p[c], axis=0, keepdims=True) + jnp.exp(sink[c] - m[c]) for c in cs]
    o_t = [_dot_tn(vh[c], p[c]) / denom[c] for c in cs]
    for c, (b, kv) in enumerate(chains):
        for g in range(0, B_GROUP, 2):
            hq = kv * B_GROUP + g
            pair = jnp.concatenate([o_t[c][:, g * blk:(g + 1) * blk],
                                    o_t[c][:, (g + 1) * blk:(g + 2) * blk]], axis=0)
            o_ref[b * blk:(b + 1) * blk, hq * B_HEAD_DIM:(hq + 2) * B_HEAD_DIM] = (
                jnp.transpose(pair).astype(o_ref.dtype))


def _swa(h, cos_t, sin_t, sink, first, last, pos, nb):
    t = h.shape[0]
    rows = nb * B_BLOCK
    n = t // rows
    n128 = t // B_BLOCK
    npos = cos_t.shape[0] // B_BLOCK
    kcol, vcol = COL_B_K // LANES, COL_B_V // LANES
    before = lambda colblk: pl.BlockSpec((B_BLOCK, LANES), lambda i, a, b, p: (jnp.maximum(i * nb - 1, 0), colblk))
    own = lambda colblk: pl.BlockSpec((rows, LANES), lambda i, a, b, p: (i, colblk))
    after = lambda colblk: pl.BlockSpec((B_BLOCK, LANES),
                                        lambda i, a, b, p: (jnp.minimum((i + 1) * nb, n128 - 1), colblk))
    t_before = pl.BlockSpec((B_BLOCK, LANES), lambda i, a, b, p: (jnp.maximum(p[i] * nb - 1, 0), 0))
    t_own = pl.BlockSpec((rows, LANES), lambda i, a, b, p: (p[i], 0))
    t_after = pl.BlockSpec((B_BLOCK, LANES), lambda i, a, b, p: (jnp.minimum((p[i] + 1) * nb, npos - 1), 0))
    qw = B_Q_HEADS * B_HEAD_DIM
    return pl.pallas_call(
        _swa_body,
        grid_spec=pltpu.PrefetchScalarGridSpec(
            num_scalar_prefetch=3,
            grid=(n,),
            in_specs=[pl.BlockSpec((rows, qw), lambda i, a, b, p: (i, COL_B_Q // qw)),
                      before(kcol), own(kcol), after(kcol), before(vcol), own(vcol), after(vcol),
                      t_before, t_before, t_own, t_own, t_after, t_after,
                      pl.BlockSpec((8, LANES), lambda i, a, b, p: (0, 0))],
            out_specs=pl.BlockSpec((rows, qw), lambda i, a, b, p: (i, 0))),
        out_shape=jax.ShapeDtypeStruct((t, qw), BRANCH_DTYPE),
        compiler_params=_params(("parallel",)),
        name="window_attn",
    )(first, last, pos, h, h, h, h, h, h, h, cos_t, sin_t, cos_t, sin_t, cos_t, sin_t, sink)


def _residual_ln(x, y, g, b):
    z = DN_ALPHA * x + y
    mu = jnp.mean(z, axis=1, keepdims=True)
    zc = z - mu
    var = jnp.mean(zc * zc, axis=1, keepdims=True)
    return zc * lax.rsqrt(var + 1e-5) * g + b


def _gated_rms(o, gate, g):
    outs = []
    for h in range(HEADS):
        sl = slice(h * HEAD_DIM, (h + 1) * HEAD_DIM)
        x = o[:, sl]
        x = x * lax.rsqrt(jnp.mean(x * x, axis=1, keepdims=True) + 1e-6) * g
        outs.append(x * _silu(gate[:, sl]))
    return jnp.concatenate(outs, axis=1)


def _merge_body(x_ref, af_ref, ab_ref, ag_ref, ob_ref, cf_ref, cb_ref, cg_ref, ma_ref, mb_ref, mc_ref,
                wa_ref, wb_ref, wc_ref, wo_ref, nrm_ref, ln_ref, o_ref):
    up = lambda ref: ref[...].astype(_f32)
    oa = _gated_rms(up(af_ref) + up(ab_ref), ag_ref[...], nrm_ref[0:1, :])
    oc = _gated_rms(up(cf_ref) + up(cb_ref), cg_ref[...], nrm_ref[1:2, :])
    pa = jnp.dot(oa.astype(_bf16), wa_ref[...], preferred_element_type=_f32)
    pb = jnp.dot(ob_ref[...].astype(_bf16), wb_ref[...], preferred_element_type=_f32)
    pc = jnp.dot(oc.astype(_bf16), wc_ref[...], preferred_element_type=_f32)
    mix = jax.nn.sigmoid(up(ma_ref)) * pa + jax.nn.sigmoid(up(mb_ref)) * pb + jax.nn.sigmoid(up(mc_ref)) * pc
    y = jnp.dot(mix.astype(_bf16), wo_ref[...], preferred_element_type=_f32)
    o_ref[...] = _residual_ln(x_ref[...], y, ln_ref[0:1, :], ln_ref[1:2, :])


def _merge(x, ha, gates, oa_f, oa_b, ob, oc_f, oc_b, wa, wb, wc, wo, nrm, ln, tm):
    t = _num_rows(x)
    x_specs, x_ops, wrap = _token_rows(x, tm)
    row = lambda w, colblk=0: pl.BlockSpec((tm, w), lambda i: (i, colblk))
    full = lambda a: pl.BlockSpec(a.shape, lambda i: (0, 0))
    return pl.pallas_call(
        wrap(_merge_body),
        grid=(t // tm,),
        in_specs=x_specs + [row(WIDTH), row(WIDTH), row(WIDTH, COL_A_G // WIDTH), row(WIDTH),
                            row(WIDTH), row(WIDTH), row(WIDTH, COL_C_GATE // WIDTH),
                            row(D_MODEL, 0), row(D_MODEL, 1), row(D_MODEL, 2),
                            full(wa), full(wb), full(wc), full(wo), full(nrm), full(ln)],
        out_specs=row(D_MODEL),
        out_shape=jax.ShapeDtypeStruct((t, D_MODEL), _f32),
        compiler_params=_params(("parallel",)),
        name="merge_mix",
    )(*x_ops, oa_f, oa_b, ha, ob, oc_f, oc_b, ha, gates, gates, gates, wa, wb, wc, wo, nrm, ln)


def _xattn_body(seq_ref, x_ref, kv_ref, wq_ref, wo_ref, ln_ref, o_ref):
    rows = x_ref.shape[0] // 2
    parts = range(2)
    x = [x_ref[i * rows:(i + 1) * rows, :] for i in parts]
    q = [jnp.dot(x[i].astype(_bf16), wq_ref[...], preferred_element_type=_f32) for i in parts]
    hs = range(MEM_HEADS)
    cols = [slice(h * MEM_HEAD_DIM, (h + 1) * MEM_HEAD_DIM) for h in hs]
    s = [[_dot_nt(q[i][:, cols[h]], kv_ref[0, :, cols[h]]) * (MEM_HEAD_DIM ** -0.5) for h in hs] for i in parts]
    p = [[jnp.exp(s[i][h] - jnp.max(s[i][h], axis=1, keepdims=True)) for h in hs] for i in parts]
    p = [[p[i][h] / jnp.sum(p[i][h], axis=1, keepdims=True) for h in hs] for i in parts]
    o = [jnp.concatenate([_dot(p[i][h], kv_ref[0, :, D_MODEL + h * MEM_HEAD_DIM:D_MODEL + (h + 1) * MEM_HEAD_DIM])
                          for h in hs], axis=1) for i in parts]
    y = [jnp.dot(o[i].astype(_bf16), wo_ref[...], preferred_element_type=_f32) for i in parts]
    for i in parts:
        o_ref[i * rows:(i + 1) * rows, :] = _residual_ln(x[i], y[i], ln_ref[0:1, :], ln_ref[1:2, :])


def _xattn(x, kv, wq, wo, ln, seq_of_tile, tm):
    t = x.shape[0]
    full = lambda a: pl.BlockSpec(a.shape, lambda i, s: (0, 0))
    return pl.pallas_call(
        _xattn_body,
        grid_spec=pltpu.PrefetchScalarGridSpec(
            num_scalar_prefetch=1,
            grid=(t // tm,),
            in_specs=[pl.BlockSpec((tm, D_MODEL), lambda i, s: (i, 0)),
                      pl.BlockSpec((1,) + kv.shape[1:], lambda i, s: (s[i], 0, 0)),
                      full(wq), full(wo), full(ln)],
            out_specs=pl.BlockSpec((tm, D_MODEL), lambda i, s: (i, 0))),
        out_shape=jax.ShapeDtypeStruct((t, D_MODEL), _f32),
        compiler_params=_params(("parallel",)),
        name="mem_xattn",
    )(seq_of_tile, x, kv, wq, wo, ln)


def _ffn_tile(x_ref, w1_ref, w2_ref, ln_ref):
    x = x_ref[...]
    xb = x.astype(_bf16)
    hidden = w2_ref.shape[0]
    gate = jnp.dot(xb, w1_ref[:, :hidden], preferred_element_type=_f32)
    up = jnp.dot(xb, w1_ref[:, hidden:], preferred_element_type=_f32)
    y = jnp.dot((_silu(gate) * up).astype(_bf16), w2_ref[...], preferred_element_type=_f32)
    return _residual_ln(x, y, ln_ref[0:1, :], ln_ref[1:2, :])


def _ffn_body(x_ref, w1_ref, w2_ref, ln_ref, o_ref):
    o_ref[...] = _ffn_tile(x_ref, w1_ref, w2_ref, ln_ref)


def _ffn_split_body(x_ref, w1_ref, w2_ref, ln_ref, head_ref, tail_ref, *, n_head):
    i = pl.program_id(0)
    y = _ffn_tile(x_ref, w1_ref, w2_ref, ln_ref)

    @pl.when(i < n_head)
    def _():
        head_ref[...] = y

    @pl.when(i >= n_head)
    def _():
        tail_ref[...] = y


def _ffn(x, w1, w2, ln, tm, split_rows=None):
    t = x.shape[0]
    ins = [pl.BlockSpec((tm, D_MODEL), lambda i: (i, 0)),
           _resident(w1.shape), _resident(w2.shape), _resident(ln.shape)]
    if split_rows is None:
        return pl.pallas_call(
            _ffn_body,
            grid=(t // tm,),
            in_specs=ins,
            out_specs=pl.BlockSpec((tm, D_MODEL), lambda i: (i, 0)),
            out_shape=jax.ShapeDtypeStruct((t, D_MODEL), _f32),
            compiler_params=_params(("parallel",)),
            name="swiglu_ffn",
        )(x, w1, w2, ln)
    n_head = split_rows // tm
    return pl.pallas_call(
        functools.partial(_ffn_split_body, n_head=n_head),
        grid=(t // tm,),
        in_specs=ins,
        out_specs=[pl.BlockSpec((tm, D_MODEL), lambda i: (jnp.minimum(i, n_head - 1), 0)),
                   pl.BlockSpec((tm, D_MODEL), lambda i: (jnp.maximum(i - n_head, 0), 0))],
        out_shape=[jax.ShapeDtypeStruct((split_rows, D_MODEL), _f32),
                   jax.ShapeDtypeStruct((t - split_rows, D_MODEL), _f32)],
        compiler_params=_params(("arbitrary",)),
        name="swiglu_ffn_split",
    )(x, w1, w2, ln)


def _boundary_tables(seq_lens, tile):
    first, last, pos, seq = [], [], [], []
    for sid, length in enumerate(seq_lens):
        n = length // tile
        for b in range(n):
            first.append(int(b == 0))
            last.append(int(b == n - 1))
            pos.append(b)
            seq.append(sid)
    as_i32 = lambda v: jnp.asarray(np.asarray(v, np.int32))
    return as_i32(first), as_i32(last), as_i32(pos), as_i32(seq)


def _split_in_cols(w):
    parts = [w[..., 0:3072], w[..., 4864:5376], w[..., 3072:3200], w[..., 3200:3328], w[..., 5376:5392]]
    pad = jnp.zeros(w.shape[:-1] + (HA_COLS - sum(p.shape[-1] for p in parts),), w.dtype)
    wa = jnp.concatenate(parts + [pad], axis=-1)
    wb = jnp.concatenate([w[..., 5392:8464], w[..., 3328:4864]], axis=-1)
    return wa, wb


def _rows8(*rows):
    n = rows[0].shape[-1]
    out = jnp.zeros((8, n), _f32)
    for r, v in enumerate(rows):
        out = out.at[r].set(v.astype(_f32))
    return out


def _lane_pad(v, offset):
    return jnp.zeros((LANES,), _f32).at[offset:offset + v.shape[0]].set(v.astype(_f32))


def kernel(x_prompt, x_sample, mem_prompt, mem_sample, w_in, hgrn_lb_logits, hgrn_norm_g, attn_sink,
           gdn_conv_w, gdn_a_log, gdn_dt_bias, gdn_norm_g, w_branch_a, w_branch_b, w_branch_c, w_mix_out,
           w_mem_q, w_mem_kv, w_mem_o, w_ffn_in, w_ffn_out, ln_g, ln_b):
    depth = w_in.shape[0]
    d = x_prompt.shape[-1]
    seq_lens = (x_prompt.shape[1],) * x_prompt.shape[0] + (x_sample.shape[1],) * x_sample.shape[0]
    n_prompt = x_prompt.shape[0] * x_prompt.shape[1]
    x = (x_prompt.reshape(-1, d), x_sample.reshape(-1, d))
    mem = jnp.concatenate([mem_prompt, mem_sample], axis=0)
    n_seq, n_mem, _ = mem.shape
    t = _num_rows(x)

    tm = math.gcd(512, *seq_lens)
    tm_proj = tm
    rows_a = math.gcd(HGRN_CHUNKS_PER_STEP * CHUNK, *seq_lens)
    first_a, last_a, _, _ = _boundary_tables(seq_lens, rows_a)
    last_a_rev = last_a[::-1]
    rows_g = math.gcd(GDN_CHUNKS_PER_STEP * CHUNK, *seq_lens)
    first_g, last_g, _, _ = _boundary_tables(seq_lens, rows_g)
    last_g_rev = last_g[::-1]
    nb_swa = math.gcd(SWA_BLOCKS_PER_STEP * B_BLOCK, *seq_lens) // B_BLOCK
    first_w, last_w, pos_w, _ = _boundary_tables(seq_lens, nb_swa * B_BLOCK)
    first_t, last_t, _, _ = _boundary_tables(seq_lens, tm)
    tm_x = math.gcd(2 * tm, *seq_lens)
    _, _, _, seq_x = _boundary_tables(seq_lens, tm_x)

    s_max = max(seq_lens)
    inv = ROPE_THETA ** (-jnp.arange(0, B_HEAD_DIM, 2, dtype=_f32) / B_HEAD_DIM)
    ang = jnp.arange(s_max, dtype=_f32)[:, None] * inv[None, :]
    cos_t = jnp.tile(jnp.cos(ang), (1, 4))
    sin_t = jnp.tile(jnp.concatenate([-jnp.sin(ang), jnp.sin(ang)], axis=1), (1, 2))

    cum = jnp.cumsum(jax.nn.softmax(hgrn_lb_logits.astype(_f32), axis=1), axis=1)
    lb = cum - cum[:, :1]

    w_in_a, w_in_b = _split_in_cols(w_in.astype(_bf16))
    bf = lambda w: w.astype(_bf16)
    w_a, w_b, w_c, w_mix = bf(w_branch_a), bf(w_branch_b), bf(w_branch_c), bf(w_mix_out)
    w_q, w_kv, w_o = bf(w_mem_q), bf(w_mem_kv), bf(w_mem_o)
    w_f1, w_f2 = bf(w_ffn_in), bf(w_ffn_out)
    mem2 = mem.reshape(n_seq * n_mem, d)

    for l in range(depth):
        lb_rows = []
        for dirn in range(2):
            lbd = lb[dirn, l]
            lb_rows += [jnp.log(lbd), jnp.log1p(-lbd)]
        (ha,) = _in_proj(_in_proj_a_body, "in_proj_a", x, w_in_a[l], [_rows8(*lb_rows)], tm_proj,
                         [(HA_COLS, _f32)])
        gates, qkv_raw = _in_proj(_in_proj_b_body, "in_proj_b", x, w_in_b[l], [], tm_proj,
                                  [(3 * d, GATE_DTYPE), (3 * WIDTH, _f32)])
        oa_f, oa_b = _hgrn(ha, first_a, last_a_rev, rows_a)
        o_b = _swa(ha, cos_t, sin_t, _rows8(_lane_pad(attn_sink[l], 0)), first_w, last_w, pos_w, nb_swa)
        qkv = _gdn_prep(qkv_raw, _rows8(*[gdn_conv_w[l, j] for j in range(CONV_K)]), first_t, last_t, tm)
        gconst = _rows8(_lane_pad(-jnp.exp(gdn_a_log[l].astype(_f32)).reshape(-1), 2 * HEADS),
                        _lane_pad(gdn_dt_bias[l].reshape(-1), 2 * HEADS))
        oc_f, oc_b = _gdn(qkv, ha, gconst, first_g, last_g_rev, rows_g)
        nrm = _rows8(hgrn_norm_g[l], gdn_norm_g[l])
        x = _merge(x, ha, gates, oa_f, oa_b, o_b, oc_f, oc_b, w_a[l], w_b[l], w_c[l], w_mix[l], nrm,
                   _rows8(ln_g[l, 0], ln_b[l, 0]), tm)
        kv = _matmul(mem2, _col_tiles(w_kv[l], 512), n_mem).reshape(n_seq, n_mem, 2 * d)
        x = _xattn(x, kv, w_q[l], w_o[l], _rows8(ln_g[l, 1], ln_b[l, 1]), seq_x, tm_x)
        x = _ffn(x, w_f1[l], w_f2[l], _rows8(ln_g[l, 2], ln_b[l, 2]), tm,
                 split_rows=n_prompt if l == depth - 1 else None)

    y_prompt, y_sample = x
    return (y_prompt.reshape(x_prompt.shape), y_sample.reshape(x_sample.shape))
```
